```python
import math
import jax, jax.numpy as jnp
from jax import lax
import numpy as np

D_MODEL = 1024
BATCH = 8
SEQ = 2048
DEPTH = 2

N_A_LAYERS = DEPTH // 2
N_B_LAYERS = DEPTH - N_A_LAYERS
N_META = 16

SSM_EXPAND = 2
D_INNER = SSM_EXPAND * D_MODEL
SSM_HEAD_DIM = 64
SSM_HEADS = D_INNER // SSM_HEAD_DIM
SSM_GROUPS = 4
SSM_HEADS_PER_GROUP = SSM_HEADS // SSM_GROUPS
D_STATE = 128
SSM_CONV = 4
CHUNK = 128
D_BC = SSM_GROUPS * D_STATE
D_XBC = D_INNER + 2 * D_BC
D_IN_PROJ = D_INNER + D_XBC + SSM_HEADS

ATTN_HEAD_DIM = 64
N_Q_HEADS = D_MODEL // ATTN_HEAD_DIM
N_KV_HEADS = 4
Q_PER_KV = N_Q_HEADS // N_KV_HEADS
D_ATTN = N_Q_HEADS * ATTN_HEAD_DIM
D_KV = N_KV_HEADS * ATTN_HEAD_DIM
WINDOW = 128
BLOCK = 128

D_FF = 2816
FFN_CONV = 3

RMS_EPS = 1e-6
NEG_INF = -1e30

kernel_name = "yoco_mamba2_swa_sink_hybrid"


def rms_norm(x, w):
    xf = x.astype(jnp.float32)
    y = xf * lax.rsqrt(jnp.mean(xf * xf, axis=-1, keepdims=True) + RMS_EPS)
    return (y * w.astype(jnp.float32)).astype(x.dtype)


def causal_dwconv(x, w, b):
    k, c = w.shape
    y = lax.conv_general_dilated(
        x, w[:, None, :].astype(x.dtype), window_strides=(1,), padding=[(k - 1, 0)],
        dimension_numbers=("NWC", "WIO", "NWC"), feature_group_count=c)
    return y + b.astype(x.dtype)


def ssd_chunked(xdt, a, b_in, c_in):
    bsz, t = xdt.shape[:2]
    nc = t // CHUNK
    xdt = xdt.reshape(bsz, nc, CHUNK, *xdt.shape[2:])
    a = a.reshape(bsz, nc, CHUNK, *a.shape[2:])
    bm = b_in.reshape(bsz, nc, CHUNK, *b_in.shape[2:])
    cm = c_in.reshape(bsz, nc, CHUNK, *c_in.shape[2:])
    a_cs = jnp.cumsum(a, axis=2)
    seg = a_cs[:, :, :, None] - a_cs[:, :, None, :]
    causal = jnp.tril(jnp.ones((CHUNK, CHUNK), bool))[None, None, :, :, None, None]
    decay_ls = jnp.where(causal, jnp.exp(jnp.where(causal, seg, 0.0)), 0.0)
    cb = jnp.einsum("bclgn,bcsgn->bclsg", cm, bm)
    y_diag = jnp.einsum("bclsg,bclsgk,bcsgkp->bclgkp", cb, decay_ls, xdt)
    decay_to_end = jnp.exp(a_cs[:, :, -1:] - a_cs)
    chunk_states = jnp.einsum("bclgn,bclgk,bclgkp->bcgkpn", bm, decay_to_end, xdt)
    chunk_decay = jnp.exp(a_cs[:, :, -1])

    def step(state, inp):
        st, dec = inp
        return state * dec[..., None, None] + st, state

    init = jnp.zeros_like(chunk_states[:, 0])
    _, prev = lax.scan(step, init, (jnp.moveaxis(chunk_states, 1, 0), jnp.moveaxis(chunk_decay, 1, 0)))
    prev = jnp.moveaxis(prev, 0, 1)
    y_off = jnp.einsum("bclgn,bcgkpn,bclgk->bclgkp", cm, prev, jnp.exp(a_cs))
    return (y_diag + y_off).reshape(bsz, t, *y_diag.shape[3:])


def mamba2_mixer(hn, w_in, conv_w, conv_b, dt_bias, a_log, d_skip, gate_norm, w_out):
    bsz, seq_len, _ = hn.shape
    zxbcdt = hn @ w_in
    z = zxbcdt[..., :D_INNER]
    xbc = zxbcdt[..., D_INNER:D_INNER + D_XBC]
    dt = zxbcdt[..., D_INNER + D_XBC:]
    xbc = jax.nn.silu(causal_dwconv(xbc, conv_w, conv_b))
    xs = xbc[..., :D_INNER]
    bm = xbc[..., D_INNER:D_INNER + D_BC]
    cm = xbc[..., D_INNER + D_BC:]
    dt = jax.nn.softplus(dt.astype(jnp.float32) + dt_bias.astype(jnp.float32))
    a = -jnp.exp(a_log.astype(jnp.float32)).reshape(SSM_GROUPS, SSM_HEADS_PER_GROUP)
    xh = xs.astype(jnp.float32).reshape(bsz, seq_len, SSM_GROUPS, SSM_HEADS_PER_GROUP, SSM_HEAD_DIM)
    dtg = dt.reshape(bsz, seq_len, SSM_GROUPS, SSM_HEADS_PER_GROUP)
    bg = bm.astype(jnp.float32).reshape(bsz, seq_len, SSM_GROUPS, D_STATE)
    cg = cm.astype(jnp.float32).reshape(bsz, seq_len, SSM_GROUPS, D_STATE)
    pad = CHUNK - N_META

    def padt(t):
        return jnp.pad(t, ((0, 0), (pad, 0)) + ((0, 0),) * (t.ndim - 2))

    y = ssd_chunked(padt(xh * dtg[..., None]), padt(dtg * a), padt(bg), padt(cg))[:, pad:]
    y = y + d_skip.astype(jnp.float32).reshape(SSM_GROUPS, SSM_HEADS_PER_GROUP)[:, :, None] * xh
    y = y.reshape(bsz, seq_len, D_INNER).astype(hn.dtype)
    y = rms_norm(y * jax.nn.silu(z), gate_norm)
    return y @ w_out


def swa_sink_attention(hn, w_q, k, v, sinks, w_o):
    bsz, seq_len, _ = hn.shape
    s_real = seq_len - N_META
    nb = s_real // BLOCK
    scale = 1.0 / math.sqrt(ATTN_HEAD_DIM)
    q = (hn @ w_q).reshape(bsz, seq_len, N_KV_HEADS, Q_PER_KV, ATTN_HEAD_DIM) * scale
    qm, qr = q[:, :N_META], q[:, N_META:]
    km, kr = k[:, :N_META], k[:, N_META:]
    vm, vr = v[:, :N_META], v[:, N_META:]
    sink = sinks.astype(jnp.float32).reshape(N_KV_HEADS, Q_PER_KV)

    sm = jnp.einsum("bqkgd,bskd->bkgqs", qm, km).astype(jnp.float32)
    sm = jnp.where(jnp.tril(jnp.ones((N_META, N_META), bool)), sm, NEG_INF)
    sm = jnp.concatenate([sm, jnp.broadcast_to(sink[None, :, :, None, None], sm.shape[:-1] + (1,))], -1)
    pm = jax.nn.softmax(sm, axis=-1)[..., :N_META].astype(v.dtype)
    om = jnp.einsum("bkgqs,bskd->bqkgd", pm, vm).reshape(bsz, N_META, D_ATTN)

    qb = qr.reshape(bsz, nb, BLOCK, N_KV_HEADS, Q_PER_KV, ATTN_HEAD_DIM)

    def band(t):
        tb = t.reshape(bsz, nb, BLOCK, N_KV_HEADS, ATTN_HEAD_DIM)
        prev = jnp.pad(tb, ((0, 0), (1, 0), (0, 0), (0, 0), (0, 0)))[:, :-1]
        return jnp.concatenate([prev, tb], axis=2)

    kband, vband = band(kr), band(vr)
    qi = jnp.arange(BLOCK)[:, None]
    si = jnp.arange(2 * BLOCK)[None, :]
    in_window = (si > qi + BLOCK - WINDOW) & (si <= qi + BLOCK)
    valid = (jnp.arange(nb)[:, None, None] > 0) | (si >= BLOCK)[None]
    mask = in_window[None] & valid
    s_meta = jnp.einsum("bnqkgd,bmkd->bnkgqm", qb, km).astype(jnp.float32)
    s_band = jnp.einsum("bnqkgd,bnskd->bnkgqs", qb, kband).astype(jnp.float32)
    s_band = jnp.where(mask[None, :, None, None], s_band, NEG_INF)
    s_sink = jnp.broadcast_to(sink[None, None, :, :, None, None], s_meta.shape[:-1] + (1,))
    p = jax.nn.softmax(jnp.concatenate([s_meta, s_band, s_sink], -1), axis=-1).astype(v.dtype)
    ob = (jnp.einsum("bnkgqm,bmkd->bnqkgd", p[..., :N_META], vm)
          + jnp.einsum("bnkgqs,bnskd->bnqkgd", p[..., N_META:N_META + 2 * BLOCK], vband))
    ob = ob.reshape(bsz, s_real, D_ATTN)
    return jnp.concatenate([om, ob], axis=1) @ w_o


def conv_ffn(hn, w_up, conv_w, conv_b, w_down):
    u = causal_dwconv(hn @ w_up, conv_w, conv_b)
    gate, val = u[..., :D_FF], u[..., D_FF:]
    return (jax.nn.silu(gate) * val) @ w_down


def setup_inputs(seed: int = 0) -> dict:
    key = jax.random.key(seed)
    ks = jax.random.split(key, 32)
    f32 = jnp.float32

    def nrm(k, shape, scale):
        return jax.random.normal(k, shape, f32) * scale

    def gain(k, shape):
        return 1.0 + 0.1 * jax.random.normal(k, shape, f32)

    na, nbl = N_A_LAYERS, N_B_LAYERS
    dt0 = jnp.exp(jax.random.uniform(ks[5], (na, SSM_HEADS), f32) * (math.log(0.1) - math.log(0.001)) + math.log(0.001))
    return {
        "x": jax.random.normal(ks[0], (BATCH, SEQ, D_MODEL), f32),
        "meta_tokens": nrm(ks[1], (N_META, D_MODEL), 1.0),
        "a_norm_pre": gain(ks[2], (na, D_MODEL)),
        "a_w_in": nrm(ks[3], (na, D_MODEL, D_IN_PROJ), D_MODEL ** -0.5),
        "a_conv_w": nrm(ks[4], (na, SSM_CONV, D_XBC), SSM_CONV ** -0.5),
        "a_conv_b": nrm(ks[6], (na, D_XBC), 0.02),
        "a_dt_bias": dt0 + jnp.log(-jnp.expm1(-dt0)),
        "a_a_log": jnp.log(jax.random.uniform(ks[7], (na, SSM_HEADS), f32, 1.0, 16.0)),
        "a_d_skip": gain(ks[8], (na, SSM_HEADS)),
        "a_gate_norm": gain(ks[9], (na, D_INNER)),
        "a_w_out": nrm(ks[10], (na, D_INNER, D_MODEL), D_INNER ** -0.5),
        "a_norm_post": gain(ks[11], (na, D_MODEL)),
        "kv_norm": gain(ks[12], (D_MODEL,)),
        "w_kv": nrm(ks[13], (D_MODEL, 2 * D_KV), D_MODEL ** -0.5),
        "b_norm_pre": gain(ks[14], (nbl, D_MODEL)),
        "b_w_q": nrm(ks[15], (nbl, D_MODEL, D_ATTN), D_MODEL ** -0.5),
        "b_sinks": nrm(ks[16], (nbl, N_Q_HEADS), 0.5),
        "b_w_o": nrm(ks[17], (nbl, D_ATTN, D_MODEL), D_ATTN ** -0.5),
        "b_norm_post": gain(ks[18], (nbl, D_MODEL)),
        "f_norm_pre": gain(ks[19], (DEPTH, D_MODEL)),
        "f_w_up": nrm(ks[20], (DEPTH, D_MODEL, 2 * D_FF), D_MODEL ** -0.5),
        "f_conv_w": nrm(ks[21], (DEPTH, FFN_CONV, 2 * D_FF), FFN_CONV ** -0.5),
        "f_conv_b": nrm(ks[22], (DEPTH, 2 * D_FF), 0.02),
        "f_w_down": nrm(ks[23], (DEPTH, D_FF, D_MODEL), D_FF ** -0.5),
        "f_norm_post": gain(ks[24], (DEPTH, D_MODEL)),
    }


def reference(x, meta_tokens, a_norm_pre, a_w_in, a_conv_w, a_conv_b, a_dt_bias, a_a_log, a_d_skip,
              a_gate_norm, a_w_out, a_norm_post, kv_norm, w_kv, b_norm_pre, b_w_q, b_sinks, b_w_o,
              b_norm_post, f_norm_pre, f_w_up, f_conv_w, f_conv_b, f_w_down, f_norm_post):
    bsz = x.shape[0]
    h = jnp.concatenate([jnp.broadcast_to(meta_tokens[None].astype(x.dtype), (bsz, N_META, D_MODEL)), x], axis=1)
    seq_len = h.shape[1]
    k_shared = None
    v_shared = None
    for i in range(DEPTH):
        if i < N_A_LAYERS:
            j = i
            mix = mamba2_mixer(rms_norm(h, a_norm_pre[j]), a_w_in[j], a_conv_w[j], a_conv_b[j], a_dt_bias[j],
                               a_a_log[j], a_d_skip[j], a_gate_norm[j], a_w_out[j])
            h = h + rms_norm(mix, a_norm_post[j])
        else:
            j = i - N_A_LAYERS
            if j == 0:
                kv = (rms_norm(h, kv_norm) @ w_kv).reshape(bsz, seq_len, 2, N_KV_HEADS, ATTN_HEAD_DIM)
                k_shared, v_shared = kv[:, :, 0], kv[:, :, 1]
            mix = swa_sink_attention(rms_norm(h, b_norm_pre[j]), b_w_q[j], k_shared, v_shared, b_sinks[j], b_w_o[j])
            h = h + rms_norm(mix, b_norm_post[j])
        ffn = conv_ffn(rms_norm(h, f_norm_pre[i]), f_w_up[i], f_conv_w[i], f_conv_b[i], f_w_down[i])
        h = h + rms_norm(ffn, f_norm_post[i])
    return h[:, N_META:]
```

```python
import functools
import math

import jax
import jax.numpy as jnp
from jax import lax
from jax.experimental import pallas as pl
from jax.experimental.pallas import tpu as pltpu

F32 = jnp.float32
BF16 = jnp.bfloat16

D_MODEL = 1024
N_META = 16
CHUNK = 128
PAD = CHUNK - N_META

D_INNER = 2048
SSM_HEAD_DIM = 64
SSM_HEADS = 32
SSM_GROUPS = 4
HEADS_PER_GROUP = SSM_HEADS // SSM_GROUPS
D_STATE = 128
SSM_CONV = 4
D_BC = SSM_GROUPS * D_STATE
D_XBC = D_INNER + 2 * D_BC

ATTN_HEAD_DIM = 64
N_Q_HEADS = 16
N_KV_HEADS = 4
Q_PER_KV = N_Q_HEADS // N_KV_HEADS
D_ATTN = N_Q_HEADS * ATTN_HEAD_DIM
D_KV = N_KV_HEADS * ATTN_HEAD_DIM

D_FF = 2816
FFN_CONV = 3

RMS_EPS = 1e-6
NEG_INF = -1e30

LANES = 128
HALO = 8
VMEM_LIMIT = 56 * 1024 * 1024


def _rms(x):
    return x * lax.rsqrt(jnp.mean(x * x, axis=-1, keepdims=True) + RMS_EPS)


def _silu(x):
    return x / (1.0 + jnp.exp(-x))


def _dot(a, b):
    return jnp.dot(a, b, preferred_element_type=F32)


def _dot_nt(a, b):
    return lax.dot_general(a, b, (((1,), (1,)), ((), ())), preferred_element_type=F32)


def _resident(shape):
    return pl.BlockSpec(shape, lambda *_: (0,) * len(shape), pipeline_mode=pl.Buffered(1))


def _params(n_axes):
    return pltpu.CompilerParams(dimension_semantics=("arbitrary",) * n_axes,
                                vmem_limit_bytes=VMEM_LIMIT)


def _in_proj_kernel(x_ref, g_ref, wz_ref, wx_ref, wdt_ref, z_ref, xbc_ref, dt_ref):
    xn = (_rms(x_ref[...]) * g_ref[...]).astype(BF16)
    step = 512
    for n0 in range(0, D_INNER, step):
        z_ref[:, n0:n0 + step] = _dot(xn, wz_ref[:, n0:n0 + step]).astype(BF16)
    for n0 in range(0, D_XBC, step):
        xbc_ref[:, n0:n0 + step] = _dot(xn, wx_ref[:, n0:n0 + step]).astype(BF16)
    dt_ref[...] = _dot(xn, wdt_ref[...])


def _in_proj(h, gain, wz, wx, wdt, tm=512):
    rows = h.shape[0]
    row_spec = lambda n: pl.BlockSpec((tm, n), lambda i: (i, 0))
    return pl.pallas_call(
        _in_proj_kernel,
        grid=(rows // tm,),
        in_specs=[row_spec(D_MODEL), _resident((1, D_MODEL)), _resident(wz.shape),
                  _resident(wx.shape), _resident(wdt.shape)],
        out_specs=[row_spec(D_INNER), row_spec(D_XBC), row_spec(LANES)],
        out_shape=[jax.ShapeDtypeStruct((rows, D_INNER), BF16),
                   jax.ShapeDtypeStruct((rows, D_XBC), BF16),
                   jax.ShapeDtypeStruct((rows, LANES), F32)],
        compiler_params=_params(1),
        name="mamba_in_proj",
    )(h, gain, wz, wx, wdt)


def _ssd_kernel(z_ref, xbc_ref, dt_ref, cw_ref, cb_ref, dtb_ref, alog_ref, dskip_ref, gn_ref,
                o_ref, ext_ref, state_ref, yg_ref):
    c = pl.program_id(1)

    @pl.when(c == 0)
    def _():
        ext_ref[0:HALO, :] = jnp.zeros((HALO, D_XBC), F32)
        state_ref[...] = jnp.zeros_like(state_ref)

    row = lax.broadcasted_iota(jnp.int32, (CHUNK, 1), 0)
    valid = jnp.logical_or(c > 0, row >= PAD)

    xin = jnp.where(valid, xbc_ref[...].astype(F32), 0.0)
    ext_ref[HALO:HALO + CHUNK, :] = xin
    acc = cb_ref[...] + cw_ref[SSM_CONV - 1:SSM_CONV, :] * xin
    for k in range(SSM_CONV - 1):
        s = SSM_CONV - 1 - k
        acc = acc + cw_ref[k:k + 1, :] * ext_ref[HALO - s:HALO - s + CHUNK, :]
    ext_ref[0:HALO, :] = ext_ref[CHUNK:CHUNK + HALO, :]
    xbc = _silu(acc)
    xs = xbc[:, :D_INNER]
    bm = jnp.where(valid, xbc[:, D_INNER:D_INNER + D_BC], 0.0)
    cm = jnp.where(valid, xbc[:, D_INNER + D_BC:], 0.0)
    xs_b = xs.astype(BF16)

    dt_raw = dt_ref[...] + dtb_ref[...]
    dt = jnp.maximum(dt_raw, 0.0) + jnp.log1p(jnp.exp(-jnp.abs(dt_raw)))
    dt = jnp.where(valid, dt, 0.0)
    a = dt * (-jnp.exp(alog_ref[...]))
    li = lax.broadcasted_iota(jnp.int32, (CHUNK, CHUNK), 0)
    si = lax.broadcasted_iota(jnp.int32, (CHUNK, CHUNK), 1)
    causal = li >= si
    acs = jnp.dot(causal.astype(F32), a, precision=lax.Precision.HIGHEST,
                  preferred_element_type=F32)
    expa = jnp.exp(acs)
    to_end = jnp.exp(acs[CHUNK - 1:CHUNK, :] - acs)
    acs_t = acs.T
    dt_t = dt.T
    wst_t = (to_end * dt).T
    lane_lo = lax.broadcasted_iota(jnp.int32, (CHUNK, LANES), 1) < SSM_HEAD_DIM

    def pair_cols(lo, hi):
        return jnp.where(lane_lo, lo, hi)

    ssq = jnp.zeros((CHUNK, 1), F32)
    gw = HEADS_PER_GROUP * SSM_HEAD_DIM
    for g in range(SSM_GROUPS):
        bg = bm[:, g * D_STATE:(g + 1) * D_STATE]
        cg_b = cm[:, g * D_STATE:(g + 1) * D_STATE].astype(BF16)
        cb_mat = _dot_nt(cg_b, bg.astype(BF16))
        bg_t = bg.T
        st = state_ref[:, g * gw:(g + 1) * gw]
        y_off = _dot(cg_b, st.astype(BF16))
        for pr in range(HEADS_PER_GROUP // 2):
            h0 = g * HEADS_PER_GROUP + 2 * pr
            c0 = h0 * SSM_HEAD_DIM
            xp = xs_b[:, c0:c0 + LANES]
            zero = jnp.zeros_like(xp)
            rhs = jnp.concatenate([jnp.where(lane_lo, xp, zero), jnp.where(lane_lo, zero, xp)], axis=0)
            m_parts, s_parts = [], []
            for hh in (h0, h0 + 1):
                seg = acs[:, hh:hh + 1] - acs_t[hh:hh + 1, :]
                decay = jnp.exp(jnp.where(causal, seg, NEG_INF))
                m_parts.append((cb_mat * decay * dt_t[hh:hh + 1, :]).astype(BF16))
                s_parts.append((bg_t * wst_t[hh:hh + 1, :]).astype(BF16))
            y_diag = _dot(jnp.concatenate(m_parts, axis=1), rhs)
            st_new = _dot(jnp.concatenate(s_parts, axis=1), rhs)
            e_pair = pair_cols(expa[:, h0:h0 + 1], expa[:, h0 + 1:h0 + 2])
            cdec = pair_cols(expa[CHUNK - 1:CHUNK, h0:h0 + 1], expa[CHUNK - 1:CHUNK, h0 + 1:h0 + 2])
            pc = pr * LANES
            state_ref[:, g * gw + pc:g * gw + pc + LANES] = st[:, pc:pc + LANES] * cdec + st_new
            y = y_diag + y_off[:, pc:pc + LANES] * e_pair + dskip_ref[:, c0:c0 + LANES] * xs[:, c0:c0 + LANES]
            y = y * _silu(z_ref[:, c0:c0 + LANES].astype(F32))
            ssq = ssq + jnp.sum(y * y, axis=-1, keepdims=True)
            yg_ref[:, c0:c0 + LANES] = y
    inv = lax.rsqrt(ssq * (1.0 / D_INNER) + RMS_EPS)
    o_ref[...] = (yg_ref[...] * inv * gn_ref[...]).astype(BF16)


def _ssd(z, xbc, dt, conv_w, conv_b, dt_bias, a_log, d_skip, gate_norm, bsz, nc):
    rows = z.shape[0]
    row_spec = lambda n: pl.BlockSpec((CHUNK, n), lambda b, c: (b * nc + c, 0))
    return pl.pallas_call(
        _ssd_kernel,
        grid=(bsz, nc),
        in_specs=[row_spec(D_INNER), row_spec(D_XBC), row_spec(LANES),
                  _resident(conv_w.shape), _resident(conv_b.shape), _resident(dt_bias.shape),
                  _resident(a_log.shape), _resident(d_skip.shape), _resident(gate_norm.shape)],
        out_specs=row_spec(D_INNER),
        out_shape=jax.ShapeDtypeStruct((rows, D_INNER), BF16),
        scratch_shapes=[pltpu.VMEM((HALO + CHUNK, D_XBC), F32),
                        pltpu.VMEM((D_STATE, D_INNER), F32),
                        pltpu.VMEM((CHUNK, D_INNER), F32)],
        compiler_params=_params(2),
        name="mamba_ssd",
    )(z, xbc, dt, conv_w, conv_b, dt_bias, a_log, d_skip, gate_norm)


def _proj_res_kernel(x_ref, w_ref, g_ref, r_ref, o_ref):
    y = _dot(x_ref[...], w_ref[...])
    o_ref[...] = r_ref[...] + _rms(y) * g_ref[...]


def _proj_res(x, w, gain, res, tm=512):
    rows, k = x.shape
    return pl.pallas_call(
        _proj_res_kernel,
        grid=(rows // tm,),
        in_specs=[pl.BlockSpec((tm, k), lambda i: (i, 0)), _resident(w.shape), _resident(gain.shape),
                  pl.BlockSpec((tm, D_MODEL), lambda i: (i, 0))],
        out_specs=pl.BlockSpec((tm, D_MODEL), lambda i: (i, 0)),
        out_shape=jax.ShapeDtypeStruct((rows, D_MODEL), F32),
        compiler_params=_params(1),
        name="proj_norm_residual",
    )(x, w, gain, res)


def _ffn_kernel(x_ref, gpre_ref, wg_ref, wv_ref, cw_ref, cb_ref, wd_ref, gpost_ref, o_ref,
                ext_ref, carry_ref, act_ref, *, tiles_per_seq, tm, tn):
    first = (pl.program_id(0) % tiles_per_seq) == 0
    x = x_ref[...]
    xn = (_rms(x) * gpre_ref[...]).astype(BF16)
    row = lax.broadcasted_iota(jnp.int32, (tm, 1), 0)
    valid = jnp.logical_or(jnp.logical_not(first), row >= PAD)

    def up_conv(w_ref, col):
        u = jnp.where(valid, _dot(xn, w_ref[:, col % D_FF:col % D_FF + tn]), 0.0)
        ext_ref[0:HALO, :] = jnp.where(first, 0.0, carry_ref[:, col:col + tn])
        ext_ref[HALO:HALO + tm, :] = u
        carry_ref[:, col:col + tn] = u[tm - HALO:tm, :]
        out = cb_ref[:, col:col + tn] + cw_ref[FFN_CONV - 1:FFN_CONV, col:col + tn] * u
        for k in range(FFN_CONV - 1):
            s = FFN_CONV - 1 - k
            out = out + cw_ref[k:k + 1, col:col + tn] * ext_ref[HALO - s:HALO - s + tm, :]
        return out

    for j in range(D_FF // tn):
        gate = up_conv(wg_ref, j * tn)
        val = up_conv(wv_ref, D_FF + j * tn)
        act_ref[:, j * tn:(j + 1) * tn] = (_silu(gate) * val).astype(BF16)
    y = _dot(act_ref[...], wd_ref[...])
    o_ref[...] = x + _rms(y) * gpost_ref[...]


def _ffn(h, gpre, wg, wv, conv_w, conv_b, wd, gpost, seq_rows, tm=544, tn=256):
    rows = h.shape[0]
    assert seq_rows % tm == 0 and D_FF % tn == 0
    body = functools.partial(_ffn_kernel, tiles_per_seq=seq_rows // tm, tm=tm, tn=tn)
    return pl.pallas_call(
        body,
        grid=(rows // tm,),
        in_specs=[pl.BlockSpec((tm, D_MODEL), lambda i: (i, 0)), _resident(gpre.shape),
                  _resident(wg.shape), _resident(wv.shape), _resident(conv_w.shape),
                  _resident(conv_b.shape), _resident(wd.shape), _resident(gpost.shape)],
        out_specs=pl.BlockSpec((tm, D_MODEL), lambda i: (i, 0)),
        out_shape=jax.ShapeDtypeStruct((rows, D_MODEL), F32),
        scratch_shapes=[pltpu.VMEM((HALO + tm, tn), F32),
                        pltpu.VMEM((HALO, 2 * D_FF), F32),
                        pltpu.VMEM((tm, D_FF), BF16)],
        compiler_params=_params(1),
        name="conv_ffn",
    )(h, gpre, wg, wv, conv_w, conv_b, wd, gpost)


def _qkv_kernel(x_ref, gq_ref, gkv_ref, wq_ref, wkv_ref, q_ref, kv_ref):
    xr = _rms(x_ref[...])
    scale = 1.0 / math.sqrt(ATTN_HEAD_DIM)
    q_ref[...] = (_dot((xr * gq_ref[...]).astype(BF16), wq_ref[...]) * scale).astype(BF16)
    kv_ref[...] = _dot((xr * gkv_ref[...]).astype(BF16), wkv_ref[...]).astype(BF16)


def _qkv(h, gq, gkv, wq, wkv, tm=512):
    rows = h.shape[0]
    row_spec = lambda n: pl.BlockSpec((tm, n), lambda i: (i, 0))
    return pl.pallas_call(
        _qkv_kernel,
        grid=(rows // tm,),
        in_specs=[row_spec(D_MODEL), _resident(gq.shape), _resident(gkv.shape),
                  _resident(wq.shape), _resident(wkv.shape)],
        out_specs=[row_spec(D_ATTN), row_spec(2 * D_KV)],
        out_shape=[jax.ShapeDtypeStruct((rows, D_ATTN), BF16),
                   jax.ShapeDtypeStruct((rows, 2 * D_KV), BF16)],
        compiler_params=_params(1),
        name="q_kv_proj",
    )(h, gq, gkv, wq, wkv)


def _attn_kernel(sink_ref, q_ref, kc_ref, kp_ref, km_ref, o_ref):
    c = pl.program_id(1)
    qi = lax.broadcasted_iota(jnp.int32, (CHUNK, CHUNK), 0)
    ki = lax.broadcasted_iota(jnp.int32, (CHUNK, CHUNK), 1)
    real_key = ki >= PAD
    mask_meta = jnp.logical_and(c >= 1, real_key)
    mask_prev = jnp.logical_and(c >= 2, ki > qi)
    mask_cur = jnp.logical_and(ki <= qi, jnp.logical_or(c >= 1, real_key))
    parts = ((km_ref, mask_meta), (kp_ref, mask_prev), (kc_ref, mask_cur))
    for hq in range(N_Q_HEADS):
        kh = hq // Q_PER_KV
        q = q_ref[:, hq * ATTN_HEAD_DIM:(hq + 1) * ATTN_HEAD_DIM]
        sink = sink_ref[hq]
        scores = []
        m = jnp.full((CHUNK, 1), sink, F32)
        for ref, mask in parts:
            k = ref[:, kh * ATTN_HEAD_DIM:(kh + 1) * ATTN_HEAD_DIM]
            s = jnp.where(mask, _dot_nt(q, k), NEG_INF)
            m = jnp.maximum(m, jnp.max(s, axis=-1, keepdims=True))
            scores.append(s)
        den = jnp.exp(sink - m)
        out = jnp.zeros((CHUNK, ATTN_HEAD_DIM), F32)
        for (ref, _), s in zip(parts, scores):
            v = ref[:, D_KV + kh * ATTN_HEAD_DIM:D_KV + (kh + 1) * ATTN_HEAD_DIM]
            p = jnp.exp(s - m)
            den = den + jnp.sum(p, axis=-1, keepdims=True)
            out = out + _dot(p.astype(BF16), v)
        o_ref[:, hq * ATTN_HEAD_DIM:(hq + 1) * ATTN_HEAD_DIM] = (out / den).astype(BF16)


def _attn(q, kv, sinks, bsz, nc):
    rows = q.shape[0]
    kv_spec = lambda f: pl.BlockSpec((CHUNK, 2 * D_KV), f)
    return pl.pallas_call(
        _attn_kernel,
        grid=(bsz, nc),
        in_specs=[pl.BlockSpec(memory_space=pltpu.SMEM),
                  pl.BlockSpec((CHUNK, D_ATTN), lambda b, c: (b * nc + c, 0)),
                  kv_spec(lambda b, c: (b * nc + c, 0)),
                  kv_spec(lambda b, c: (b * nc + jnp.maximum(c - 1, 0), 0)),
                  kv_spec(lambda b, c: (b * nc, 0))],
        out_specs=pl.BlockSpec((CHUNK, D_ATTN), lambda b, c: (b * nc + c, 0)),
        out_shape=jax.ShapeDtypeStruct((rows, D_ATTN), BF16),
        compiler_params=_params(2),
        name="swa_sink_attention",
    )(sinks, q, kv, kv, kv)


def kernel(x, meta_tokens, a_norm_pre, a_w_in, a_conv_w, a_conv_b, a_dt_bias, a_a_log, a_d_skip, a_gate_norm, a_w_out, a_norm_post, kv_norm, w_kv, b_norm_pre, b_w_q, b_sinks, b_w_o, b_norm_post, f_norm_pre, f_w_up, f_conv_w, f_conv_b, f_w_down, f_norm_post):
    bsz, seq, _ = x.shape
    assert seq % CHUNK == 0
    nc = seq // CHUNK + 1
    seq_rows = nc * CHUNK
    depth = f_norm_pre.shape[0]
    n_a = a_norm_pre.shape[0]

    def row(v):
        return v.astype(F32).reshape(1, -1)

    def lane_pad(v):
        return jnp.pad(v, ((0, 0), (0, LANES - v.shape[1])))

    head = jnp.concatenate([jnp.zeros((PAD, D_MODEL), x.dtype), meta_tokens.astype(x.dtype)], axis=0)
    h = jnp.concatenate([jnp.broadcast_to(head[None], (bsz, CHUNK, D_MODEL)), x], axis=1)
    h = h.reshape(bsz * seq_rows, D_MODEL)

    kv = None
    for i in range(depth):
        if i < n_a:
            j = i
            w_in = a_w_in[j]
            z, xbc, dt = _in_proj(h, row(a_norm_pre[j]), w_in[:, :D_INNER].astype(BF16),
                                  w_in[:, D_INNER:D_INNER + D_XBC].astype(BF16),
                                  lane_pad(w_in[:, D_INNER + D_XBC:]).astype(BF16))
            y = _ssd(z, xbc, dt, a_conv_w[j].astype(F32), row(a_conv_b[j]), lane_pad(row(a_dt_bias[j])),
                     lane_pad(row(a_a_log[j])), row(jnp.repeat(a_d_skip[j], SSM_HEAD_DIM)),
                     row(a_gate_norm[j]), bsz, nc)
            h = _proj_res(y, a_w_out[j].astype(BF16), row(a_norm_post[j]), h)
        else:
            j = i - n_a
            if j == 0:
                kv_gain = row(kv_norm)
            q, kv_new = _qkv(h, row(b_norm_pre[j]), kv_gain, b_w_q[j].astype(BF16), w_kv.astype(BF16))
            if j == 0:
                kv = kv_new
            att = _attn(q, kv, b_sinks[j].astype(F32), bsz, nc)
            h = _proj_res(att, b_w_o[j].astype(BF16), row(b_norm_post[j]), h)
        w_up = f_w_up[i]
        h = _ffn(h, row(f_norm_pre[i]), w_up[:, :D_FF].astype(BF16), w_up[:, D_FF:].astype(BF16),
                 f_conv_w[i].astype(F32), row(f_conv_b[i]), f_w_down[i].astype(BF16),
                 row(f_norm_post[i]), seq_rows)
    return h.reshape(bsz, seq_rows, D_MODEL)[:, CHUNK:]
```

```python
import functools
import math

import jax
import jax.numpy as jnp
from jax import lax
from jax.experimental import pallas as pl
from jax.experimental.pallas import tpu as pltpu

F32 = jnp.float32
BF16 = jnp.bfloat16

D_MODEL = 1024
N_META = 16
CHUNK = 128
PAD = CHUNK - N_META

D_INNER = 2048
SSM_HEAD_DIM = 64
SSM_HEADS = 32
SSM_GROUPS = 4
HEADS_PER_GROUP = SSM_HEADS // SSM_GROUPS
D_STATE = 128
SSM_CONV = 4
D_BC = SSM_GROUPS * D_STATE
D_XBC = D_INNER + 2 * D_BC

ATTN_HEAD_DIM = 64
N_Q_HEADS = 16
N_KV_HEADS = 4
Q_PER_KV = N_Q_HEADS // N_KV_HEADS
D_ATTN = N_Q_HEADS * ATTN_HEAD_DIM
D_KV = N_KV_HEADS * ATTN_HEAD_DIM

D_FF = 2816
FFN_CONV = 3

RMS_EPS = 1e-6
NEG_INF = -1e30

LANES = 128
HALO = 8
VMEM_LIMIT = 56 * 1024 * 1024


def _rms(x):
    return x * lax.rsqrt(jnp.mean(x * x, axis=-1, keepdims=True) + RMS_EPS)


def _silu(x):
    return x / (1.0 + jnp.exp(-x))


def _dot(a, b):
    return jnp.dot(a, b, preferred_element_type=F32)


def _dot_nt(a, b):
    return lax.dot_general(a, b, (((1,), (1,)), ((), ())), preferred_element_type=F32)


def _resident(shape):
    return pl.BlockSpec(shape, lambda *_: (0,) * len(shape), pipeline_mode=pl.Buffered(1))


def _params(n_axes):
    return pltpu.CompilerParams(dimension_semantics=("arbitrary",) * n_axes,
                                vmem_limit_bytes=VMEM_LIMIT)


def _in_proj_kernel(x_ref, g_ref, wz_ref, wx_ref, wdt_ref, z_ref, xbc_ref, dt_ref):
    xn = (_rms(x_ref[...]) * g_ref[...]).astype(BF16)
    step = 512
    for n0 in range(0, D_INNER, step):
        z_ref[:, n0:n0 + step] = _dot(xn, wz_ref[:, n0:n0 + step]).astype(BF16)
    for n0 in range(0, D_XBC, step):
        xbc_ref[:, n0:n0 + step] = _dot(xn, wx_ref[:, n0:n0 + step]).astype(BF16)
    dt_ref[...] = _dot(xn, wdt_ref[...])


def _in_proj(h, gain, wz, wx, wdt, tm=512):
    rows = h.shape[0]
    row_spec = lambda n: pl.BlockSpec((tm, n), lambda i: (i, 0))
    return pl.pallas_call(
        _in_proj_kernel,
        grid=(rows // tm,),
        in_specs=[row_spec(D_MODEL), _resident((1, D_MODEL)), _resident(wz.shape),
                  _resident(wx.shape), _resident(wdt.shape)],
        out_specs=[row_spec(D_INNER), row_spec(D_XBC), row_spec(LANES)],
        out_shape=[jax.ShapeDtypeStruct((rows, D_INNER), BF16),
                   jax.ShapeDtypeStruct((rows, D_XBC), BF16),
                   jax.ShapeDtypeStruct((rows, LANES), F32)],
        compiler_params=_params(1),
        name="mamba_in_proj",
    )(h, gain, wz, wx, wdt)


def _ssd_kernel(z_ref, xbc_ref, dt_ref, cw_ref, cb_ref, dtb_ref, alog_ref, dskip_ref, gn_ref,
                o_ref, ext_ref, state_ref, yg_ref):
    c = pl.program_id(1)

    @pl.when(c == 0)
    def _():
        ext_ref[0:HALO, :] = jnp.zeros((HALO, D_XBC), F32)
        state_ref[...] = jnp.zeros_like(state_ref)

    row = lax.broadcasted_iota(jnp.int32, (CHUNK, 1), 0)
    valid = jnp.logical_or(c > 0, row >= PAD)

    xin = jnp.where(valid, xbc_ref[...].astype(F32), 0.0)
    ext_ref[HALO:HALO + CHUNK, :] = xin
    acc = cb_ref[...] + cw_ref[SSM_CONV - 1:SSM_CONV, :] * xin
    for k in range(SSM_CONV - 1):
        s = SSM_CONV - 1 - k
        acc = acc + cw_ref[k:k + 1, :] * ext_ref[HALO - s:HALO - s + CHUNK, :]
    ext_ref[0:HALO, :] = ext_ref[CHUNK:CHUNK + HALO, :]
    xbc = _silu(acc)
    xs = xbc[:, :D_INNER]
    bm = jnp.where(valid, xbc[:, D_INNER:D_INNER + D_BC], 0.0)
    cm = jnp.where(valid, xbc[:, D_INNER + D_BC:], 0.0)
    xs_b = xs.astype(BF16)

    dt_raw = dt_ref[...] + dtb_ref[...]
    dt = jnp.maximum(dt_raw, 0.0) + jnp.log1p(jnp.exp(-jnp.abs(dt_raw)))
    dt = jnp.where(valid, dt, 0.0)
    a = dt * (-jnp.exp(alog_ref[...]))
    li = lax.broadcasted_iota(jnp.int32, (CHUNK, CHUNK), 0)
    si = lax.broadcasted_iota(jnp.int32, (CHUNK, CHUNK), 1)
    causal = li >= si
    acs = jnp.dot(causal.astype(F32), a, precision=lax.Precision.HIGHEST,
                  preferred_element_type=F32)
    expa = jnp.exp(acs)
    to_end = jnp.exp(acs[CHUNK - 1:CHUNK, :] - acs)
    acs_t = acs.T
    dt_t = dt.T
    wst_t = (to_end * dt).T
    lane_lo = lax.broadcasted_iota(jnp.int32, (CHUNK, LANES), 1) < SSM_HEAD_DIM

    def pair_cols(lo, hi):
        return jnp.where(lane_lo, lo, hi)

    ssq = jnp.zeros((CHUNK, 1), F32)
    gw = HEADS_PER_GROUP * SSM_HEAD_DIM
    for g in range(SSM_GROUPS):
        bg = bm[:, g * D_STATE:(g + 1) * D_STATE]
        cg_b = cm[:, g * D_STATE:(g + 1) * D_STATE].astype(BF16)
        cb_mat = _dot_nt(cg_b, bg.astype(BF16))
        bg_t = bg.T
        st = state_ref[:, g * gw:(g + 1) * gw]
        y_off = _dot(cg_b, st.astype(BF16))
        for pr in range(HEADS_PER_GROUP // 2):
            h0 = g * HEADS_PER_GROUP + 2 * pr
            c0 = h0 * SSM_HEAD_DIM
            xp = xs_b[:, c0:c0 + LANES]
            zero = jnp.zeros_like(xp)
            rhs = jnp.concatenate([jnp.where(lane_lo, xp, zero), jnp.where(lane_lo, zero, xp)], axis=0)
            m_parts, s_parts = [], []
            for hh in (h0, h0 + 1):
                seg = acs[:, hh:hh + 1] - acs_t[hh:hh + 1, :]
                decay = jnp.exp(jnp.where(causal, seg, NEG_INF))
                m_parts.append((cb_mat * decay * dt_t[hh:hh + 1, :]).astype(BF16))
                s_parts.append((bg_t * wst_t[hh:hh + 1, :]).astype(BF16))
            y_diag = _dot(jnp.concatenate(m_parts, axis=1), rhs)
            st_new = _dot(jnp.concatenate(s_parts, axis=1), rhs)
            e_pair = pair_cols(expa[:, h0:h0 + 1], expa[:, h0 + 1:h0 + 2])
            cdec = pair_cols(expa[CHUNK - 1:CHUNK, h0:h0 + 1], expa[CHUNK - 1:CHUNK, h0 + 1:h0 + 2])
            pc = pr * LANES
            state_ref[:, g * gw + pc:g * gw + pc + LANES] = st[:, pc:pc + LANES] * cdec + st_new
            y = y_diag + y_off[:, pc:pc + LANES] * e_pair + dskip_ref[:, c0:c0 + LANES] * xs[:, c0:c0 + LANES]
            y = y * _silu(z_ref[:, c0:c0 + LANES].astype(F32))
            ssq = ssq + jnp.sum(y * y, axis=-1, keepdims=True)
            yg_ref[:, c0:c0 + LANES] = y
    inv = lax.rsqrt(ssq * (1.0 / D_INNER) + RMS_EPS)
    o_ref[...] = (yg_ref[...] * inv * gn_ref[...]).astype(BF16)


def _ssd(z, xbc, dt, conv_w, conv_b, dt_bias, a_log, d_skip, gate_norm, bsz, nc):
    rows = z.shape[0]
    row_spec = lambda n: pl.BlockSpec((CHUNK, n), lambda b, c: (b * nc + c, 0))
    return pl.pallas_call(
        _ssd_kernel,
        grid=(bsz, nc),
        in_specs=[row_spec(D_INNER), row_spec(D_XBC), row_spec(LANES),
                  _resident(conv_w.shape), _resident(conv_b.shape), _resident(dt_bias.shape),
                  _resident(a_log.shape), _resident(d_skip.shape), _resident(gate_norm.shape)],
        out_specs=row_spec(D_INNER),
        out_shape=jax.ShapeDtypeStruct((rows, D_INNER), BF16),
        scratch_shapes=[pltpu.VMEM((HALO + CHUNK, D_XBC), F32),
                        pltpu.VMEM((D_STATE, D_INNER), F32),
                        pltpu.VMEM((CHUNK, D_INNER), F32)],
        compiler_params=_params(2),
        name="mamba_ssd",
    )(z, xbc, dt, conv_w, conv_b, dt_bias, a_log, d_skip, gate_norm)


def _proj_res_kernel(x_ref, w_ref, g_ref, r_ref, o_ref):
    y = _dot(x_ref[...], w_ref[...])
    o_ref[...] = r_ref[...] + _rms(y) * g_ref[...]


def _proj_res(x, w, gain, res, tm=512):
    rows, k = x.shape
    return pl.pallas_call(
        _proj_res_kernel,
        grid=(rows // tm,),
        in_specs=[pl.BlockSpec((tm, k), lambda i: (i, 0)), _resident(w.shape), _resident(gain.shape),
                  pl.BlockSpec((tm, D_MODEL), lambda i: (i, 0))],
        out_specs=pl.BlockSpec((tm, D_MODEL), lambda i: (i, 0)),
        out_shape=jax.ShapeDtypeStruct((rows, D_MODEL), F32),
        compiler_params=_params(1),
        name="proj_norm_residual",
    )(x, w, gain, res)


def _ffn_kernel(x_ref, gpre_ref, wg_ref, wv_ref, cw_ref, cb_ref, wd_ref, gpost_ref, o_ref,
                ext_ref, carry_ref, act_ref, *, tiles_per_seq, tm, tn):
    first = (pl.program_id(0) % tiles_per_seq) == 0
    x = x_ref[...]
    xn = (_rms(x) * gpre_ref[...]).astype(BF16)
    row = lax.broadcasted_iota(jnp.int32, (tm, 1), 0)
    valid = jnp.logical_or(jnp.logical_not(first), row >= PAD)

    def up_conv(w_ref, col):
        u = jnp.where(valid, _dot(xn, w_ref[:, col % D_FF:col % D_FF + tn]), 0.0)
        ext_ref[0:HALO, :] = jnp.where(first, 0.0, carry_ref[:, col:col + tn])
        ext_ref[HALO:HALO + tm, :] = u
        carry_ref[:, col:col + tn] = u[tm - HALO:tm, :]
        out = cb_ref[:, col:col + tn] + cw_ref[FFN_CONV - 1:FFN_CONV, col:col + tn] * u
        for k in range(FFN_CONV - 1):
            s = FFN_CONV - 1 - k
            out = out + cw_ref[k:k + 1, col:col + tn] * ext_ref[HALO - s:HALO - s + tm, :]
        return out

    for j in range(D_FF // tn):
        gate = up_conv(wg_ref, j * tn)
        val = up_conv(wv_ref, D_FF + j * tn)
        act_ref[:, j * tn:(j + 1) * tn] = (_silu(gate) * val).astype(BF16)
    y = _dot(act_ref[...], wd_ref[...])
    o_ref[...] = x + _rms(y) * gpost_ref[...]


def _ffn(h, gpre, wg, wv, conv_w, conv_b, wd, gpost, seq_rows, tm=544, tn=256):
    rows = h.shape[0]
    assert seq_rows % tm == 0 and D_FF % tn == 0
    body = functools.partial(_ffn_kernel, tiles_per_seq=seq_rows // tm, tm=tm, tn=tn)
    return pl.pallas_call(
        body,
        grid=(rows // tm,),
        in_specs=[pl.BlockSpec((tm, D_MODEL), lambda i: (i, 0)), _resident(gpre.shape),
                  _resident(wg.shape), _resident(wv.shape), _resident(conv_w.shape),
                  _resident(conv_b.shape), _resident(wd.shape), _resident(gpost.shape)],
        out_specs=pl.BlockSpec((tm, D_MODEL), lambda i: (i, 0)),
        out_shape=jax.ShapeDtypeStruct((rows, D_MODEL), F32),
        scratch_shapes=[pltpu.VMEM((HALO + tm, tn), F32),
                        pltpu.VMEM((HALO, 2 * D_FF), F32),
                        pltpu.VMEM((tm, D_FF), BF16)],
        compiler_params=_params(1),
        name="conv_ffn",
    )(h, gpre, wg, wv, conv_w, conv_b, wd, gpost)


def _q_proj_kernel(x_ref, gq_ref, wqt_ref, qt_ref):
    xq = (_rms(x_ref[...]) * gq_ref[...]).astype(BF16)
    scale = 1.0 / math.sqrt(ATTN_HEAD_DIM)
    qt_ref[...] = (_dot_nt(wqt_ref[...], xq) * scale).astype(BF16)


def _q_proj(h, gq, wqt, tm=512):
    rows = h.shape[0]
    return pl.pallas_call(
        _q_proj_kernel,
        grid=(rows // tm,),
        in_specs=[pl.BlockSpec((tm, D_MODEL), lambda i: (i, 0)), _resident(gq.shape), _resident(wqt.shape)],
        out_specs=pl.BlockSpec((D_ATTN, tm), lambda i: (0, i)),
        out_shape=jax.ShapeDtypeStruct((D_ATTN, rows), BF16),
        compiler_params=_params(1),
        name="q_proj",
    )(h, gq, wqt)


def _kv_proj_kernel(x_ref, g_ref, wk_ref, wvt_ref, k_ref, vt_ref):
    xn = (_rms(x_ref[...]) * g_ref[...]).astype(BF16)
    k_ref[...] = _dot(xn, wk_ref[...]).astype(BF16)
    vt_ref[...] = _dot_nt(wvt_ref[...], xn).astype(BF16)


def _kv_proj(h, g, wk, wvt, tm=512):
    rows = h.shape[0]
    return pl.pallas_call(
        _kv_proj_kernel,
        grid=(rows // tm,),
        in_specs=[pl.BlockSpec((tm, D_MODEL), lambda i: (i, 0)), _resident(g.shape),
                  _resident(wk.shape), _resident(wvt.shape)],
        out_specs=[pl.BlockSpec((tm, D_KV), lambda i: (i, 0)), pl.BlockSpec((D_KV, tm), lambda i: (0, i))],
        out_shape=[jax.ShapeDtypeStruct((rows, D_KV), BF16), jax.ShapeDtypeStruct((D_KV, rows), BF16)],
        compiler_params=_params(1),
        name="kv_proj",
    )(h, g, wk, wvt)


def _attn_kernel(sink_ref, qt_ref, kc_ref, kp_ref, km_ref, vc_ref, vp_ref, vm_ref, ot_ref):
    c = pl.program_id(1)
    nkeys = N_META + 2 * CHUNK
    ki = lax.broadcasted_iota(jnp.int32, (nkeys, CHUNK), 0)
    qi = lax.broadcasted_iota(jnp.int32, (nkeys, CHUNK), 1)
    kprev = ki - N_META
    kcur = ki - (N_META + CHUNK)
    vis_meta = jnp.logical_and(ki < N_META, c >= 1)
    vis_prev = jnp.logical_and(jnp.logical_and(kprev >= 0, kprev < CHUNK), jnp.logical_and(kprev > qi, c >= 2))
    vis_cur = jnp.logical_and(jnp.logical_and(kcur >= 0, kcur <= qi), jnp.logical_or(c >= 1, kcur >= PAD))
    visible = jnp.logical_or(jnp.logical_or(vis_meta, vis_prev), vis_cur)
    bias = jnp.where(visible, 0.0, NEG_INF)
    bias = jnp.concatenate([bias] * Q_PER_KV, axis=1)
    k_all = jnp.concatenate([km_ref[PAD:CHUNK, :], kp_ref[...], kc_ref[...]], axis=0)
    pad_rows = jnp.zeros((PAD, Q_PER_KV * CHUNK), BF16)
    for kh in range(N_KV_HEADS):
        heads = range(kh * Q_PER_KV, (kh + 1) * Q_PER_KV)
        qs = jnp.concatenate([qt_ref[hq * ATTN_HEAD_DIM:(hq + 1) * ATTN_HEAD_DIM, :] for hq in heads], axis=1)
        zq = jnp.zeros_like(qs)
        qpad = jnp.concatenate([qs, zq] if kh % 2 == 0 else [zq, qs], axis=0)
        kk = k_all[:, (kh // 2) * LANES:(kh // 2 + 1) * LANES]
        s = _dot(kk, qpad) + bias
        sink = jnp.concatenate([jnp.full((1, CHUNK), sink_ref[hq], F32) for hq in heads], axis=1)
        m = jnp.maximum(jnp.max(s, axis=0, keepdims=True), sink)
        p = jnp.exp(s - m)
        den = jnp.sum(p, axis=0, keepdims=True) + jnp.exp(sink - m)
        rows = slice(kh * ATTN_HEAD_DIM, (kh + 1) * ATTN_HEAD_DIM)
        vt = jnp.concatenate([vm_ref[rows, :], vp_ref[rows, :], vc_ref[rows, :]], axis=1)
        p_all = jnp.concatenate([pad_rows, p.astype(BF16)], axis=0)
        o = _dot(vt, p_all) / den
        for g, hq in enumerate(heads):
            ot_ref[hq * ATTN_HEAD_DIM:(hq + 1) * ATTN_HEAD_DIM, :] = o[:, g * CHUNK:(g + 1) * CHUNK].astype(BF16)


def _attn(qt, k, vt, sinks, bsz, nc):
    rows = qt.shape[1]
    cur = lambda b, c: b * nc + c
    prev = lambda b, c: b * nc + jnp.maximum(c - 1, 0)
    meta = lambda b, c: b * nc
    k_spec = lambda f: pl.BlockSpec((CHUNK, D_KV), lambda b, c: (f(b, c), 0))
    vt_spec = lambda f: pl.BlockSpec((D_KV, CHUNK), lambda b, c: (0, f(b, c)))
    return pl.pallas_call(
        _attn_kernel,
        grid=(bsz, nc),
        in_specs=[pl.BlockSpec(memory_space=pltpu.SMEM),
                  pl.BlockSpec((D_ATTN, CHUNK), lambda b, c: (0, cur(b, c))),
                  k_spec(cur), k_spec(prev), k_spec(meta), vt_spec(cur), vt_spec(prev), vt_spec(meta)],
        out_specs=pl.BlockSpec((D_ATTN, CHUNK), lambda b, c: (0, cur(b, c))),
        out_shape=jax.ShapeDtypeStruct((D_ATTN, rows), BF16),
        compiler_params=_params(2),
        name="swa_sink_attention",
    )(sinks, qt, k, k, k, vt, vt, vt)


def _tproj_res_kernel(xt_ref, w_ref, g_ref, r_ref, o_ref):
    y = lax.dot_general(xt_ref[...], w_ref[...], (((0,), (0,)), ((), ())), preferred_element_type=F32)
    o_ref[...] = r_ref[...] + _rms(y) * g_ref[...]


def _tproj_res(xt, w, gain, res, tm=512):
    k, rows = xt.shape
    return pl.pallas_call(
        _tproj_res_kernel,
        grid=(rows // tm,),
        in_specs=[pl.BlockSpec((k, tm), lambda i: (0, i)), _resident(w.shape), _resident(gain.shape),
                  pl.BlockSpec((tm, D_MODEL), lambda i: (i, 0))],
        out_specs=pl.BlockSpec((tm, D_MODEL), lambda i: (i, 0)),
        out_shape=jax.ShapeDtypeStruct((rows, D_MODEL), F32),
        compiler_params=_params(1),
        name="tproj_norm_residual",
    )(xt, w, gain, res)


def kernel(x, meta_tokens, a_norm_pre, a_w_in, a_conv_w, a_conv_b, a_dt_bias, a_a_log, a_d_skip, a_gate_norm, a_w_out, a_norm_post, kv_norm, w_kv, b_norm_pre, b_w_q, b_sinks, b_w_o, b_norm_post, f_norm_pre, f_w_up, f_conv_w, f_conv_b, f_w_down, f_norm_post):
    bsz, seq, _ = x.shape
    assert seq % CHUNK == 0
    nc = seq // CHUNK + 1
    seq_rows = nc * CHUNK
    depth = f_norm_pre.shape[0]
    n_a = a_norm_pre.shape[0]

    def row(v):
        return v.astype(F32).reshape(1, -1)

    def lane_pad(v):
        return jnp.pad(v, ((0, 0), (0, LANES - v.shape[1])))

    head = jnp.concatenate([jnp.zeros((PAD, D_MODEL), x.dtype), meta_tokens.astype(x.dtype)], axis=0)
    h = jnp.concatenate([jnp.broadcast_to(head[None], (bsz, CHUNK, D_MODEL)), x], axis=1)
    h = h.reshape(bsz * seq_rows, D_MODEL)

    kv = None
    for i in range(depth):
        if i < n_a:
            j = i
            w_in = a_w_in[j]
            z, xbc, dt = _in_proj(h, row(a_norm_pre[j]), w_in[:, :D_INNER].astype(BF16),
                                  w_in[:, D_INNER:D_INNER + D_XBC].astype(BF16),
                                  lane_pad(w_in[:, D_INNER + D_XBC:]).astype(BF16))
            y = _ssd(z, xbc, dt, a_conv_w[j].astype(F32), row(a_conv_b[j]), lane_pad(row(a_dt_bias[j])),
                     lane_pad(row(a_a_log[j])), row(jnp.repeat(a_d_skip[j], SSM_HEAD_DIM)),
                     row(a_gate_norm[j]), bsz, nc)
            h = _proj_res(y, a_w_out[j].astype(BF16), row(a_norm_post[j]), h)
        else:
            j = i - n_a
            if j == 0:
                kv = _kv_proj(h, row(kv_norm), w_kv[:, :D_KV].astype(BF16), w_kv[:, D_KV:].T.astype(BF16))
            qt = _q_proj(h, row(b_norm_pre[j]), b_w_q[j].T.astype(BF16))
            att_t = _attn(qt, kv[0], kv[1], b_sinks[j].astype(F32), bsz, nc)
            h = _tproj_res(att_t, b_w_o[j].astype(BF16), row(b_norm_post[j]), h)
        w_up = f_w_up[i]
        h = _ffn(h, row(f_norm_pre[i]), w_up[:, :D_FF].astype(BF16), w_up[:, D_FF:].astype(BF16),
                 f_conv_w[i].astype(F32), row(f_conv_b[i]), f_w_down[i].astype(BF16),
                 row(f_norm_post[i]), seq_rows)
    return h.reshape(bsz, seq_rows, D_MODEL)[:, CHUNK:]
```

```python
import functools
import math

import jax
import jax.numpy as jnp
import numpy as np
from jax import lax
from jax.experimental import pallas as pl
from jax.experimental.pallas import tpu as pltpu

F32 = jnp.float32
BF16 = jnp.bfloat16

D_MODEL = 1024
N_META = 16
CHUNK = 128
PAD = CHUNK - N_META

D_INNER = 2048
SSM_HEAD_DIM = 64
SSM_HEADS = 32
SSM_GROUPS = 4
HEADS_PER_GROUP = SSM_HEADS // SSM_GROUPS
D_STATE = 128
SSM_CONV = 4
D_BC = SSM_GROUPS * D_STATE
D_XBC = D_INNER + 2 * D_BC

ATTN_HEAD_DIM = 64
N_Q_HEADS = 16
N_KV_HEADS = 4
Q_PER_KV = N_Q_HEADS // N_KV_HEADS
D_ATTN = N_Q_HEADS * ATTN_HEAD_DIM
D_KV = N_KV_HEADS * ATTN_HEAD_DIM

D_FF = 2816
FFN_CONV = 3

RMS_EPS = 1e-6
NEG_INF = -1e30

LANES = 128
REP = SSM_HEADS
assert 3 * REP <= LANES
HALO = 8
VMEM_LIMIT = 56 * 1024 * 1024


def _rms(x):
    return x * lax.rsqrt(jnp.mean(x * x, axis=-1, keepdims=True) + RMS_EPS)


def _silu(x):
    return x / (1.0 + jnp.exp(-x))


def _dot(a, b):
    return jnp.dot(a, b, preferred_element_type=F32)


def _dot_nt(a, b):
    return lax.dot_general(a, b, (((1,), (1,)), ((), ())), preferred_element_type=F32)


def _resident(shape):
    return pl.BlockSpec(shape, lambda *_: (0,) * len(shape), pipeline_mode=pl.Buffered(1))


def _params(n_axes):
    return pltpu.CompilerParams(dimension_semantics=("arbitrary",) * n_axes,
                                vmem_limit_bytes=VMEM_LIMIT)


def _causal_conv(u, first, ext_ref, carry_ref, cw_ref, cb_ref, col, taps):
    tm, tn = u.shape
    ext_ref[0:HALO, :] = jnp.where(first, 0.0, carry_ref[:, col:col + tn])
    ext_ref[HALO:HALO + tm, :] = u
    carry_ref[:, col:col + tn] = u[tm - HALO:tm, :]
    out = cb_ref[:, col:col + tn] + cw_ref[taps - 1:taps, col:col + tn] * u
    for k in range(taps - 1):
        s = taps - 1 - k
        out = out + cw_ref[k:k + 1, col:col + tn] * ext_ref[HALO - s:HALO - s + tm, :]
    return out


def _in_proj_kernel(x_ref, g_ref, wz_ref, wx_ref, wdt_ref, cw_ref, cb_ref, dtb_ref,
                    zs_ref, xbc_ref, dt_ref, ext_ref, carry_ref, *, tiles_per_seq, tm, tn):
    first = (pl.program_id(0) % tiles_per_seq) == 0
    row = lax.broadcasted_iota(jnp.int32, (tm, 1), 0)
    valid = jnp.logical_or(jnp.logical_not(first), row >= PAD)
    xn = jnp.where(valid, _rms(x_ref[...]) * g_ref[...], 0.0).astype(BF16)
    for n0 in range(0, D_INNER, tn):
        zs_ref[:, n0:n0 + tn] = _silu(_dot(xn, wz_ref[:, n0:n0 + tn])).astype(BF16)
    for n0 in range(0, D_XBC, tn):
        u = _dot(xn, wx_ref[:, n0:n0 + tn])
        y = _silu(_causal_conv(u, first, ext_ref, carry_ref, cw_ref, cb_ref, n0, SSM_CONV))
        if n0 >= D_INNER:
            y = jnp.where(valid, y, 0.0)
        xbc_ref[:, n0:n0 + tn] = y.astype(BF16)
    dt_raw = _dot(xn, wdt_ref[...]) + dtb_ref[...]
    dt = jnp.maximum(dt_raw, 0.0) + jnp.log1p(jnp.exp(-jnp.abs(dt_raw)))
    dt_ref[...] = jnp.where(valid, dt, 0.0)


def _in_proj(h, gain, wz, wx, wdt, conv_w, conv_b, dt_bias, seq_rows, tm=544, tn=512):
    rows = h.shape[0]
    assert seq_rows % tm == 0 and D_INNER % tn == 0 and D_XBC % tn == 0
    row_spec = lambda n: pl.BlockSpec((tm, n), lambda i: (i, 0))
    body = functools.partial(_in_proj_kernel, tiles_per_seq=seq_rows // tm, tm=tm, tn=tn)
    return pl.pallas_call(
        body,
        grid=(rows // tm,),
        in_specs=[row_spec(D_MODEL), _resident((1, D_MODEL)), _resident(wz.shape),
                  _resident(wx.shape), _resident(wdt.shape), _resident(conv_w.shape),
                  _resident(conv_b.shape), _resident(dt_bias.shape)],
        out_specs=[row_spec(D_INNER), row_spec(D_XBC), row_spec(LANES)],
        out_shape=[jax.ShapeDtypeStruct((rows, D_INNER), BF16),
                   jax.ShapeDtypeStruct((rows, D_XBC), BF16),
                   jax.ShapeDtypeStruct((rows, LANES), F32)],
        scratch_shapes=[pltpu.VMEM((HALO + tm, tn), F32), pltpu.VMEM((HALO, D_XBC), F32)],
        compiler_params=_params(1),
        name="mamba_in_proj",
    )(h, gain, wz, wx, wdt, conv_w, conv_b, dt_bias)


def _split3(x):
    hi = x.astype(BF16).astype(F32)
    r = x - hi
    mid = r.astype(BF16).astype(F32)
    return hi, mid, r - mid


def _pack3(x, lane, fill):
    hi, mid, lo = _split3(x)
    packed = jnp.where(lane < REP, hi, jnp.where(lane < 2 * REP, mid, jnp.where(lane < 3 * REP, lo, fill)))
    return packed.astype(BF16)


def _ssd_kernel(zs_ref, xbc_ref, dt_ref, alog_ref, dskip_ref, gn_ref, tril3_ref, e3_ref, selc_ref,
                o_ref, state_ref, yg_ref):
    c = pl.program_id(1)

    @pl.when(c == 0)
    def _():
        state_ref[...] = jnp.zeros_like(state_ref)

    lane = lax.broadcasted_iota(jnp.int32, (CHUNK, LANES), 1)
    li = lax.broadcasted_iota(jnp.int32, (CHUNK, CHUNK), 0)
    cbias = jnp.where(li >= lane, 0.0, NEG_INF)
    cbias = jnp.concatenate([cbias] * HEADS_PER_GROUP, axis=1)

    dt = dt_ref[...]
    a = dt * (-jnp.exp(alog_ref[...]))
    a3 = jnp.concatenate([t.astype(BF16) for t in _split3(a)], axis=0)
    acs = _dot(tril3_ref[...], a3)
    b = jnp.maximum(jnp.log(dt), NEG_INF) - acs
    expa = jnp.exp(acs)
    wst = jnp.exp(acs[CHUNK - 1:CHUNK, :] + b)
    spread = _dot(jnp.concatenate([_pack3(expa, lane, 0.0), _pack3(wst, lane, 0.0)], axis=0), e3_ref[...])
    e_exp = spread[:CHUNK]
    w_exp = spread[CHUNK:]
    u3 = _pack3(acs, lane, 1.0)
    bt_terms = _split3(b.T)
    sub = lax.broadcasted_iota(jnp.int32, (HEADS_PER_GROUP, HEADS_PER_GROUP * CHUNK), 0)
    blk = lax.broadcasted_iota(jnp.int32, (HEADS_PER_GROUP, HEADS_PER_GROUP * CHUNK), 1) // CHUNK
    on_diag = sub == blk

    xs_f = xbc_ref[:, :D_INNER].astype(F32)
    xw_b = (xs_f * w_exp).astype(BF16)
    lane_lo = lane < SSM_HEAD_DIM
    ssq = jnp.zeros((CHUNK, 1), F32)
    gw = HEADS_PER_GROUP * SSM_HEAD_DIM
    for g in range(SSM_GROUPS):
        gc = slice(g * gw, (g + 1) * gw)
        bg_b = xbc_ref[:, D_INNER + g * D_STATE:D_INNER + (g + 1) * D_STATE]
        cg_b = xbc_ref[:, D_INNER + D_BC + g * D_STATE:D_INNER + D_BC + (g + 1) * D_STATE]
        cb_mat = _dot_nt(cg_b, bg_b)
        st = state_ref[:, gc]
        y_off = _dot(cg_b, st.astype(BF16))
        st_new = lax.dot_general(bg_b, xw_b[:, gc], (((0,), (0,)), ((), ())), preferred_element_type=F32)
        state_ref[:, gc] = st * e_exp[CHUNK - 1:CHUNK, gc] + st_new
        dyn = [jnp.where(on_diag, jnp.concatenate([t[g * HEADS_PER_GROUP:(g + 1) * HEADS_PER_GROUP, :]]
                                                  * HEADS_PER_GROUP, axis=1), 0.0) for t in bt_terms]
        dyn.append(jnp.zeros_like(dyn[0]))
        v3 = jnp.concatenate([selc_ref[g], jnp.concatenate(dyn, axis=0).astype(BF16)], axis=0)
        seg = _dot(u3, v3)
        m = (jnp.exp(seg + cbias) * jnp.concatenate([cb_mat] * HEADS_PER_GROUP, axis=1)).astype(BF16)
        y_parts = []
        for pr in range(HEADS_PER_GROUP // 2):
            c0 = g * gw + pr * LANES
            xp = xbc_ref[:, c0:c0 + LANES]
            zero = jnp.zeros_like(xp)
            rhs = jnp.concatenate([jnp.where(lane_lo, xp, zero), jnp.where(lane_lo, zero, xp)], axis=0)
            y_parts.append(_dot(m[:, pr * 2 * CHUNK:(pr + 1) * 2 * CHUNK], rhs))
        y = jnp.concatenate(y_parts, axis=1) + y_off * e_exp[:, gc] + dskip_ref[:, gc] * xs_f[:, gc]
        y = y * zs_ref[:, gc].astype(F32)
        ssq = ssq + jnp.sum(y * y, axis=-1, keepdims=True)
        yg_ref[:, gc] = y
    inv = lax.rsqrt(ssq * (1.0 / D_INNER) + RMS_EPS)
    o_ref[...] = (yg_ref[...] * inv * gn_ref[...]).astype(BF16)


def _ssd_constants():
    tril3 = np.tile(np.tril(np.ones((CHUNK, CHUNK), np.float32)), (1, 3))
    e3 = np.zeros((LANES, D_INNER), np.float32)
    selc = np.zeros((SSM_GROUPS, 3 * REP, HEADS_PER_GROUP * CHUNK), np.float32)
    for t in range(3):
        for h in range(SSM_HEADS):
            e3[REP * t + h, h * SSM_HEAD_DIM:(h + 1) * SSM_HEAD_DIM] = 1.0
            g, k = divmod(h, HEADS_PER_GROUP)
            selc[g, REP * t + h, k * CHUNK:(k + 1) * CHUNK] = 1.0
    return jnp.asarray(tril3, BF16), jnp.asarray(e3, BF16), jnp.asarray(selc, BF16)


def _ssd(zs, xbc, dt, a_log, d_skip, gate_norm, bsz, nc):
    rows = zs.shape[0]
    tril3, e3, selc = _ssd_constants()
    row_spec = lambda n: pl.BlockSpec((CHUNK, n), lambda b, c: (b * nc + c, 0))
    return pl.pallas_call(
        _ssd_kernel,
        grid=(bsz, nc),
        in_specs=[row_spec(D_INNER), row_spec(D_XBC), row_spec(LANES),
                  _resident(a_log.shape), _resident(d_skip.shape), _resident(gate_norm.shape),
                  _resident(tril3.shape), _resident(e3.shape), _resident(selc.shape)],
        out_specs=row_spec(D_INNER),
        out_shape=jax.ShapeDtypeStruct((rows, D_INNER), BF16),
        scratch_shapes=[pltpu.VMEM((D_STATE, D_INNER), F32),
                        pltpu.VMEM((CHUNK, D_INNER), F32)],
        compiler_params=_params(2),
        name="mamba_ssd",
    )(zs, xbc, dt, a_log, d_skip, gate_norm, tril3, e3, selc)


def _proj_res_kernel(x_ref, w_ref, g_ref, r_ref, o_ref):
    y = _dot(x_ref[...], w_ref[...])
    o_ref[...] = r_ref[...] + _rms(y) * g_ref[...]


def _proj_res(x, w, gain, res, tm=512):
    rows, k = x.shape
    return pl.pallas_call(
        _proj_res_kernel,
        grid=(rows // tm,),
        in_specs=[pl.BlockSpec((tm, k), lambda i: (i, 0)), _resident(w.shape), _resident(gain.shape),
                  pl.BlockSpec((tm, D_MODEL), lambda i: (i, 0))],
        out_specs=pl.BlockSpec((tm, D_MODEL), lambda i: (i, 0)),
        out_shape=jax.ShapeDtypeStruct((rows, D_MODEL), F32),
        compiler_params=_params(1),
        name="proj_norm_residual",
    )(x, w, gain, res)


def _ffn_kernel(x_ref, gpre_ref, wg_ref, wv_ref, cw_ref, cb_ref, wd_ref, gpost_ref, o_ref,
                ext_ref, carry_ref, act_ref, *, tiles_per_seq, tm, tn):
    first = (pl.program_id(0) % tiles_per_seq) == 0
    x = x_ref[...]
    row = lax.broadcasted_iota(jnp.int32, (tm, 1), 0)
    valid = jnp.logical_or(jnp.logical_not(first), row >= PAD)
    xn = jnp.where(valid, _rms(x) * gpre_ref[...], 0.0).astype(BF16)

    def up_conv(w_ref, col):
        u = _dot(xn, w_ref[:, col % D_FF:col % D_FF + tn])
        return _causal_conv(u, first, ext_ref, carry_ref, cw_ref, cb_ref, col, FFN_CONV)

    for j in range(D_FF // tn):
        gate = up_conv(wg_ref, j * tn)
        val = up_conv(wv_ref, D_FF + j * tn)
        act_ref[:, j * tn:(j + 1) * tn] = (_silu(gate) * val).astype(BF16)
    y = _dot(act_ref[...], wd_ref[...])
    o_ref[...] = x + _rms(y) * gpost_ref[...]


def _ffn(h, gpre, wg, wv, conv_w, conv_b, wd, gpost, seq_rows, tm=544, tn=256):
    rows = h.shape[0]
    assert seq_rows % tm == 0 and D_FF % tn == 0
    body = functools.partial(_ffn_kernel, tiles_per_seq=seq_rows // tm, tm=tm, tn=tn)
    return pl.pallas_call(
        body,
        grid=(rows // tm,),
        in_specs=[pl.BlockSpec((tm, D_MODEL), lambda i: (i, 0)), _resident(gpre.shape),
                  _resident(wg.shape), _resident(wv.shape), _resident(conv_w.shape),
                  _resident(conv_b.shape), _resident(wd.shape), _resident(gpost.shape)],
        out_specs=pl.BlockSpec((tm, D_MODEL), lambda i: (i, 0)),
        out_shape=jax.ShapeDtypeStruct((rows, D_MODEL), F32),
        scratch_shapes=[pltpu.VMEM((HALO + tm, tn), F32),
                        pltpu.VMEM((HALO, 2 * D_FF), F32),
                        pltpu.VMEM((tm, D_FF), BF16)],
        compiler_params=_params(1),
        name="conv_ffn",
    )(h, gpre, wg, wv, conv_w, conv_b, wd, gpost)


def _qkv_proj_kernel(x_ref, gq_ref, gkv_ref, wqt_ref, wk_ref, wvt_ref, qt_ref, k_ref, vt_ref):
    xr = _rms(x_ref[...])
    xq = (xr * gq_ref[...]).astype(BF16)
    xkv = (xr * gkv_ref[...]).astype(BF16)
    scale = 1.0 / math.sqrt(ATTN_HEAD_DIM)
    qt_ref[...] = (_dot_nt(wqt_ref[...], xq) * scale).astype(BF16)
    k_ref[...] = _dot(xkv, wk_ref[...]).astype(BF16)
    vt_ref[...] = _dot_nt(wvt_ref[...], xkv).astype(BF16)


def _qkv_proj(h, gq, gkv, wqt, wk, wvt, tm=512):
    rows = h.shape[0]
    return pl.pallas_call(
        _qkv_proj_kernel,
        grid=(rows // tm,),
        in_specs=[pl.BlockSpec((tm, D_MODEL), lambda i: (i, 0)), _resident(gq.shape), _resident(gkv.shape),
                  _resident(wqt.shape), _resident(wk.shape), _resident(wvt.shape)],
        out_specs=[pl.BlockSpec((D_ATTN, tm), lambda i: (0, i)), pl.BlockSpec((tm, D_KV), lambda i: (i, 0)),
                   pl.BlockSpec((D_KV, tm), lambda i: (0, i))],
        out_shape=[jax.ShapeDtypeStruct((D_ATTN, rows), BF16), jax.ShapeDtypeStruct((rows, D_KV), BF16),
                   jax.ShapeDtypeStruct((D_KV, rows), BF16)],
        compiler_params=_params(1),
        name="qkv_proj",
    )(h, gq, gkv, wqt, wk, wvt)


def _attn_kernel(sink_ref, qt_ref, kc_ref, kp_ref, km_ref, vc_ref, vp_ref, vm_ref, ot_ref):
    c = pl.program_id(1)
    nkeys = N_META + 2 * CHUNK
    ki = lax.broadcasted_iota(jnp.int32, (nkeys, CHUNK), 0)
    qi = lax.broadcasted_iota(jnp.int32, (nkeys, CHUNK), 1)
    kprev = ki - N_META
    kcur = ki - (N_META + CHUNK)
    vis_meta = jnp.logical_and(ki < N_META, c >= 1)
    vis_prev = jnp.logical_and(jnp.logical_and(kprev >= 0, kprev < CHUNK), jnp.logical_and(kprev > qi, c >= 2))
    vis_cur = jnp.logical_and(jnp.logical_and(kcur >= 0, kcur <= qi), jnp.logical_or(c >= 1, kcur >= PAD))
    visible = jnp.logical_or(jnp.logical_or(vis_meta, vis_prev), vis_cur)
    bias = jnp.where(visible, 0.0, NEG_INF)
    bias = jnp.concatenate([bias] * Q_PER_KV, axis=1)
    k_all = jnp.concatenate([km_ref[PAD:CHUNK, :], kp_ref[...], kc_ref[...]], axis=0)
    pad_rows = jnp.zeros((PAD, Q_PER_KV * CHUNK), BF16)
    for kh in range(N_KV_HEADS):
        heads = range(kh * Q_PER_KV, (kh + 1) * Q_PER_KV)
        qs = jnp.concatenate([qt_ref[hq * ATTN_HEAD_DIM:(hq + 1) * ATTN_HEAD_DIM, :] for hq in heads], axis=1)
        zq = jnp.zeros_like(qs)
        qpad = jnp.concatenate([qs, zq] if kh % 2 == 0 else [zq, qs], axis=0)
        kk = k_all[:, (kh // 2) * LANES:(kh // 2 + 1) * LANES]
        s = _dot(kk, qpad) + bias
        sink = jnp.concatenate([jnp.full((1, CHUNK), sink_ref[hq], F32) for hq in heads], axis=1)
        m = jnp.maximum(jnp.max(s, axis=0, keepdims=True), sink)
        p = jnp.exp(s - m)
        den = jnp.sum(p, axis=0, keepdims=True) + jnp.exp(sink - m)
        rows = slice(kh * ATTN_HEAD_DIM, (kh + 1) * ATTN_HEAD_DIM)
        vt = jnp.concatenate([vm_ref[rows, :], vp_ref[rows, :], vc_ref[rows, :]], axis=1)
        p_all = jnp.concatenate([pad_rows, p.astype(BF16)], axis=0)
        o = _dot(vt, p_all) / den
        for g, hq in enumerate(heads):
            ot_ref[hq * ATTN_HEAD_DIM:(hq + 1) * ATTN_HEAD_DIM, :] = o[:, g * CHUNK:(g + 1) * CHUNK].astype(BF16)


def _attn(qt, k, vt, sinks, bsz, nc):
    rows = qt.shape[1]
    cur = lambda b, c: b * nc + c
    prev = lambda b, c: b * nc + jnp.maximum(c - 1, 0)
    meta = lambda b, c: b * nc
    k_spec = lambda f: pl.BlockSpec((CHUNK, D_KV), lambda b, c: (f(b, c), 0))
    vt_spec = lambda f: pl.BlockSpec((D_KV, CHUNK), lambda b, c: (0, f(b, c)))
    return pl.pallas_call(
        _attn_kernel,
        grid=(bsz, nc),
        in_specs=[pl.BlockSpec(memory_space=pltpu.SMEM),
                  pl.BlockSpec((D_ATTN, CHUNK), lambda b, c: (0, cur(b, c))),
                  k_spec(cur), k_spec(prev), k_spec(meta), vt_spec(cur), vt_spec(prev), vt_spec(meta)],
        out_specs=pl.BlockSpec((D_ATTN, CHUNK), lambda b, c: (0, cur(b, c))),
        out_shape=jax.ShapeDtypeStruct((D_ATTN, rows), BF16),
        compiler_params=_params(2),
        name="swa_sink_attention",
    )(sinks, qt, k, k, k, vt, vt, vt)


def _tproj_res_kernel(xt_ref, w_ref, g_ref, r_ref, o_ref):
    y = lax.dot_general(xt_ref[...], w_ref[...], (((0,), (0,)), ((), ())), preferred_element_type=F32)
    o_ref[...] = r_ref[...] + _rms(y) * g_ref[...]


def _tproj_res(xt, w, gain, res, tm=512):
    k, rows = xt.shape
    return pl.pallas_call(
        _tproj_res_kernel,
        grid=(rows // tm,),
        in_specs=[pl.BlockSpec((k, tm), lambda i: (0, i)), _resident(w.shape), _resident(gain.shape),
                  pl.BlockSpec((tm, D_MODEL), lambda i: (i, 0))],
        out_specs=pl.BlockSpec((tm, D_MODEL), lambda i: (i, 0)),
        out_shape=jax.ShapeDtypeStruct((rows, D_MODEL), F32),
        compiler_params=_params(1),
        name="tproj_norm_residual",
    )(xt, w, gain, res)


def kernel(x, meta_tokens, a_norm_pre, a_w_in, a_conv_w, a_conv_b, a_dt_bias, a_a_log, a_d_skip, a_gate_norm, a_w_out, a_norm_post, kv_norm, w_kv, b_norm_pre, b_w_q, b_sinks, b_w_o, b_norm_post, f_norm_pre, f_w_up, f_conv_w, f_conv_b, f_w_down, f_norm_post):
    bsz, seq, _ = x.shape
    assert seq % CHUNK == 0
    nc = seq // CHUNK + 1
    seq_rows = nc * CHUNK
    depth = f_norm_pre.shape[0]
    n_a = a_norm_pre.shape[0]

    def row(v):
        return v.astype(F32).reshape(1, -1)

    def rep3(v):
        return jnp.pad(jnp.concatenate([v] * 3, axis=1), ((0, 0), (0, LANES - 3 * REP)))

    head = jnp.concatenate([jnp.zeros((PAD, D_MODEL), x.dtype), meta_tokens.astype(x.dtype)], axis=0)
    h = jnp.concatenate([jnp.broadcast_to(head[None], (bsz, CHUNK, D_MODEL)), x], axis=1)
    h = h.reshape(bsz * seq_rows, D_MODEL)

    kv = None
    for i in range(depth):
        if i < n_a:
            j = i
            w_in = a_w_in[j]
            zs, xbc, dt = _in_proj(h, row(a_norm_pre[j]), w_in[:, :D_INNER].astype(BF16),
                                   w_in[:, D_INNER:D_INNER + D_XBC].astype(BF16),
                                   rep3(w_in[:, D_INNER + D_XBC:]).astype(BF16),
                                   a_conv_w[j].astype(F32), row(a_conv_b[j]), rep3(row(a_dt_bias[j])), seq_rows)
            y = _ssd(zs, xbc, dt, rep3(row(a_a_log[j])), row(jnp.repeat(a_d_skip[j], SSM_HEAD_DIM)),
                     row(a_gate_norm[j]), bsz, nc)
            h = _proj_res(y, a_w_out[j].astype(BF16), row(a_norm_post[j]), h)
        else:
            j = i - n_a
            qt, k_new, vt_new = _qkv_proj(h, row(b_norm_pre[j]), row(kv_norm), b_w_q[j].T.astype(BF16),
                                          w_kv[:, :D_KV].astype(BF16), w_kv[:, D_KV:].T.astype(BF16))
            if j == 0:
                kv = (k_new, vt_new)
            att_t = _attn(qt, kv[0], kv[1], b_sinks[j].astype(F32), bsz, nc)
            h = _tproj_res(att_t, b_w_o[j].astype(BF16), row(b_norm_post[j]), h)
        w_up = f_w_up[i]
        h = _ffn(h, row(f_norm_pre[i]), w_up[:, :D_FF].astype(BF16), w_up[:, D_FF:].astype(BF16),
                 f_conv_w[i].astype(F32), row(f_conv_b[i]), f_w_down[i].astype(BF16),
                 row(f_norm_post[i]), seq_rows)
    return h.reshape(bsz, seq_rows, D_MODEL)[:, CHUNK:]
```

```python
import functools
import math

import jax
import jax.numpy as jnp
import numpy as np
from jax import lax
from jax.experimental import pallas as pl
from jax.experimental.pallas import tpu as pltpu

F32 = jnp.float32
BF16 = jnp.bfloat16

D_MODEL = 1024
N_META = 16
CHUNK = 128
PAD = CHUNK - N_META

D_INNER = 2048
SSM_HEAD_DIM = 64
SSM_HEADS = 32
SSM_GROUPS = 4
HEADS_PER_GROUP = SSM_HEADS // SSM_GROUPS
D_STATE = 128
SSM_CONV = 4
D_BC = SSM_GROUPS * D_STATE
D_XBC = D_INNER + 2 * D_BC

ATTN_HEAD_DIM = 64
N_Q_HEADS = 16
N_KV_HEADS = 4
Q_PER_KV = N_Q_HEADS // N_KV_HEADS
D_ATTN = N_Q_HEADS * ATTN_HEAD_DIM
D_KV = N_KV_HEADS * ATTN_HEAD_DIM

D_FF = 2816
FFN_CONV = 3

RMS_EPS = 1e-6
NEG_INF = -1e30

LANES = 128
HALO = 8
REP = SSM_HEADS
assert 3 * REP <= LANES
VMEM_LIMIT = 56 * 1024 * 1024
BODY_TM = 4 * CHUNK
IN_PROJ_TN = 512
FFN_TN = 256


def _rms(x):
    return x * lax.rsqrt(jnp.mean(x * x, axis=-1, keepdims=True) + RMS_EPS)


def _silu(x):
    return x / (1.0 + jnp.exp(-x))


def _dot(a, b):
    return jnp.dot(a, b, preferred_element_type=F32)


def _dot_nt(a, b):
    return lax.dot_general(a, b, (((1,), (1,)), ((), ())), preferred_element_type=F32)


def _dot_tn(a, b):
    return lax.dot_general(a, b, (((0,), (0,)), ((), ())), preferred_element_type=F32)


def _resident(shape):
    return pl.BlockSpec(shape, lambda *_: (0,) * len(shape), pipeline_mode=pl.Buffered(1))


def _params(n_axes):
    return pltpu.CompilerParams(dimension_semantics=("arbitrary",) * n_axes,
                                vmem_limit_bytes=VMEM_LIMIT)


def _row_tile(head):
    return CHUNK if head else BODY_TM


def _causal_conv(u, first, init_ref, ext_ref, carry_ref, tail_ref, cw_ref, cb_ref, col, taps):
    tm, tn = u.shape
    cols = slice(col, col + tn)
    ext_ref[0:HALO, :] = jnp.where(first, init_ref[:, cols], carry_ref[:, cols])
    ext_ref[HALO:HALO + tm, :] = u
    carry_ref[:, cols] = u[tm - HALO:tm, :]
    if tail_ref is not None:
        tail_ref[:, cols] = u[tm - HALO:tm, :]
    out = cb_ref[:, cols] + cw_ref[taps - 1:taps, cols] * u
    for k in range(taps - 1):
        s = taps - 1 - k
        out = out + cw_ref[k:k + 1, cols] * ext_ref[HALO - s:HALO - s + tm, :]
    return out


def _normed_input(x, gain_ref, head):
    xn = _rms(x) * gain_ref[...]
    if head:
        row = lax.broadcasted_iota(jnp.int32, (x.shape[0], 1), 0)
        xn = jnp.where(row >= PAD, xn, 0.0)
    return xn.astype(BF16)


def _in_proj_kernel(*refs, tiles_per_seq, tm, tn, head):
    x_ref, g_ref, wz_ref, wx_ref, wdt_ref, cw_ref, cb_ref, dtb_ref, init_ref, zs_ref, xbc_ref, dt_ref = refs[:12]
    tail_ref = refs[12] if head else None
    ext_ref, carry_ref = refs[-2:]
    first = (pl.program_id(0) % tiles_per_seq) == 0
    xn = _normed_input(x_ref[...], g_ref, head)
    valid = lax.broadcasted_iota(jnp.int32, (tm, 1), 0) >= PAD
    for n0 in range(0, D_INNER, tn):
        zs_ref[:, n0:n0 + tn] = _silu(_dot(xn, wz_ref[:, n0:n0 + tn])).astype(BF16)
    for n0 in range(0, D_XBC, tn):
        u = _dot(xn, wx_ref[:, n0:n0 + tn])
        y = _silu(_causal_conv(u, first, init_ref, ext_ref, carry_ref, tail_ref, cw_ref, cb_ref, n0, SSM_CONV))
        if head and n0 >= D_INNER:
            y = jnp.where(valid, y, 0.0)
        xbc_ref[:, n0:n0 + tn] = y.astype(BF16)
    dt_raw = _dot(xn, wdt_ref[...]) + dtb_ref[...]
    dt = jnp.maximum(dt_raw, 0.0) + jnp.log1p(jnp.exp(-jnp.abs(dt_raw)))
    dt_ref[...] = jnp.where(valid, dt, 0.0) if head else dt


def _in_proj(h, gain, wz, wx, wdt, conv_w, conv_b, dt_bias, conv_init, seq_rows, head):
    rows = h.shape[0]
    tm, tn = _row_tile(head), IN_PROJ_TN
    assert seq_rows % tm == 0 and D_INNER % tn == 0 and D_XBC % tn == 0
    row_spec = lambda n: pl.BlockSpec((tm, n), lambda i: (i, 0))
    out_specs = [row_spec(D_INNER), row_spec(D_XBC), row_spec(LANES)]
    out_shape = [jax.ShapeDtypeStruct((rows, D_INNER), BF16), jax.ShapeDtypeStruct((rows, D_XBC), BF16),
                 jax.ShapeDtypeStruct((rows, LANES), F32)]
    if head:
        out_specs.append(_resident((HALO, D_XBC)))
        out_shape.append(jax.ShapeDtypeStruct((HALO, D_XBC), F32))
    return pl.pallas_call(
        functools.partial(_in_proj_kernel, tiles_per_seq=seq_rows // tm, tm=tm, tn=tn, head=head),
        grid=(rows // tm,),
        in_specs=[row_spec(D_MODEL), _resident((1, D_MODEL)), _resident(wz.shape),
                  _resident(wx.shape), _resident(wdt.shape), _resident(conv_w.shape),
                  _resident(conv_b.shape), _resident(dt_bias.shape), _resident(conv_init.shape)],
        out_specs=out_specs,
        out_shape=out_shape,
        scratch_shapes=[pltpu.VMEM((HALO + tm, tn), F32), pltpu.VMEM((HALO, D_XBC), F32)],
        compiler_params=_params(1),
        name="mamba_in_proj",
    )(h, gain, wz, wx, wdt, conv_w, conv_b, dt_bias, conv_init)


def _split3(x):
    hi = x.astype(BF16).astype(F32)
    r = x - hi
    mid = r.astype(BF16).astype(F32)
    return hi, mid, r - mid


def _pack3(x, lane, fill):
    hi, mid, lo = _split3(x)
    packed = jnp.where(lane < REP, hi, jnp.where(lane < 2 * REP, mid, jnp.where(lane < 3 * REP, lo, fill)))
    return packed.astype(BF16)


def _ssd_kernel(*refs, head):
    (zs_ref, xbc_ref, dt_ref, alog_ref, dskip_ref, gn_ref, tril3_ref, e3_ref, selc_ref, init_ref, o_ref) = refs[:11]
    final_ref = refs[11] if head else None
    state_ref, yg_ref = refs[-2:]
    c = pl.program_id(1)

    @pl.when(c == 0)
    def _():
        state_ref[...] = init_ref[...]

    lane = lax.broadcasted_iota(jnp.int32, (CHUNK, LANES), 1)
    li = lax.broadcasted_iota(jnp.int32, (CHUNK, CHUNK), 0)
    cbias = jnp.where(li >= lane, 0.0, NEG_INF)
    cbias = jnp.concatenate([cbias] * HEADS_PER_GROUP, axis=1)

    dt = dt_ref[...]
    a = dt * (-jnp.exp(alog_ref[...]))
    a3 = jnp.concatenate([t.astype(BF16) for t in _split3(a)], axis=0)
    acs = _dot(tril3_ref[...], a3)
    b = jnp.maximum(jnp.log(dt), NEG_INF) - acs
    expa = jnp.exp(acs)
    wst = jnp.exp(acs[CHUNK - 1:CHUNK, :] + b)
    spread = _dot(jnp.concatenate([_pack3(expa, lane, 0.0), _pack3(wst, lane, 0.0)], axis=0), e3_ref[...])
    e_exp = spread[:CHUNK]
    w_exp = spread[CHUNK:]
    u3 = _pack3(acs, lane, 1.0)
    bt_terms = _split3(b.T)
    sub = lax.broadcasted_iota(jnp.int32, (HEADS_PER_GROUP, HEADS_PER_GROUP * CHUNK), 0)
    blk = lax.broadcasted_iota(jnp.int32, (HEADS_PER_GROUP, HEADS_PER_GROUP * CHUNK), 1) // CHUNK
    on_diag = sub == blk

    xs_f = xbc_ref[:, :D_INNER].astype(F32)
    xw_b = (xs_f * w_exp).astype(BF16)
    lane_lo = lane < SSM_HEAD_DIM
    ssq = jnp.zeros((CHUNK, 1), F32)
    gw = HEADS_PER_GROUP * SSM_HEAD_DIM
    for g in range(SSM_GROUPS):
        gc = slice(g * gw, (g + 1) * gw)
        bg_b = xbc_ref[:, D_INNER + g * D_STATE:D_INNER + (g + 1) * D_STATE]
        cg_b = xbc_ref[:, D_INNER + D_BC + g * D_STATE:D_INNER + D_BC + (g + 1) * D_STATE]
        cb_mat = _dot_nt(cg_b, bg_b)
        st = state_ref[:, gc]
        y_off = _dot(cg_b, st.astype(BF16))
        state_ref[:, gc] = st * e_exp[CHUNK - 1:CHUNK, gc] + _dot_tn(bg_b, xw_b[:, gc])
        dyn = [jnp.where(on_diag, jnp.concatenate([t[g * HEADS_PER_GROUP:(g + 1) * HEADS_PER_GROUP, :]]
                                                  * HEADS_PER_GROUP, axis=1), 0.0) for t in bt_terms]
        dyn.append(jnp.zeros_like(dyn[0]))
        v3 = jnp.concatenate([selc_ref[g], jnp.concatenate(dyn, axis=0).astype(BF16)], axis=0)
        seg = _dot(u3, v3)
        m = (jnp.exp(seg + cbias) * jnp.concatenate([cb_mat] * HEADS_PER_GROUP, axis=1)).astype(BF16)
        y_parts = []
        for pr in range(HEADS_PER_GROUP // 2):
            c0 = g * gw + pr * LANES
            xp = xbc_ref[:, c0:c0 + LANES]
            zero = jnp.zeros_like(xp)
            rhs = jnp.concatenate([jnp.where(lane_lo, xp, zero), jnp.where(lane_lo, zero, xp)], axis=0)
            y_parts.append(_dot(m[:, pr * 2 * CHUNK:(pr + 1) * 2 * CHUNK], rhs))
        y = jnp.concatenate(y_parts, axis=1) + y_off * e_exp[:, gc] + dskip_ref[:, gc] * xs_f[:, gc]
        y = y * zs_ref[:, gc].astype(F32)
        ssq = ssq + jnp.sum(y * y, axis=-1, keepdims=True)
        yg_ref[:, gc] = y
    inv = lax.rsqrt(ssq * (1.0 / D_INNER) + RMS_EPS)
    o_ref[...] = (yg_ref[...] * inv * gn_ref[...]).astype(BF16)
    if head:
        final_ref[...] = state_ref[...]


def _ssd_constants():
    tril3 = np.tile(np.tril(np.ones((CHUNK, CHUNK), np.float32)), (1, 3))
    e3 = np.zeros((LANES, D_INNER), np.float32)
    selc = np.zeros((SSM_GROUPS, 3 * REP, HEADS_PER_GROUP * CHUNK), np.float32)
    for t in range(3):
        for h in range(SSM_HEADS):
            e3[REP * t + h, h * SSM_HEAD_DIM:(h + 1) * SSM_HEAD_DIM] = 1.0
            g, k = divmod(h, HEADS_PER_GROUP)
            selc[g, REP * t + h, k * CHUNK:(k + 1) * CHUNK] = 1.0
    return jnp.asarray(tril3, BF16), jnp.asarray(e3, BF16), jnp.asarray(selc, BF16)


def _ssd(zs, xbc, dt, a_log, d_skip, gate_norm, state_init, bsz, nc, head):
    rows = zs.shape[0]
    tril3, e3, selc = _ssd_constants()
    row_spec = lambda n: pl.BlockSpec((CHUNK, n), lambda b, c: (b * nc + c, 0))
    out_specs = [row_spec(D_INNER)]
    out_shape = [jax.ShapeDtypeStruct((rows, D_INNER), BF16)]
    if head:
        out_specs.append(_resident((D_STATE, D_INNER)))
        out_shape.append(jax.ShapeDtypeStruct((D_STATE, D_INNER), F32))
    return pl.pallas_call(
        functools.partial(_ssd_kernel, head=head),
        grid=(bsz, nc),
        in_specs=[row_spec(D_INNER), row_spec(D_XBC), row_spec(LANES),
                  _resident(a_log.shape), _resident(d_skip.shape), _resident(gate_norm.shape),
                  _resident(tril3.shape), _resident(e3.shape), _resident(selc.shape),
                  _resident(state_init.shape)],
        out_specs=out_specs,
        out_shape=out_shape,
        scratch_shapes=[pltpu.VMEM((D_STATE, D_INNER), F32),
                        pltpu.VMEM((CHUNK, D_INNER), F32)],
        compiler_params=_params(2),
        name="mamba_ssd",
    )(zs, xbc, dt, a_log, d_skip, gate_norm, tril3, e3, selc, state_init)


def _proj_res_kernel(x_ref, w_ref, g_ref, r_ref, o_ref, *, transposed):
    y = _dot_tn(x_ref[...], w_ref[...]) if transposed else _dot(x_ref[...], w_ref[...])
    o_ref[...] = r_ref[...] + _rms(y) * g_ref[...]


def _proj_res(x, w, gain, res, head, transposed=False):
    rows = res.shape[0]
    k = w.shape[0]
    tm = _row_tile(head)
    x_spec = pl.BlockSpec((k, tm), lambda i: (0, i)) if transposed else pl.BlockSpec((tm, k), lambda i: (i, 0))
    return pl.pallas_call(
        functools.partial(_proj_res_kernel, transposed=transposed),
        grid=(rows // tm,),
        in_specs=[x_spec, _resident(w.shape), _resident(gain.shape),
                  pl.BlockSpec((tm, D_MODEL), lambda i: (i, 0))],
        out_specs=pl.BlockSpec((tm, D_MODEL), lambda i: (i, 0)),
        out_shape=jax.ShapeDtypeStruct((rows, D_MODEL), F32),
        compiler_params=_params(1),
        name="proj_norm_residual",
    )(x, w, gain, res)


def _ffn_kernel(*refs, tiles_per_seq, tm, tn, head):
    x_ref, gpre_ref, wg_ref, wv_ref, cw_ref, cb_ref, wd_ref, gpost_ref, init_ref, o_ref = refs[:10]
    tail_ref = refs[10] if head else None
    ext_ref, carry_ref, act_ref = refs[-3:]
    first = (pl.program_id(0) % tiles_per_seq) == 0
    x = x_ref[...]
    xn = _normed_input(x, gpre_ref, head)

    def up_conv(w_ref, col):
        u = _dot(xn, w_ref[:, col % D_FF:col % D_FF + tn])
        return _causal_conv(u, first, init_ref, ext_ref, carry_ref, tail_ref, cw_ref, cb_ref, col, FFN_CONV)

    for j in range(D_FF // tn):
        gate = up_conv(wg_ref, j * tn)
        val = up_conv(wv_ref, D_FF + j * tn)
        act_ref[:, j * tn:(j + 1) * tn] = (_silu(gate) * val).astype(BF16)
    y = _dot(act_ref[...], wd_ref[...])
    o_ref[...] = x + _rms(y) * gpost_ref[...]


def _ffn(h, gpre, wg, wv, conv_w, conv_b, wd, gpost, conv_init, seq_rows, head):
    rows = h.shape[0]
    tm, tn = _row_tile(head), FFN_TN
    assert seq_rows % tm == 0 and D_FF % tn == 0
    out_specs = [pl.BlockSpec((tm, D_MODEL), lambda i: (i, 0))]
    out_shape = [jax.ShapeDtypeStruct((rows, D_MODEL), F32)]
    if head:
        out_specs.append(_resident((HALO, 2 * D_FF)))
        out_shape.append(jax.ShapeDtypeStruct((HALO, 2 * D_FF), F32))
    return pl.pallas_call(
        functools.partial(_ffn_kernel, tiles_per_seq=seq_rows // tm, tm=tm, tn=tn, head=head),
        grid=(rows // tm,),
        in_specs=[pl.BlockSpec((tm, D_MODEL), lambda i: (i, 0)), _resident(gpre.shape),
                  _resident(wg.shape), _resident(wv.shape), _resident(conv_w.shape),
                  _resident(conv_b.shape), _resident(wd.shape), _resident(gpost.shape),
                  _resident(conv_init.shape)],
        out_specs=out_specs,
        out_shape=out_shape,
        scratch_shapes=[pltpu.VMEM((HALO + tm, tn), F32),
                        pltpu.VMEM((HALO, 2 * D_FF), F32),
                        pltpu.VMEM((tm, D_FF), BF16)],
        compiler_params=_params(1),
        name="conv_ffn",
    )(h, gpre, wg, wv, conv_w, conv_b, wd, gpost, conv_init)


def _qkv_proj_kernel(x_ref, gq_ref, gkv_ref, wqt_ref, wk_ref, wvt_ref, qt_ref, k_ref, vt_ref):
    xr = _rms(x_ref[...])
    xq = (xr * gq_ref[...]).astype(BF16)
    xkv = (xr * gkv_ref[...]).astype(BF16)
    scale = 1.0 / math.sqrt(ATTN_HEAD_DIM)
    qt_ref[...] = (_dot_nt(wqt_ref[...], xq) * scale).astype(BF16)
    k_ref[...] = _dot(xkv, wk_ref[...]).astype(BF16)
    vt_ref[...] = _dot_nt(wvt_ref[...], xkv).astype(BF16)


def _qkv_proj(h, gq, gkv, wqt, wk, wvt, head):
    rows = h.shape[0]
    tm = _row_tile(head)
    return pl.pallas_call(
        _qkv_proj_kernel,
        grid=(rows // tm,),
        in_specs=[pl.BlockSpec((tm, D_MODEL), lambda i: (i, 0)), _resident(gq.shape), _resident(gkv.shape),
                  _resident(wqt.shape), _resident(wk.shape), _resident(wvt.shape)],
        out_specs=[pl.BlockSpec((D_ATTN, tm), lambda i: (0, i)), pl.BlockSpec((tm, D_KV), lambda i: (i, 0)),
                   pl.BlockSpec((D_KV, tm), lambda i: (0, i))],
        out_shape=[jax.ShapeDtypeStruct((D_ATTN, rows), BF16), jax.ShapeDtypeStruct((rows, D_KV), BF16),
                   jax.ShapeDtypeStruct((D_KV, rows), BF16)],
        compiler_params=_params(1),
        name="qkv_proj",
    )(h, gq, gkv, wqt, wk, wvt)


def _attn_kernel(sink_ref, qt_ref, kc_ref, kp_ref, km_ref, vc_ref, vp_ref, vm_ref, ot_ref, *, head):
    c = pl.program_id(1)
    nkeys = N_META + 2 * CHUNK
    ki = lax.broadcasted_iota(jnp.int32, (nkeys, CHUNK), 0)
    qi = lax.broadcasted_iota(jnp.int32, (nkeys, CHUNK), 1)
    kprev = ki - N_META
    kcur = ki - (N_META + CHUNK)
    in_cur = jnp.logical_and(kcur >= 0, kcur <= qi)
    if head:
        visible = jnp.logical_and(in_cur, kcur >= PAD)
    else:
        in_prev = jnp.logical_and(jnp.logical_and(kprev >= 0, kprev < CHUNK), jnp.logical_and(kprev > qi, c >= 1))
        visible = jnp.logical_or(jnp.logical_or(ki < N_META, in_prev), in_cur)
    bias = jnp.where(visible, 0.0, NEG_INF)
    bias = jnp.concatenate([bias] * Q_PER_KV, axis=1)
    k_all = jnp.concatenate([km_ref[PAD:CHUNK, :], kp_ref[...], kc_ref[...]], axis=0)
    pad_rows = jnp.zeros((PAD, Q_PER_KV * CHUNK), BF16)
    for kh in range(N_KV_HEADS):
        heads = range(kh * Q_PER_KV, (kh + 1) * Q_PER_KV)
        qs = jnp.concatenate([qt_ref[hq * ATTN_HEAD_DIM:(hq + 1) * ATTN_HEAD_DIM, :] for hq in heads], axis=1)
        zq = jnp.zeros_like(qs)
        qpad = jnp.concatenate([qs, zq] if kh % 2 == 0 else [zq, qs], axis=0)
        kk = k_all[:, (kh // 2) * LANES:(kh // 2 + 1) * LANES]
        s = _dot(kk, qpad) + bias
        sink = jnp.concatenate([jnp.full((1, CHUNK), sink_ref[hq], F32) for hq in heads], axis=1)
        m = jnp.maximum(jnp.max(s, axis=0, keepdims=True), sink)
        p = jnp.exp(s - m)
        den = jnp.sum(p, axis=0, keepdims=True) + jnp.exp(sink - m)
        rows = slice(kh * ATTN_HEAD_DIM, (kh + 1) * ATTN_HEAD_DIM)
        vt = jnp.concatenate([vm_ref[rows, :], vp_ref[rows, :], vc_ref[rows, :]], axis=1)
        p_all = jnp.concatenate([pad_rows, p.astype(BF16)], axis=0)
        o = _dot(vt, p_all) / den
        for g, hq in enumerate(heads):
            ot_ref[hq * ATTN_HEAD_DIM:(hq + 1) * ATTN_HEAD_DIM, :] = o[:, g * CHUNK:(g + 1) * CHUNK].astype(BF16)


def _attn(qt, k, vt, k_head, vt_head, sinks, bsz, nc, head):
    rows = qt.shape[1]
    cur = lambda b, c: b * nc + c
    prev = lambda b, c: b * nc + jnp.maximum(c - 1, 0)
    k_spec = lambda f: pl.BlockSpec((CHUNK, D_KV), lambda b, c: (f(b, c), 0))
    vt_spec = lambda f: pl.BlockSpec((D_KV, CHUNK), lambda b, c: (0, f(b, c)))
    return pl.pallas_call(
        functools.partial(_attn_kernel, head=head),
        grid=(bsz, nc),
        in_specs=[pl.BlockSpec(memory_space=pltpu.SMEM),
                  pl.BlockSpec((D_ATTN, CHUNK), lambda b, c: (0, cur(b, c))),
                  k_spec(cur), k_spec(prev), _resident(k_head.shape),
                  vt_spec(cur), vt_spec(prev), _resident(vt_head.shape)],
        out_specs=pl.BlockSpec((D_ATTN, CHUNK), lambda b, c: (0, cur(b, c))),
        out_shape=jax.ShapeDtypeStruct((D_ATTN, rows), BF16),
        compiler_params=_params(2),
        name="swa_sink_attention",
    )(sinks, qt, k, k, k_head, vt, vt, vt_head)


def kernel(x, meta_tokens, a_norm_pre, a_w_in, a_conv_w, a_conv_b, a_dt_bias, a_a_log, a_d_skip, a_gate_norm, a_w_out, a_norm_post, kv_norm, w_kv, b_norm_pre, b_w_q, b_sinks, b_w_o, b_norm_post, f_norm_pre, f_w_up, f_conv_w, f_conv_b, f_w_down, f_norm_post):
    bsz, seq, _ = x.shape
    assert seq % BODY_TM == 0
    depth = f_norm_pre.shape[0]
    n_a = a_norm_pre.shape[0]

    def row(v):
        return v.astype(F32).reshape(1, -1)

    def rep3(v):
        return jnp.pad(jnp.concatenate([v] * 3, axis=1), ((0, 0), (0, LANES - 3 * REP)))

    passes = {True: (1, CHUNK, 1), False: (bsz, seq, seq // CHUNK)}
    hs = {True: jnp.concatenate([jnp.zeros((PAD, D_MODEL), F32), meta_tokens.astype(F32)], axis=0),
          False: x.astype(F32).reshape(bsz * seq, D_MODEL)}

    kv = {}
    for i in range(depth):
        if i < n_a:
            j = i
            w_in = a_w_in[j]
            wz, wx = w_in[:, :D_INNER].astype(BF16), w_in[:, D_INNER:D_INNER + D_XBC].astype(BF16)
            wdt = rep3(w_in[:, D_INNER + D_XBC:]).astype(BF16)
            w_out = a_w_out[j].astype(BF16)
            conv_init = jnp.zeros((HALO, D_XBC), F32)
            state_init = jnp.zeros((D_STATE, D_INNER), F32)
            for head in (True, False):
                nseq, seq_rows, nc = passes[head]
                outs = _in_proj(hs[head], row(a_norm_pre[j]), wz, wx, wdt, a_conv_w[j].astype(F32),
                                row(a_conv_b[j]), rep3(row(a_dt_bias[j])), conv_init, seq_rows, head)
                ys = _ssd(outs[0], outs[1], outs[2], rep3(row(a_a_log[j])),
                          row(jnp.repeat(a_d_skip[j], SSM_HEAD_DIM)), row(a_gate_norm[j]), state_init, nseq, nc, head)
                hs[head] = _proj_res(ys[0], w_out, row(a_norm_post[j]), hs[head], head)
                if head:
                    conv_init, state_init = outs[3], ys[1]
        else:
            j = i - n_a
            wqt, wo = b_w_q[j].T.astype(BF16), b_w_o[j].astype(BF16)
            wk, wvt = w_kv[:, :D_KV].astype(BF16), w_kv[:, D_KV:].T.astype(BF16)
            for head in (True, False):
                nseq, seq_rows, nc = passes[head]
                qt, k_new, vt_new = _qkv_proj(hs[head], row(b_norm_pre[j]), row(kv_norm), wqt, wk, wvt, head)
                if j == 0:
                    kv[head] = (k_new, vt_new)
                att_t = _attn(qt, kv[head][0], kv[head][1], kv[True][0], kv[True][1],
                              b_sinks[j].astype(F32), nseq, nc, head)
                hs[head] = _proj_res(att_t, wo, row(b_norm_post[j]), hs[head], head, transposed=True)
        w_up = f_w_up[i]
        wg, wv, wd = w_up[:, :D_FF].astype(BF16), w_up[:, D_FF:].astype(BF16), f_w_down[i].astype(BF16)
        conv_init = jnp.zeros((HALO, 2 * D_FF), F32)
        for head in (True, False):
            nseq, seq_rows, nc = passes[head]
            outs = _ffn(hs[head], row(f_norm_pre[i]), wg, wv, f_conv_w[i].astype(F32), row(f_conv_b[i]), wd,
                        row(f_norm_post[i]), conv_init, seq_rows, head)
            hs[head] = outs[0]
            if head:
                conv_init = outs[1]
    return hs[False].reshape(bsz, seq, D_MODEL)
```

```python
import functools
import math

import jax
import jax.numpy as jnp
import numpy as np
from jax import lax
from jax.experimental import pallas as pl
from jax.experimental.pallas import tpu as pltpu

F32 = jnp.float32
BF16 = jnp.bfloat16

D_MODEL = 1024
N_META = 16
CHUNK = 128
PAD = CHUNK - N_META

D_INNER = 2048
SSM_HEAD_DIM = 64
SSM_HEADS = 32
SSM_GROUPS = 4
HEADS_PER_GROUP = SSM_HEADS // SSM_GROUPS
D_STATE = 128
SSM_CONV = 4
D_BC = SSM_GROUPS * D_STATE
D_XBC = D_INNER + 2 * D_BC

ATTN_HEAD_DIM = 64
N_Q_HEADS = 16
N_KV_HEADS = 4
Q_PER_KV = N_Q_HEADS // N_KV_HEADS
D_ATTN = N_Q_HEADS * ATTN_HEAD_DIM
D_KV = N_KV_HEADS * ATTN_HEAD_DIM

D_FF = 2816
FFN_CONV = 3

RMS_EPS = 1e-6
NEG_INF = -1e30

LANES = 128
HALO = 8
REP = SSM_HEADS
assert 3 * REP <= LANES
VMEM_LIMIT = 56 * 1024 * 1024
BODY_TM = 4 * CHUNK
IN_PROJ_TN = 512
FFN_TN = 256


def _rms(x):
    return x * lax.rsqrt(jnp.mean(x * x, axis=-1, keepdims=True) + RMS_EPS)


def _silu(x):
    return x / (1.0 + jnp.exp(-x))


def _dot(a, b):
    return jnp.dot(a, b, preferred_element_type=F32)


def _dot_nt(a, b):
    return lax.dot_general(a, b, (((1,), (1,)), ((), ())), preferred_element_type=F32)


def _dot_tn(a, b):
    return lax.dot_general(a, b, (((0,), (0,)), ((), ())), preferred_element_type=F32)


def _resident(shape):
    return pl.BlockSpec(shape, lambda *_: (0,) * len(shape), pipeline_mode=pl.Buffered(1))


def _params(n_axes):
    return pltpu.CompilerParams(dimension_semantics=("arbitrary",) * n_axes,
                                vmem_limit_bytes=VMEM_LIMIT)


def _row_tile(head):
    return CHUNK if head else BODY_TM


def _causal_conv(u, first, init_ref, ext_ref, carry_ref, tail_ref, cw_ref, cb_ref, col, taps):
    tm, tn = u.shape
    cols = slice(col, col + tn)
    ext_ref[0:HALO, :] = jnp.where(first, init_ref[:, cols], carry_ref[:, cols])
    ext_ref[HALO:HALO + tm, :] = u
    carry_ref[:, cols] = u[tm - HALO:tm, :]
    if tail_ref is not None:
        tail_ref[:, cols] = u[tm - HALO:tm, :]
    out = cb_ref[:, cols] + cw_ref[taps - 1:taps, cols] * u
    for k in range(taps - 1):
        s = taps - 1 - k
        out = out + cw_ref[k:k + 1, cols] * ext_ref[HALO - s:HALO - s + tm, :]
    return out


def _normed_input(x, gain_ref, head):
    xn = _rms(x) * gain_ref[...]
    if head:
        row = lax.broadcasted_iota(jnp.int32, (x.shape[0], 1), 0)
        xn = jnp.where(row >= PAD, xn, 0.0)
    return xn.astype(BF16)


def _in_proj_kernel(*refs, tiles_per_seq, tm, tn, head):
    x_ref, g_ref, wz_ref, wx_ref, wdt_ref, cw_ref, cb_ref, dtb_ref, init_ref, zs_ref, xbc_ref, dt_ref = refs[:12]
    tail_ref = refs[12] if head else None
    ext_ref, carry_ref = refs[-2:]
    first = (pl.program_id(0) % tiles_per_seq) == 0
    xn = _normed_input(x_ref[...], g_ref, head)
    valid = lax.broadcasted_iota(jnp.int32, (tm, 1), 0) >= PAD
    for n0 in range(0, D_INNER, tn):
        zs_ref[:, n0:n0 + tn] = _silu(_dot(xn, wz_ref[:, n0:n0 + tn])).astype(BF16)
    for n0 in range(0, D_XBC, tn):
        u = _dot(xn, wx_ref[:, n0:n0 + tn])
        y = _silu(_causal_conv(u, first, init_ref, ext_ref, carry_ref, tail_ref, cw_ref, cb_ref, n0, SSM_CONV))
        if head and n0 >= D_INNER:
            y = jnp.where(valid, y, 0.0)
        xbc_ref[:, n0:n0 + tn] = y.astype(BF16)
    dt_raw = _dot(xn, wdt_ref[...]) + dtb_ref[...]
    dt = jnp.maximum(dt_raw, 0.0) + jnp.log1p(jnp.exp(-jnp.abs(dt_raw)))
    dt_ref[...] = jnp.where(valid, dt, 0.0) if head else dt


def _in_proj(h, gain, wz, wx, wdt, conv_w, conv_b, dt_bias, conv_init, seq_rows, head):
    rows = h.shape[0]
    tm, tn = _row_tile(head), IN_PROJ_TN
    assert seq_rows % tm == 0 and D_INNER % tn == 0 and D_XBC % tn == 0
    row_spec = lambda n: pl.BlockSpec((tm, n), lambda i: (i, 0))
    out_specs = [row_spec(D_INNER), row_spec(D_XBC), row_spec(LANES)]
    out_shape = [jax.ShapeDtypeStruct((rows, D_INNER), BF16), jax.ShapeDtypeStruct((rows, D_XBC), BF16),
                 jax.ShapeDtypeStruct((rows, LANES), F32)]
    if head:
        out_specs.append(_resident((HALO, D_XBC)))
        out_shape.append(jax.ShapeDtypeStruct((HALO, D_XBC), F32))
    return pl.pallas_call(
        functools.partial(_in_proj_kernel, tiles_per_seq=seq_rows // tm, tm=tm, tn=tn, head=head),
        grid=(rows // tm,),
        in_specs=[row_spec(D_MODEL), _resident((1, D_MODEL)), _resident(wz.shape),
                  _resident(wx.shape), _resident(wdt.shape), _resident(conv_w.shape),
                  _resident(conv_b.shape), _resident(dt_bias.shape), _resident(conv_init.shape)],
        out_specs=out_specs,
        out_shape=out_shape,
        scratch_shapes=[pltpu.VMEM((HALO + tm, tn), F32), pltpu.VMEM((HALO, D_XBC), F32)],
        compiler_params=_params(1),
        name="mamba_in_proj",
    )(h, gain, wz, wx, wdt, conv_w, conv_b, dt_bias, conv_init)


def _split3(x):
    hi = x.astype(BF16).astype(F32)
    r = x - hi
    mid = r.astype(BF16).astype(F32)
    return hi, mid, r - mid


def _pack3(x, lane, fill):
    hi, mid, lo = _split3(x)
    packed = jnp.where(lane < REP, hi, jnp.where(lane < 2 * REP, mid, jnp.where(lane < 3 * REP, lo, fill)))
    return packed.astype(BF16)


def _ssd_kernel(*refs, head):
    (zs_ref, xbc_ref, dt_ref, alog_ref, dskip_ref, gn_ref, tril3_ref, e3_ref, selc_ref, init_ref, o_ref) = refs[:11]
    final_ref = refs[11] if head else None
    state_ref, yg_ref = refs[-2:]
    c = pl.program_id(1)

    @pl.when(c == 0)
    def _():
        state_ref[...] = init_ref[...]

    lane = lax.broadcasted_iota(jnp.int32, (CHUNK, LANES), 1)
    li = lax.broadcasted_iota(jnp.int32, (CHUNK, CHUNK), 0)
    cbias = jnp.where(li >= lane, 0.0, NEG_INF)
    cbias = jnp.concatenate([cbias] * HEADS_PER_GROUP, axis=1)

    dt = dt_ref[...]
    a = dt * (-jnp.exp(alog_ref[...]))
    a3 = jnp.concatenate([t.astype(BF16) for t in _split3(a)], axis=0)
    acs = _dot(tril3_ref[...], a3)
    b = jnp.maximum(jnp.log(dt), NEG_INF) - acs
    expa = jnp.exp(acs)
    wst = jnp.exp(acs[CHUNK - 1:CHUNK, :] + b)
    spread = _dot(jnp.concatenate([_pack3(expa, lane, 0.0), _pack3(wst, lane, 0.0)], axis=0), e3_ref[...])
    e_exp = spread[:CHUNK]
    w_exp = spread[CHUNK:]
    u3 = _pack3(acs, lane, 1.0)
    bt_terms = _split3(b.T)
    sub = lax.broadcasted_iota(jnp.int32, (HEADS_PER_GROUP, HEADS_PER_GROUP * CHUNK), 0)
    blk = lax.broadcasted_iota(jnp.int32, (HEADS_PER_GROUP, HEADS_PER_GROUP * CHUNK), 1) // CHUNK
    on_diag = sub == blk

    xs_f = xbc_ref[:, :D_INNER].astype(F32)
    xw_b = (xs_f * w_exp).astype(BF16)
    lane_lo = lane < SSM_HEAD_DIM
    ssq = jnp.zeros((CHUNK, 1), F32)
    gw = HEADS_PER_GROUP * SSM_HEAD_DIM
    for g in range(SSM_GROUPS):
        gc = slice(g * gw, (g + 1) * gw)
        bg_b = xbc_ref[:, D_INNER + g * D_STATE:D_INNER + (g + 1) * D_STATE]
        cg_b = xbc_ref[:, D_INNER + D_BC + g * D_STATE:D_INNER + D_BC + (g + 1) * D_STATE]
        cb_mat = _dot_nt(cg_b, bg_b)
        st = state_ref[:, gc]
        y_off = _dot(cg_b, st.astype(BF16))
        state_ref[:, gc] = st * e_exp[CHUNK - 1:CHUNK, gc] + _dot_tn(bg_b, xw_b[:, gc])
        dyn = [jnp.where(on_diag, jnp.concatenate([t[g * HEADS_PER_GROUP:(g + 1) * HEADS_PER_GROUP, :]]
                                                  * HEADS_PER_GROUP, axis=1), 0.0) for t in bt_terms]
        dyn.append(jnp.zeros_like(dyn[0]))
        v3 = jnp.concatenate([selc_ref[g], jnp.concatenate(dyn, axis=0).astype(BF16)], axis=0)
        seg = _dot(u3, v3)
        m = (jnp.exp(seg + cbias) * jnp.concatenate([cb_mat] * HEADS_PER_GROUP, axis=1)).astype(BF16)
        y_parts = []
        for pr in range(HEADS_PER_GROUP // 2):
            c0 = g * gw + pr * LANES
            xp = xbc_ref[:, c0:c0 + LANES]
            zero = jnp.zeros_like(xp)
            rhs = jnp.concatenate([jnp.where(lane_lo, xp, zero), jnp.where(lane_lo, zero, xp)], axis=0)
            y_parts.append(_dot(m[:, pr * 2 * CHUNK:(pr + 1) * 2 * CHUNK], rhs))
        y = jnp.concatenate(y_parts, axis=1) + y_off * e_exp[:, gc] + dskip_ref[:, gc] * xs_f[:, gc]
        y = y * zs_ref[:, gc].astype(F32)
        ssq = ssq + jnp.sum(y * y, axis=-1, keepdims=True)
        yg_ref[:, gc] = y
    inv = lax.rsqrt(ssq * (1.0 / D_INNER) + RMS_EPS)
    o_ref[...] = (yg_ref[...] * inv * gn_ref[...]).astype(BF16)
    if head:
        final_ref[...] = state_ref[...]


def _ssd_constants():
    tril3 = np.tile(np.tril(np.ones((CHUNK, CHUNK), np.float32)), (1, 3))
    e3 = np.zeros((LANES, D_INNER), np.float32)
    selc = np.zeros((SSM_GROUPS, 3 * REP, HEADS_PER_GROUP * CHUNK), np.float32)
    for t in range(3):
        for h in range(SSM_HEADS):
            e3[REP * t + h, h * SSM_HEAD_DIM:(h + 1) * SSM_HEAD_DIM] = 1.0
            g, k = divmod(h, HEADS_PER_GROUP)
            selc[g, REP * t + h, k * CHUNK:(k + 1) * CHUNK] = 1.0
    return jnp.asarray(tril3, BF16), jnp.asarray(e3, BF16), jnp.asarray(selc, BF16)


def _ssd(zs, xbc, dt, a_log, d_skip, gate_norm, state_init, bsz, nc, head):
    rows = zs.shape[0]
    tril3, e3, selc = _ssd_constants()
    row_spec = lambda n: pl.BlockSpec((CHUNK, n), lambda b, c: (b * nc + c, 0))
    out_specs = [row_spec(D_INNER)]
    out_shape = [jax.ShapeDtypeStruct((rows, D_INNER), BF16)]
    if head:
        out_specs.append(_resident((D_STATE, D_INNER)))
        out_shape.append(jax.ShapeDtypeStruct((D_STATE, D_INNER), F32))
    return pl.pallas_call(
        functools.partial(_ssd_kernel, head=head),
        grid=(bsz, nc),
        in_specs=[row_spec(D_INNER), row_spec(D_XBC), row_spec(LANES),
                  _resident(a_log.shape), _resident(d_skip.shape), _resident(gate_norm.shape),
                  _resident(tril3.shape), _resident(e3.shape), _resident(selc.shape),
                  _resident(state_init.shape)],
        out_specs=out_specs,
        out_shape=out_shape,
        scratch_shapes=[pltpu.VMEM((D_STATE, D_INNER), F32),
                        pltpu.VMEM((CHUNK, D_INNER), F32)],
        compiler_params=_params(2),
        name="mamba_ssd",
    )(zs, xbc, dt, a_log, d_skip, gate_norm, tril3, e3, selc, state_init)


def _proj_res_kernel(x_ref, w_ref, g_ref, r_ref, o_ref, *, transposed):
    y = _dot_tn(x_ref[...], w_ref[...]) if transposed else _dot(x_ref[...], w_ref[...])
    o_ref[...] = r_ref[...] + _rms(y) * g_ref[...]


def _proj_res(x, w, gain, res, head, transposed=False):
    rows = res.shape[0]
    k = w.shape[0]
    tm = _row_tile(head)
    x_spec = pl.BlockSpec((k, tm), lambda i: (0, i)) if transposed else pl.BlockSpec((tm, k), lambda i: (i, 0))
    return pl.pallas_call(
        functools.partial(_proj_res_kernel, transposed=transposed),
        grid=(rows // tm,),
        in_specs=[x_spec, _resident(w.shape), _resident(gain.shape),
                  pl.BlockSpec((tm, D_MODEL), lambda i: (i, 0))],
        out_specs=pl.BlockSpec((tm, D_MODEL), lambda i: (i, 0)),
        out_shape=jax.ShapeDtypeStruct((rows, D_MODEL), F32),
        compiler_params=_params(1),
        name="proj_norm_residual",
    )(x, w, gain, res)


def _ffn_kernel(*refs, tiles_per_seq, tm, tn, head):
    x_ref, gpre_ref, wg_ref, wv_ref, cw_ref, cb_ref, wd_ref, gpost_ref, init_ref, o_ref = refs[:10]
    tail_ref = refs[10] if head else None
    ext_ref, carry_ref, act_ref = refs[-3:]
    first = (pl.program_id(0) % tiles_per_seq) == 0
    x = x_ref[...]
    xn = _normed_input(x, gpre_ref, head)

    def up_conv(w_ref, col):
        u = _dot(xn, w_ref[:, col % D_FF:col % D_FF + tn])
        return _causal_conv(u, first, init_ref, ext_ref, carry_ref, tail_ref, cw_ref, cb_ref, col, FFN_CONV)

    for j in range(D_FF // tn):
        gate = up_conv(wg_ref, j * tn)
        val = up_conv(wv_ref, D_FF + j * tn)
        act_ref[:, j * tn:(j + 1) * tn] = (_silu(gate) * val).astype(BF16)
    y = _dot(act_ref[...], wd_ref[...])
    o_ref[...] = x + _rms(y) * gpost_ref[...]


def _ffn(h, gpre, wg, wv, conv_w, conv_b, wd, gpost, conv_init, seq_rows, head):
    rows = h.shape[0]
    tm, tn = _row_tile(head), FFN_TN
    assert seq_rows % tm == 0 and D_FF % tn == 0
    out_specs = [pl.BlockSpec((tm, D_MODEL), lambda i: (i, 0))]
    out_shape = [jax.ShapeDtypeStruct((rows, D_MODEL), F32)]
    if head:
        out_specs.append(_resident((HALO, 2 * D_FF)))
        out_shape.append(jax.ShapeDtypeStruct((HALO, 2 * D_FF), F32))
    return pl.pallas_call(
        functools.partial(_ffn_kernel, tiles_per_seq=seq_rows // tm, tm=tm, tn=tn, head=head),
        grid=(rows // tm,),
        in_specs=[pl.BlockSpec((tm, D_MODEL), lambda i: (i, 0)), _resident(gpre.shape),
                  _resident(wg.shape), _resident(wv.shape), _resident(conv_w.shape),
                  _resident(conv_b.shape), _resident(wd.shape), _resident(gpost.shape),
                  _resident(conv_init.shape)],
        out_specs=out_specs,
        out_shape=out_shape,
        scratch_shapes=[pltpu.VMEM((HALO + tm, tn), F32),
                        pltpu.VMEM((HALO, 2 * D_FF), F32),
                        pltpu.VMEM((tm, D_FF), BF16)],
        compiler_params=_params(1),
        name="conv_ffn",
    )(h, gpre, wg, wv, conv_w, conv_b, wd, gpost, conv_init)


def _qkv_proj_kernel(x_ref, gq_ref, gkv_ref, wq_ref, wk_ref, wv_ref, qt_ref, k_ref, vt_ref):
    xr = _rms(x_ref[...])
    xq = (xr * gq_ref[...]).astype(BF16)
    xkv = (xr * gkv_ref[...]).astype(BF16)
    scale = 1.0 / math.sqrt(ATTN_HEAD_DIM)
    qt_ref[...] = (_dot(xq, wq_ref[...]) * scale).T.astype(BF16)
    k_ref[...] = _dot(xkv, wk_ref[...]).astype(BF16)
    vt_ref[...] = _dot(xkv, wv_ref[...]).T.astype(BF16)


def _qkv_proj(h, gq, gkv, wq, wk, wv_kv, head):
    rows = h.shape[0]
    tm = _row_tile(head)
    return pl.pallas_call(
        _qkv_proj_kernel,
        grid=(rows // tm,),
        in_specs=[pl.BlockSpec((tm, D_MODEL), lambda i: (i, 0)), _resident(gq.shape), _resident(gkv.shape),
                  _resident(wq.shape), _resident(wk.shape), _resident(wv_kv.shape)],
        out_specs=[pl.BlockSpec((D_ATTN, tm), lambda i: (0, i)), pl.BlockSpec((tm, D_KV), lambda i: (i, 0)),
                   pl.BlockSpec((D_KV, tm), lambda i: (0, i))],
        out_shape=[jax.ShapeDtypeStruct((D_ATTN, rows), BF16), jax.ShapeDtypeStruct((rows, D_KV), BF16),
                   jax.ShapeDtypeStruct((D_KV, rows), BF16)],
        compiler_params=_params(1),
        name="qkv_proj",
    )(h, gq, gkv, wq, wk, wv_kv)


def _attn_kernel(sink_ref, qt_ref, kc_ref, kp_ref, km_ref, vc_ref, vp_ref, vm_ref, ot_ref, *, head):
    c = pl.program_id(1)
    nkeys = N_META + 2 * CHUNK
    ki = lax.broadcasted_iota(jnp.int32, (nkeys, CHUNK), 0)
    qi = lax.broadcasted_iota(jnp.int32, (nkeys, CHUNK), 1)
    kprev = ki - N_META
    kcur = ki - (N_META + CHUNK)
    in_cur = jnp.logical_and(kcur >= 0, kcur <= qi)
    if head:
        visible = jnp.logical_and(in_cur, kcur >= PAD)
    else:
        in_prev = jnp.logical_and(jnp.logical_and(kprev >= 0, kprev < CHUNK), jnp.logical_and(kprev > qi, c >= 1))
        visible = jnp.logical_or(jnp.logical_or(ki < N_META, in_prev), in_cur)
    bias = jnp.concatenate([jnp.where(visible, 0.0, NEG_INF)] * N_Q_HEADS, axis=1)
    k_all = jnp.concatenate([km_ref[PAD:CHUNK, :], kp_ref[...], kc_ref[...]], axis=0)
    qw = Q_PER_KV * CHUNK
    zeros = jnp.zeros((ATTN_HEAD_DIM, qw), BF16)

    def q_heads(kh):
        return jnp.concatenate([qt_ref[hq * ATTN_HEAD_DIM:(hq + 1) * ATTN_HEAD_DIM, :]
                                for hq in range(kh * Q_PER_KV, (kh + 1) * Q_PER_KV)], axis=1)

    halves = []
    for pair in range(N_KV_HEADS // 2):
        qbd = jnp.concatenate([jnp.concatenate([q_heads(2 * pair), zeros], axis=1),
                               jnp.concatenate([zeros, q_heads(2 * pair + 1)], axis=1)], axis=0)
        halves.append(_dot(k_all[:, pair * LANES:(pair + 1) * LANES], qbd))
    s = jnp.concatenate(halves, axis=1) + bias
    sink = jnp.concatenate([jnp.full((1, CHUNK), sink_ref[hq], F32) for hq in range(N_Q_HEADS)], axis=1)
    m = jnp.maximum(jnp.max(s, axis=0, keepdims=True), sink)
    p = jnp.exp(s - m)
    inv_den = 1.0 / (jnp.sum(p, axis=0, keepdims=True) + jnp.exp(sink - m))
    pb = p.astype(BF16)
    pad_rows = jnp.zeros((PAD, qw), BF16)
    for kh in range(N_KV_HEADS):
        rows = slice(kh * ATTN_HEAD_DIM, (kh + 1) * ATTN_HEAD_DIM)
        vt = jnp.concatenate([vm_ref[rows, :], vp_ref[rows, :], vc_ref[rows, :]], axis=1)
        p_all = jnp.concatenate([pad_rows, pb[:, kh * qw:(kh + 1) * qw]], axis=0)
        o = _dot(vt, p_all) * inv_den[:, kh * qw:(kh + 1) * qw]
        for g in range(Q_PER_KV):
            hq = kh * Q_PER_KV + g
            ot_ref[hq * ATTN_HEAD_DIM:(hq + 1) * ATTN_HEAD_DIM, :] = o[:, g * CHUNK:(g + 1) * CHUNK].astype(BF16)


def _attn(qt, k, vt, k_head, vt_head, sinks, bsz, nc, head):
    rows = qt.shape[1]
    cur = lambda b, c: b * nc + c
    prev = lambda b, c: b * nc + jnp.maximum(c - 1, 0)
    k_spec = lambda f: pl.BlockSpec((CHUNK, D_KV), lambda b, c: (f(b, c), 0))
    vt_spec = lambda f: pl.BlockSpec((D_KV, CHUNK), lambda b, c: (0, f(b, c)))
    return pl.pallas_call(
        functools.partial(_attn_kernel, head=head),
        grid=(bsz, nc),
        in_specs=[pl.BlockSpec(memory_space=pltpu.SMEM),
                  pl.BlockSpec((D_ATTN, CHUNK), lambda b, c: (0, cur(b, c))),
                  k_spec(cur), k_spec(prev), _resident(k_head.shape),
                  vt_spec(cur), vt_spec(prev), _resident(vt_head.shape)],
        out_specs=pl.BlockSpec((D_ATTN, CHUNK), lambda b, c: (0, cur(b, c))),
        out_shape=jax.ShapeDtypeStruct((D_ATTN, rows), BF16),
        compiler_params=_params(2),
        name="swa_sink_attention",
    )(sinks, qt, k, k, k_head, vt, vt, vt_head)


def kernel(x, meta_tokens, a_norm_pre, a_w_in, a_conv_w, a_conv_b, a_dt_bias, a_a_log, a_d_skip, a_gate_norm, a_w_out, a_norm_post, kv_norm, w_kv, b_norm_pre, b_w_q, b_sinks, b_w_o, b_norm_post, f_norm_pre, f_w_up, f_conv_w, f_conv_b, f_w_down, f_norm_post):
    bsz, seq, _ = x.shape
    assert seq % BODY_TM == 0
    depth = f_norm_pre.shape[0]
    n_a = a_norm_pre.shape[0]

    def row(v):
        return v.astype(F32).reshape(1, -1)

    def rep3(v):
        return jnp.pad(jnp.concatenate([v] * 3, axis=1), ((0, 0), (0, LANES - 3 * REP)))

    passes = {True: (1, CHUNK, 1), False: (bsz, seq, seq // CHUNK)}
    hs = {True: jnp.concatenate([jnp.zeros((PAD, D_MODEL), F32), meta_tokens.astype(F32)], axis=0),
          False: x.astype(F32).reshape(bsz * seq, D_MODEL)}

    kv = {}
    for i in range(depth):
        if i < n_a:
            j = i
            w_in = a_w_in[j]
            wz, wx = w_in[:, :D_INNER].astype(BF16), w_in[:, D_INNER:D_INNER + D_XBC].astype(BF16)
            wdt = rep3(w_in[:, D_INNER + D_XBC:]).astype(BF16)
            w_out = a_w_out[j].astype(BF16)
            conv_init = jnp.zeros((HALO, D_XBC), F32)
            state_init = jnp.zeros((D_STATE, D_INNER), F32)
            for head in (True, False):
                nseq, seq_rows, nc = passes[head]
                outs = _in_proj(hs[head], row(a_norm_pre[j]), wz, wx, wdt, a_conv_w[j].astype(F32),
                                row(a_conv_b[j]), rep3(row(a_dt_bias[j])), conv_init, seq_rows, head)
                ys = _ssd(outs[0], outs[1], outs[2], rep3(row(a_a_log[j])),
                          row(jnp.repeat(a_d_skip[j], SSM_HEAD_DIM)), row(a_gate_norm[j]), state_init, nseq, nc, head)
                hs[head] = _proj_res(ys[0], w_out, row(a_norm_post[j]), hs[head], head)
                if head:
                    conv_init, state_init = outs[3], ys[1]
        else:
            j = i - n_a
            wq, wo = b_w_q[j].astype(BF16), b_w_o[j].astype(BF16)
            wk, wv_kv = w_kv[:, :D_KV].astype(BF16), w_kv[:, D_KV:].astype(BF16)
            for head in (True, False):
                nseq, seq_rows, nc = passes[head]
                qt, k_new, vt_new = _qkv_proj(hs[head], row(b_norm_pre[j]), row(kv_norm), wq, wk, wv_kv, head)
                if j == 0:
                    kv[head] = (k_new, vt_new)
                att_t = _attn(qt, kv[head][0], kv[head][1], kv[True][0], kv[True][1],
                              b_sinks[j].astype(F32), nseq, nc, head)
                hs[head] = _proj_res(att_t, wo, row(b_norm_post[j]), hs[head], head, transposed=True)
        w_up = f_w_up[i]
        wg, wv, wd = w_up[:, :D_FF].astype(BF16), w_up[:, D_FF:].astype(BF16), f_w_down[i].astype(BF16)
        conv_init = jnp.zeros((HALO, 2 * D_FF), F32)
        for head in (True, False):
            nseq, seq_rows, nc = passes[head]
            outs = _ffn(hs[head], row(f_norm_pre[i]), wg, wv, f_conv_w[i].astype(F32), row(f_conv_b[i]), wd,
                        row(f_norm_post[i]), conv_init, seq_rows, head)
            hs[head] = outs[0]
            if head:
                conv_init = outs[1]
    return hs[False].reshape(bsz, seq, D_MODEL)
```

```python
import functools
import math

import jax
import jax.numpy as jnp
import numpy as np
from jax import lax
from jax.experimental import pallas as pl
from jax.experimental.pallas import tpu as pltpu

F32 = jnp.float32
BF16 = jnp.bfloat16

D_MODEL = 1024
N_META = 16
CHUNK = 128
PAD = CHUNK - N_META

D_INNER = 2048
SSM_HEAD_DIM = 64
SSM_HEADS = 32
SSM_GROUPS = 4
HEADS_PER_GROUP = SSM_HEADS // SSM_GROUPS
D_STATE = 128
SSM_CONV = 4
D_BC = SSM_GROUPS * D_STATE
D_XBC = D_INNER + 2 * D_BC

ATTN_HEAD_DIM = 64
N_Q_HEADS = 16
N_KV_HEADS = 4
Q_PER_KV = N_Q_HEADS // N_KV_HEADS
D_ATTN = N_Q_HEADS * ATTN_HEAD_DIM
D_KV = N_KV_HEADS * ATTN_HEAD_DIM

D_FF = 2816
FFN_CONV = 3

RMS_EPS = 1e-6
NEG_INF = -1e30

LANES = 128
HALO = 8
REP = SSM_HEADS
assert 3 * REP <= LANES
VMEM_LIMIT = 56 * 1024 * 1024
BODY_TM = 4 * CHUNK
IN_PROJ_TN = 512
XBC_BLOCK = 1024
XBC_BLOCKS = D_XBC // XBC_BLOCK
assert D_INNER % XBC_BLOCK == 0 and D_XBC % XBC_BLOCK == 0
FFN_TN = 256


def _rms(x):
    return x * lax.rsqrt(jnp.mean(x * x, axis=-1, keepdims=True) + RMS_EPS)


def _silu(x):
    return x / (1.0 + jnp.exp(-x))


def _dot(a, b):
    return jnp.dot(a, b, preferred_element_type=F32)


def _dot_nt(a, b):
    return lax.dot_general(a, b, (((1,), (1,)), ((), ())), preferred_element_type=F32)


def _dot_tn(a, b):
    return lax.dot_general(a, b, (((0,), (0,)), ((), ())), preferred_element_type=F32)


def _resident(shape):
    return pl.BlockSpec(shape, lambda *_: (0,) * len(shape), pipeline_mode=pl.Buffered(1))


def _params(n_axes):
    return pltpu.CompilerParams(dimension_semantics=("arbitrary",) * n_axes,
                                vmem_limit_bytes=VMEM_LIMIT)


def _row_tile(head):
    return CHUNK if head else BODY_TM


def _causal_conv(u, first, init_ref, ext_ref, carry_ref, tail_ref, cw_ref, cb_ref, col, taps):
    tm, tn = u.shape
    cols = slice(col, col + tn)
    ext_ref[0:HALO, :] = jnp.where(first, init_ref[:, cols], carry_ref[:, cols])
    ext_ref[HALO:HALO + tm, :] = u
    carry_ref[:, cols] = u[tm - HALO:tm, :]
    if tail_ref is not None:
        tail_ref[:, cols] = u[tm - HALO:tm, :]
    out = cb_ref[:, cols] + cw_ref[taps - 1:taps, cols] * u
    for k in range(taps - 1):
        s = taps - 1 - k
        out = out + cw_ref[k:k + 1, cols] * ext_ref[HALO - s:HALO - s + tm, :]
    return out


def _normed_input(x, gain_ref, head):
    xn = _rms(x) * gain_ref[...]
    if head:
        row = lax.broadcasted_iota(jnp.int32, (x.shape[0], 1), 0)
        xn = jnp.where(row >= PAD, xn, 0.0)
    return xn.astype(BF16)


def _in_proj_kernel(*refs, tiles_per_seq, tm, tn, head):
    x_ref, g_ref, wz_ref = refs[:3]
    wx_refs = refs[3:3 + XBC_BLOCKS]
    wdt_ref, cw_ref, cb_ref, dtb_ref, init_ref, zs_ref, xbc_ref, dt_ref = refs[3 + XBC_BLOCKS:11 + XBC_BLOCKS]
    tail_ref = refs[11 + XBC_BLOCKS] if head else None
    ext_ref, carry_ref = refs[-2:]
    first = (pl.program_id(0) % tiles_per_seq) == 0
    xn = _normed_input(x_ref[...], g_ref, head)
    valid = lax.broadcasted_iota(jnp.int32, (tm, 1), 0) >= PAD
    for n0 in range(0, D_INNER, tn):
        zs_ref[:, n0:n0 + tn] = _silu(_dot(xn, wz_ref[:, n0:n0 + tn])).astype(BF16)
    for n0 in range(0, D_XBC, tn):
        w_ref, w0 = wx_refs[n0 // XBC_BLOCK], n0 % XBC_BLOCK
        u = _dot(xn, w_ref[:, w0:w0 + tn])
        y = _silu(_causal_conv(u, first, init_ref, ext_ref, carry_ref, tail_ref, cw_ref, cb_ref, n0, SSM_CONV))
        if head and n0 >= D_INNER:
            y = jnp.where(valid, y, 0.0)
        xbc_ref[:, n0:n0 + tn] = y.astype(BF16)
    dt_raw = _dot(xn, wdt_ref[...]) + dtb_ref[...]
    dt = jnp.maximum(dt_raw, 0.0) + jnp.log1p(jnp.exp(-jnp.abs(dt_raw)))
    dt_ref[...] = jnp.where(valid, dt, 0.0) if head else dt


def _in_proj(h, gain, w_in, layer, wdt, conv_w, conv_b, dt_bias, conv_init, seq_rows, head):
    rows = h.shape[0]
    tm, tn = _row_tile(head), IN_PROJ_TN
    assert seq_rows % tm == 0 and D_INNER % tn == 0 and XBC_BLOCK % tn == 0
    row_spec = lambda n: pl.BlockSpec((tm, n), lambda i: (i, 0))
    w_block = lambda width, idx: pl.BlockSpec((None, D_MODEL, width), lambda i: (layer, 0, idx),
                                              pipeline_mode=pl.Buffered(1))
    out_specs = [row_spec(D_INNER), row_spec(D_XBC), row_spec(LANES)]
    out_shape = [jax.ShapeDtypeStruct((rows, D_INNER), BF16), jax.ShapeDtypeStruct((rows, D_XBC), BF16),
                 jax.ShapeDtypeStruct((rows, LANES), F32)]
    if head:
        out_specs.append(_resident((HALO, D_XBC)))
        out_shape.append(jax.ShapeDtypeStruct((HALO, D_XBC), F32))
    return pl.pallas_call(
        functools.partial(_in_proj_kernel, tiles_per_seq=seq_rows // tm, tm=tm, tn=tn, head=head),
        grid=(rows // tm,),
        in_specs=[row_spec(D_MODEL), _resident((1, D_MODEL)), w_block(D_INNER, 0)]
                 + [w_block(XBC_BLOCK, D_INNER // XBC_BLOCK + k) for k in range(XBC_BLOCKS)]
                 + [_resident(wdt.shape), _resident(conv_w.shape), _resident(conv_b.shape),
                    _resident(dt_bias.shape), _resident(conv_init.shape)],
        out_specs=out_specs,
        out_shape=out_shape,
        scratch_shapes=[pltpu.VMEM((HALO + tm, tn), F32), pltpu.VMEM((HALO, D_XBC), F32)],
        compiler_params=_params(1),
        name="mamba_in_proj",
    )(h, gain, w_in, *([w_in] * XBC_BLOCKS), wdt, conv_w, conv_b, dt_bias, conv_init)


def _split3(x):
    hi = x.astype(BF16).astype(F32)
    r = x - hi
    mid = r.astype(BF16).astype(F32)
    return hi, mid, r - mid


def _pack3(x, lane, fill):
    hi, mid, lo = _split3(x)
    packed = jnp.where(lane < REP, hi, jnp.where(lane < 2 * REP, mid, jnp.where(lane < 3 * REP, lo, fill)))
    return packed.astype(BF16)


def _ssd_kernel(*refs, head):
    (zs_ref, xbc_ref, dt_ref, alog_ref, dskip_ref, gn_ref, tril3_ref, e3_ref, selc_ref, init_ref, o_ref) = refs[:11]
    final_ref = refs[11] if head else None
    state_ref, yg_ref = refs[-2:]
    c = pl.program_id(1)

    @pl.when(c == 0)
    def _():
        state_ref[...] = init_ref[...]

    lane = lax.broadcasted_iota(jnp.int32, (CHUNK, LANES), 1)
    li = lax.broadcasted_iota(jnp.int32, (CHUNK, CHUNK), 0)
    cbias = jnp.where(li >= lane, 0.0, NEG_INF)
    cbias = jnp.concatenate([cbias] * HEADS_PER_GROUP, axis=1)

    dt = dt_ref[...]
    a = dt * (-jnp.exp(alog_ref[...]))
    a3 = jnp.concatenate([t.astype(BF16) for t in _split3(a)], axis=0)
    acs = _dot(tril3_ref[...], a3)
    b = jnp.maximum(jnp.log(dt), NEG_INF) - acs
    expa = jnp.exp(acs)
    wst = jnp.exp(acs[CHUNK - 1:CHUNK, :] + b)
    spread = _dot(jnp.concatenate([_pack3(expa, lane, 0.0), _pack3(wst, lane, 0.0)], axis=0), e3_ref[...])
    e_exp = spread[:CHUNK]
    w_exp = spread[CHUNK:]
    u3 = _pack3(acs, lane, 1.0)
    bt_terms = _split3(b.T)
    sub = lax.broadcasted_iota(jnp.int32, (HEADS_PER_GROUP, HEADS_PER_GROUP * CHUNK), 0)
    blk = lax.broadcasted_iota(jnp.int32, (HEADS_PER_GROUP, HEADS_PER_GROUP * CHUNK), 1) // CHUNK
    on_diag = sub == blk

    xs_f = xbc_ref[:, :D_INNER].astype(F32)
    xw_b = (xs_f * w_exp).astype(BF16)
    lane_lo = lane < SSM_HEAD_DIM
    ssq = jnp.zeros((CHUNK, 1), F32)
    gw = HEADS_PER_GROUP * SSM_HEAD_DIM
    for g in range(SSM_GROUPS):
        gc = slice(g * gw, (g + 1) * gw)
        bg_b = xbc_ref[:, D_INNER + g * D_STATE:D_INNER + (g + 1) * D_STATE]
        cg_b = xbc_ref[:, D_INNER + D_BC + g * D_STATE:D_INNER + D_BC + (g + 1) * D_STATE]
        cb_mat = _dot_nt(cg_b, bg_b)
        st = state_ref[:, gc]
        y_off = _dot(cg_b, st.astype(BF16))
        state_ref[:, gc] = st * e_exp[CHUNK - 1:CHUNK, gc] + _dot_tn(bg_b, xw_b[:, gc])
        dyn = [jnp.where(on_diag, jnp.concatenate([t[g * HEADS_PER_GROUP:(g + 1) * HEADS_PER_GROUP, :]]
                                                  * HEADS_PER_GROUP, axis=1), 0.0) for t in bt_terms]
        dyn.append(jnp.zeros_like(dyn[0]))
        v3 = jnp.concatenate([selc_ref[g], jnp.concatenate(dyn, axis=0).astype(BF16)], axis=0)
        seg = _dot(u3, v3)
        m = (jnp.exp(seg + cbias) * jnp.concatenate([cb_mat] * HEADS_PER_GROUP, axis=1)).astype(BF16)
        y_parts = []
        for pr in range(HEADS_PER_GROUP // 2):
            c0 = g * gw + pr * LANES
            xp = xbc_ref[:, c0:c0 + LANES]
            zero = jnp.zeros_like(xp)
            rhs = jnp.concatenate([jnp.where(lane_lo, xp, zero), jnp.where(lane_lo, zero, xp)], axis=0)
            y_parts.append(_dot(m[:, pr * 2 * CHUNK:(pr + 1) * 2 * CHUNK], rhs))
        y = jnp.concatenate(y_parts, axis=1) + y_off * e_exp[:, gc] + dskip_ref[:, gc] * xs_f[:, gc]
        y = y * zs_ref[:, gc].astype(F32)
        ssq = ssq + jnp.sum(y * y, axis=-1, keepdims=True)
        yg_ref[:, gc] = y
    inv = lax.rsqrt(ssq * (1.0 / D_INNER) + RMS_EPS)
    o_ref[...] = (yg_ref[...] * inv * gn_ref[...]).astype(BF16)
    if head:
        final_ref[...] = state_ref[...]


def _ssd_constants():
    tril3 = np.tile(np.tril(np.ones((CHUNK, CHUNK), np.float32)), (1, 3))
    e3 = np.zeros((LANES, D_INNER), np.float32)
    selc = np.zeros((SSM_GROUPS, 3 * REP, HEADS_PER_GROUP * CHUNK), np.float32)
    for t in range(3):
        for h in range(SSM_HEADS):
            e3[REP * t + h, h * SSM_HEAD_DIM:(h + 1) * SSM_HEAD_DIM] = 1.0
            g, k = divmod(h, HEADS_PER_GROUP)
            selc[g, REP * t + h, k * CHUNK:(k + 1) * CHUNK] = 1.0
    return jnp.asarray(tril3, BF16), jnp.asarray(e3, BF16), jnp.asarray(selc, BF16)


def _ssd(zs, xbc, dt, a_log, d_skip, gate_norm, state_init, bsz, nc, head):
    rows = zs.shape[0]
    tril3, e3, selc = _ssd_constants()
    row_spec = lambda n: pl.BlockSpec((CHUNK, n), lambda b, c: (b * nc + c, 0))
    out_specs = [row_spec(D_INNER)]
    out_shape = [jax.ShapeDtypeStruct((rows, D_INNER), BF16)]
    if head:
        out_specs.append(_resident((D_STATE, D_INNER)))
        out_shape.append(jax.ShapeDtypeStruct((D_STATE, D_INNER), F32))
    return pl.pallas_call(
        functools.partial(_ssd_kernel, head=head),
        grid=(bsz, nc),
        in_specs=[row_spec(D_INNER), row_spec(D_XBC), row_spec(LANES),
                  _resident(a_log.shape), _resident(d_skip.shape), _resident(gate_norm.shape),
                  _resident(tril3.shape), _resident(e3.shape), _resident(selc.shape),
                  _resident(state_init.shape)],
        out_specs=out_specs,
        out_shape=out_shape,
        scratch_shapes=[pltpu.VMEM((D_STATE, D_INNER), F32),
                        pltpu.VMEM((CHUNK, D_INNER), F32)],
        compiler_params=_params(2),
        name="mamba_ssd",
    )(zs, xbc, dt, a_log, d_skip, gate_norm, tril3, e3, selc, state_init)


def _ffn_kernel(*refs, tiles_per_seq, tm, tn, head, mix_transposed):
    (res_ref, mix_ref, wmix_ref, gmix_ref, gpre_ref, wg_ref, wv_ref, cw_ref, cb_ref, wd_ref, gpost_ref,
     init_ref, o_ref) = refs[:13]
    tail_ref = refs[13] if head else None
    ext_ref, carry_ref, act_ref = refs[-3:]
    first = (pl.program_id(0) % tiles_per_seq) == 0
    mix = _dot_tn(mix_ref[...], wmix_ref[...]) if mix_transposed else _dot(mix_ref[...], wmix_ref[...])
    x = res_ref[...] + _rms(mix) * gmix_ref[...]
    xn = _normed_input(x, gpre_ref, head)

    def up_conv(w_ref, col):
        u = _dot(xn, w_ref[:, col % D_FF:col % D_FF + tn])
        return _causal_conv(u, first, init_ref, ext_ref, carry_ref, tail_ref, cw_ref, cb_ref, col, FFN_CONV)

    for j in range(D_FF // tn):
        gate = up_conv(wg_ref, j * tn)
        val = up_conv(wv_ref, D_FF + j * tn)
        act_ref[:, j * tn:(j + 1) * tn] = (_silu(gate) * val).astype(BF16)
    y = _dot(act_ref[...], wd_ref[...])
    o_ref[...] = x + _rms(y) * gpost_ref[...]


def _ffn(res, mix, wmix, gmix, gpre, w_up, w_down, layer, conv_w, conv_b, gpost, conv_init, seq_rows, head,
         mix_transposed):
    rows = res.shape[0]
    k = wmix.shape[0]
    tm, tn = _row_tile(head), FFN_TN
    assert seq_rows % tm == 0 and D_FF % tn == 0
    row_spec = pl.BlockSpec((tm, D_MODEL), lambda i: (i, 0))
    mix_spec = pl.BlockSpec((k, tm), lambda i: (0, i)) if mix_transposed else pl.BlockSpec((tm, k), lambda i: (i, 0))
    up_half = lambda half: pl.BlockSpec((None, D_MODEL, D_FF), lambda i: (layer, 0, half), pipeline_mode=pl.Buffered(1))
    down_spec = pl.BlockSpec((None, D_FF, D_MODEL), lambda i: (layer, 0, 0), pipeline_mode=pl.Buffered(1))
    out_specs = [row_spec]
    out_shape = [jax.ShapeDtypeStruct((rows, D_MODEL), F32)]
    if head:
        out_specs.append(_resident((HALO, 2 * D_FF)))
        out_shape.append(jax.ShapeDtypeStruct((HALO, 2 * D_FF), F32))
    return pl.pallas_call(
        functools.partial(_ffn_kernel, tiles_per_seq=seq_rows // tm, tm=tm, tn=tn, head=head,
                          mix_transposed=mix_transposed),
        grid=(rows // tm,),
        in_specs=[row_spec, mix_spec, _resident(wmix.shape), _resident(gmix.shape), _resident(gpre.shape),
                  up_half(0), up_half(1), _resident(conv_w.shape), _resident(conv_b.shape), down_spec,
                  _resident(gpost.shape), _resident(conv_init.shape)],
        out_specs=out_specs,
        out_shape=out_shape,
        scratch_shapes=[pltpu.VMEM((HALO + tm, tn), F32),
                        pltpu.VMEM((HALO, 2 * D_FF), F32),
                        pltpu.VMEM((tm, D_FF), BF16)],
        compiler_params=_params(1),
        name="mix_proj_conv_ffn",
    )(res, mix, wmix, gmix, gpre, w_up, w_up, conv_w, conv_b, w_down, gpost, conv_init)


def _qkv_proj_kernel(x_ref, gq_ref, gkv_ref, wq_ref, wk_ref, wv_ref, qt_ref, k_ref, vt_ref):
    xr = _rms(x_ref[...])
    xq = (xr * gq_ref[...]).astype(BF16)
    xkv = (xr * gkv_ref[...]).astype(BF16)
    scale = 1.0 / math.sqrt(ATTN_HEAD_DIM)
    qt_ref[...] = (_dot(xq, wq_ref[...]) * scale).T.astype(BF16)
    k_ref[...] = _dot(xkv, wk_ref[...]).astype(BF16)
    vt_ref[...] = _dot(xkv, wv_ref[...]).T.astype(BF16)


def _qkv_proj(h, gq, gkv, wq, wkv, head):
    rows = h.shape[0]
    tm = _row_tile(head)
    return pl.pallas_call(
        _qkv_proj_kernel,
        grid=(rows // tm,),
        in_specs=[pl.BlockSpec((tm, D_MODEL), lambda i: (i, 0)), _resident(gq.shape), _resident(gkv.shape),
                  _resident(wq.shape), pl.BlockSpec((D_MODEL, D_KV), lambda i: (0, 0), pipeline_mode=pl.Buffered(1)),
                  pl.BlockSpec((D_MODEL, D_KV), lambda i: (0, 1), pipeline_mode=pl.Buffered(1))],
        out_specs=[pl.BlockSpec((D_ATTN, tm), lambda i: (0, i)), pl.BlockSpec((tm, D_KV), lambda i: (i, 0)),
                   pl.BlockSpec((D_KV, tm), lambda i: (0, i))],
        out_shape=[jax.ShapeDtypeStruct((D_ATTN, rows), BF16), jax.ShapeDtypeStruct((rows, D_KV), BF16),
                   jax.ShapeDtypeStruct((D_KV, rows), BF16)],
        compiler_params=_params(1),
        name="qkv_proj",
    )(h, gq, gkv, wq, wkv, wkv)


def _attn_kernel(sink_ref, qt_ref, kc_ref, kp_ref, km_ref, vc_ref, vp_ref, vm_ref, ot_ref, *, head):
    c = pl.program_id(1)
    nkeys = N_META + 2 * CHUNK
    ki = lax.broadcasted_iota(jnp.int32, (nkeys, CHUNK), 0)
    qi = lax.broadcasted_iota(jnp.int32, (nkeys, CHUNK), 1)
    kprev = ki - N_META
    kcur = ki - (N_META + CHUNK)
    in_cur = jnp.logical_and(kcur >= 0, kcur <= qi)
    if head:
        visible = jnp.logical_and(in_cur, kcur >= PAD)
    else:
        in_prev = jnp.logical_and(jnp.logical_and(kprev >= 0, kprev < CHUNK), jnp.logical_and(kprev > qi, c >= 1))
        visible = jnp.logical_or(jnp.logical_or(ki < N_META, in_prev), in_cur)
    bias = jnp.concatenate([jnp.where(visible, 0.0, NEG_INF)] * N_Q_HEADS, axis=1)
    k_all = jnp.concatenate([km_ref[PAD:CHUNK, :], kp_ref[...], kc_ref[...]], axis=0)
    qw = Q_PER_KV * CHUNK
    zeros = jnp.zeros((ATTN_HEAD_DIM, qw), BF16)

    def q_heads(kh):
        return jnp.concatenate([qt_ref[hq * ATTN_HEAD_DIM:(hq + 1) * ATTN_HEAD_DIM, :]
                                for hq in range(kh * Q_PER_KV, (kh + 1) * Q_PER_KV)], axis=1)

    halves = []
    for pair in range(N_KV_HEADS // 2):
        qbd = jnp.concatenate([jnp.concatenate([q_heads(2 * pair), zeros], axis=1),
                               jnp.concatenate([zeros, q_heads(2 * pair + 1)], axis=1)], axis=0)
        halves.append(_dot(k_all[:, pair * LANES:(pair + 1) * LANES], qbd))
    s = jnp.concatenate(halves, axis=1) + bias
    sink = jnp.concatenate([jnp.full((1, CHUNK), sink_ref[hq], F32) for hq in range(N_Q_HEADS)], axis=1)
    m = jnp.maximum(jnp.max(s, axis=0, keepdims=True), sink)
    p = jnp.exp(s - m)
    inv_den = 1.0 / (jnp.sum(p, axis=0, keepdims=True) + jnp.exp(sink - m))
    pb = p.astype(BF16)
    pad_rows = jnp.zeros((PAD, qw), BF16)
    for kh in range(N_KV_HEADS):
        rows = slice(kh * ATTN_HEAD_DIM, (kh + 1) * ATTN_HEAD_DIM)
        vt = jnp.concatenate([vm_ref[rows, :], vp_ref[rows, :], vc_ref[rows, :]], axis=1)
        p_all = jnp.concatenate([pad_rows, pb[:, kh * qw:(kh + 1) * qw]], axis=0)
        o = _dot(vt, p_all) * inv_den[:, kh * qw:(kh + 1) * qw]
        for g in range(Q_PER_KV):
            hq = kh * Q_PER_KV + g
            ot_ref[hq * ATTN_HEAD_DIM:(hq + 1) * ATTN_HEAD_DIM, :] = o[:, g * CHUNK:(g + 1) * CHUNK].astype(BF16)


def _attn(qt, k, vt, k_head, vt_head, sinks, bsz, nc, head):
    rows = qt.shape[1]
    cur = lambda b, c: b * nc + c
    prev = lambda b, c: b * nc + jnp.maximum(c - 1, 0)
    k_spec = lambda f: pl.BlockSpec((CHUNK, D_KV), lambda b, c: (f(b, c), 0))
    vt_spec = lambda f: pl.BlockSpec((D_KV, CHUNK), lambda b, c: (0, f(b, c)))
    return pl.pallas_call(
        functools.partial(_attn_kernel, head=head),
        grid=(bsz, nc),
        in_specs=[pl.BlockSpec(memory_space=pltpu.SMEM),
                  pl.BlockSpec((D_ATTN, CHUNK), lambda b, c: (0, cur(b, c))),
                  k_spec(cur), k_spec(prev), _resident(k_head.shape),
                  vt_spec(cur), vt_spec(prev), _resident(vt_head.shape)],
        out_specs=pl.BlockSpec((D_ATTN, CHUNK), lambda b, c: (0, cur(b, c))),
        out_shape=jax.ShapeDtypeStruct((D_ATTN, rows), BF16),
        compiler_params=_params(2),
        name="swa_sink_attention",
    )(sinks, qt, k, k, k_head, vt, vt, vt_head)


def kernel(x, meta_tokens, a_norm_pre, a_w_in, a_conv_w, a_conv_b, a_dt_bias, a_a_log, a_d_skip, a_gate_norm, a_w_out, a_norm_post, kv_norm, w_kv, b_norm_pre, b_w_q, b_sinks, b_w_o, b_norm_post, f_norm_pre, f_w_up, f_conv_w, f_conv_b, f_w_down, f_norm_post):
    bsz, seq, _ = x.shape
    assert seq % BODY_TM == 0
    depth = f_norm_pre.shape[0]
    n_a = a_norm_pre.shape[0]

    def row(v):
        return v.astype(F32).reshape(1, -1)

    def rep3(v):
        return jnp.pad(jnp.concatenate([v] * 3, axis=1), ((0, 0), (0, LANES - 3 * REP)))

    passes = {True: (1, CHUNK, 1), False: (bsz, seq, seq // CHUNK)}
    hs = {True: jnp.concatenate([jnp.zeros((PAD, D_MODEL), F32), meta_tokens.astype(F32)], axis=0),
          False: x.astype(F32).reshape(bsz * seq, D_MODEL)}

    w_in_b, w_up_b, w_down_b = a_w_in.astype(BF16), f_w_up.astype(BF16), f_w_down.astype(BF16)
    kv = {}
    for i in range(depth):
        mixes = {}
        if i < n_a:
            j = i
            wdt = rep3(a_w_in[j][:, D_INNER + D_XBC:]).astype(BF16)
            w_mix, g_mix, mix_transposed = a_w_out[j].astype(BF16), row(a_norm_post[j]), False
            conv_init = jnp.zeros((HALO, D_XBC), F32)
            state_init = jnp.zeros((D_STATE, D_INNER), F32)
            for head in (True, False):
                nseq, seq_rows, nc = passes[head]
                outs = _in_proj(hs[head], row(a_norm_pre[j]), w_in_b, j, wdt, a_conv_w[j].astype(F32),
                                row(a_conv_b[j]), rep3(row(a_dt_bias[j])), conv_init, seq_rows, head)
                ys = _ssd(outs[0], outs[1], outs[2], rep3(row(a_a_log[j])),
                          row(jnp.repeat(a_d_skip[j], SSM_HEAD_DIM)), row(a_gate_norm[j]), state_init, nseq, nc, head)
                mixes[head] = ys[0]
                if head:
                    conv_init, state_init = outs[3], ys[1]
        else:
            j = i - n_a
            wq, wkv = b_w_q[j].astype(BF16), w_kv.astype(BF16)
            w_mix, g_mix, mix_transposed = b_w_o[j].astype(BF16), row(b_norm_post[j]), True
            for head in (True, False):
                nseq, seq_rows, nc = passes[head]
                qt, k_new, vt_new = _qkv_proj(hs[head], row(b_norm_pre[j]), row(kv_norm), wq, wkv, head)
                if j == 0:
                    kv[head] = (k_new, vt_new)
                mixes[head] = _attn(qt, kv[head][0], kv[head][1], kv[True][0], kv[True][1],
                                    b_sinks[j].astype(F32), nseq, nc, head)
        conv_init = jnp.zeros((HALO, 2 * D_FF), F32)
        for head in (True, False):
            nseq, seq_rows, nc = passes[head]
            outs = _ffn(hs[head], mixes[head], w_mix, g_mix, row(f_norm_pre[i]), w_up_b, w_down_b, i,
                        f_conv_w[i].astype(F32), row(f_conv_b[i]), row(f_norm_post[i]), conv_init, seq_rows, head,
                        mix_transposed)
            hs[head] = outs[0]
            if head:
                conv_init = outs[1]
    return hs[False].reshape(bsz, seq, D_MODEL)
```

```python
import functools
import math

import jax
import jax.numpy as jnp
import numpy as np
from jax import lax
from jax.experimental import pallas as pl
from jax.experimental.pallas import tpu as pltpu

F32 = jnp.float32
BF16 = jnp.bfloat16

D_MODEL = 1024
N_META = 16
CHUNK = 128
PAD = CHUNK - N_META

D_INNER = 2048
SSM_HEAD_DIM = 64
SSM_HEADS = 32
SSM_GROUPS = 4
HEADS_PER_GROUP = SSM_HEADS // SSM_GROUPS
D_STATE = 128
SSM_CONV = 4
D_BC = SSM_GROUPS * D_STATE
D_XBC = D_INNER + 2 * D_BC

ATTN_HEAD_DIM = 64
N_Q_HEADS = 16
N_KV_HEADS = 4
Q_PER_KV = N_Q_HEADS // N_KV_HEADS
D_ATTN = N_Q_HEADS * ATTN_HEAD_DIM
D_KV = N_KV_HEADS * ATTN_HEAD_DIM

D_FF = 2816
FFN_CONV = 3

RMS_EPS = 1e-6
NEG_INF = -1e30

LANES = 128
HALO = 8
REP = SSM_HEADS
assert 3 * REP <= LANES
VMEM_LIMIT = 56 * 1024 * 1024
BODY_TM = 4 * CHUNK
CHUNKS_PER_STEP = 4
IN_PROJ_TM = 8 * CHUNK
IN_PROJ_TN = 512
XBC_BLOCK = 1024
XBC_BLOCKS = D_XBC // XBC_BLOCK
assert D_INNER % XBC_BLOCK == 0 and D_XBC % XBC_BLOCK == 0
FFN_TN = 256


def _rms(x):
    return x * lax.rsqrt(jnp.mean(x * x, axis=-1, keepdims=True) + RMS_EPS)


def _silu(x):
    return x * jax.nn.sigmoid(x)


def _dot(a, b):
    return jnp.dot(a, b, preferred_element_type=F32)


def _dot_nt(a, b):
    return lax.dot_general(a, b, (((1,), (1,)), ((), ())), preferred_element_type=F32)


def _dot_tn(a, b):
    return lax.dot_general(a, b, (((0,), (0,)), ((), ())), preferred_element_type=F32)


def _resident(shape):
    return pl.BlockSpec(shape, lambda *_: (0,) * len(shape), pipeline_mode=pl.Buffered(1))


def _params(n_axes):
    return pltpu.CompilerParams(dimension_semantics=("arbitrary",) * n_axes,
                                vmem_limit_bytes=VMEM_LIMIT)


def _row_tile(head):
    return CHUNK if head else BODY_TM


def _chunks_per_step(nc):
    return CHUNKS_PER_STEP if nc % CHUNKS_PER_STEP == 0 else 1


def _causal_conv(u, first, init_ref, ext_ref, carry_ref, tail_ref, cw_ref, cb_ref, col, taps):
    tm, tn = u.shape
    cols = slice(col, col + tn)
    ext_ref[0:HALO, :] = jnp.where(first, init_ref[:, cols], carry_ref[:, cols])
    ext_ref[HALO:HALO + tm, :] = u
    carry_ref[:, cols] = u[tm - HALO:tm, :]
    if tail_ref is not None:
        tail_ref[:, cols] = u[tm - HALO:tm, :]
    out = cb_ref[:, cols] + cw_ref[taps - 1:taps, cols] * u
    for k in range(taps - 1):
        s = taps - 1 - k
        out = out + cw_ref[k:k + 1, cols] * ext_ref[HALO - s:HALO - s + tm, :]
    return out


def _normed_input(x, gain_ref, head):
    xn = _rms(x) * gain_ref[...]
    if head:
        row = lax.broadcasted_iota(jnp.int32, (x.shape[0], 1), 0)
        xn = jnp.where(row >= PAD, xn, 0.0)
    return xn.astype(BF16)


def _in_proj_kernel(*refs, tiles_per_seq, tm, tn, head):
    x_ref, g_ref, wz_ref = refs[:3]
    wx_refs = refs[3:3 + XBC_BLOCKS]
    wdt_ref, cw_ref, cb_ref, dtb_ref, init_ref, zs_ref, xbc_ref, dt_ref = refs[3 + XBC_BLOCKS:11 + XBC_BLOCKS]
    tail_ref = refs[11 + XBC_BLOCKS] if head else None
    ext_ref, carry_ref = refs[-2:]
    first = (pl.program_id(0) % tiles_per_seq) == 0
    xn = _normed_input(x_ref[...], g_ref, head)
    valid = lax.broadcasted_iota(jnp.int32, (tm, 1), 0) >= PAD
    for n0 in range(0, D_INNER, tn):
        zs_ref[:, n0:n0 + tn] = _silu(_dot(xn, wz_ref[:, n0:n0 + tn])).astype(BF16)
    for n0 in range(0, D_XBC, tn):
        w_ref, w0 = wx_refs[n0 // XBC_BLOCK], n0 % XBC_BLOCK
        u = _dot(xn, w_ref[:, w0:w0 + tn])
        y = _silu(_causal_conv(u, first, init_ref, ext_ref, carry_ref, tail_ref, cw_ref, cb_ref, n0, SSM_CONV))
        if head and n0 >= D_INNER:
            y = jnp.where(valid, y, 0.0)
        xbc_ref[:, n0:n0 + tn] = y.astype(BF16)
    dt_raw = _dot(xn, wdt_ref[...]) + dtb_ref[...]
    dt = jnp.maximum(dt_raw, 0.0) + jnp.log1p(jnp.exp(-jnp.abs(dt_raw)))
    dt_ref[...] = jnp.where(valid, dt, 0.0) if head else dt


def _in_proj(h, gain, w_in, layer, wdt, conv_w, conv_b, dt_bias, conv_init, seq_rows, head):
    rows = h.shape[0]
    tm, tn = (CHUNK if head else IN_PROJ_TM), IN_PROJ_TN
    assert seq_rows % tm == 0 and D_INNER % tn == 0 and XBC_BLOCK % tn == 0
    row_spec = lambda n: pl.BlockSpec((tm, n), lambda i: (i, 0))
    w_block = lambda width, idx: pl.BlockSpec((None, D_MODEL, width), lambda i: (layer, 0, idx),
                                              pipeline_mode=pl.Buffered(1))
    out_specs = [row_spec(D_INNER), row_spec(D_XBC), row_spec(LANES)]
    out_shape = [jax.ShapeDtypeStruct((rows, D_INNER), BF16), jax.ShapeDtypeStruct((rows, D_XBC), BF16),
                 jax.ShapeDtypeStruct((rows, LANES), F32)]
    if head:
        out_specs.append(_resident((HALO, D_XBC)))
        out_shape.append(jax.ShapeDtypeStruct((HALO, D_XBC), F32))
    return pl.pallas_call(
        functools.partial(_in_proj_kernel, tiles_per_seq=seq_rows // tm, tm=tm, tn=tn, head=head),
        grid=(rows // tm,),
        in_specs=[row_spec(D_MODEL), _resident((1, D_MODEL)), w_block(D_INNER, 0)]
                 + [w_block(XBC_BLOCK, D_INNER // XBC_BLOCK + k) for k in range(XBC_BLOCKS)]
                 + [_resident(wdt.shape), _resident(conv_w.shape), _resident(conv_b.shape),
                    _resident(dt_bias.shape), _resident(conv_init.shape)],
        out_specs=out_specs,
        out_shape=out_shape,
        scratch_shapes=[pltpu.VMEM((HALO + tm, tn), F32), pltpu.VMEM((HALO, D_XBC), F32)],
        compiler_params=_params(1),
        name="mamba_in_proj",
    )(h, gain, w_in, *([w_in] * XBC_BLOCKS), wdt, conv_w, conv_b, dt_bias, conv_init)


def _split3(x):
    hi = x.astype(BF16).astype(F32)
    r = x - hi
    mid = r.astype(BF16).astype(F32)
    return hi, mid, r - mid


def _pack3(x, lane, fill):
    hi, mid, lo = _split3(x)
    packed = jnp.where(lane < REP, hi, jnp.where(lane < 2 * REP, mid, jnp.where(lane < 3 * REP, lo, fill)))
    return packed.astype(BF16)


def _ssd_kernel(*refs, head, cps):
    (zs_ref, xbc_ref, dt_ref, alog_ref, dskip_ref, gn_ref, tril3_ref, e3_ref, selc_ref, init_ref, o_ref) = refs[:11]
    final_ref = refs[11] if head else None
    state_ref, yg_ref = refs[-2:]

    @pl.when(pl.program_id(1) == 0)
    def _():
        state_ref[...] = init_ref[...]

    for sub in range(cps):
        rows = pl.ds(sub * CHUNK, CHUNK)
        _ssd_chunk(zs_ref.at[rows], xbc_ref.at[rows], dt_ref.at[rows], alog_ref, dskip_ref, gn_ref, tril3_ref,
                   e3_ref, selc_ref, o_ref.at[rows], state_ref, yg_ref.at[rows])
    if head:
        final_ref[...] = state_ref[...]


def _ssd_chunk(zs_ref, xbc_ref, dt_ref, alog_ref, dskip_ref, gn_ref, tril3_ref, e3_ref, selc_ref, o_ref,
               state_ref, yg_ref):
    lane = lax.broadcasted_iota(jnp.int32, (CHUNK, LANES), 1)
    li = lax.broadcasted_iota(jnp.int32, (CHUNK, CHUNK), 0)
    cbias = jnp.where(li >= lane, 0.0, NEG_INF)
    cbias = jnp.concatenate([cbias] * HEADS_PER_GROUP, axis=1)

    dt = dt_ref[...]
    a = dt * (-jnp.exp(alog_ref[...]))
    a3 = jnp.concatenate([t.astype(BF16) for t in _split3(a)], axis=0)
    acs = _dot(tril3_ref[...], a3)
    b = jnp.maximum(jnp.log(dt), NEG_INF) - acs
    expa = jnp.exp(acs)
    wst = jnp.exp(acs[CHUNK - 1:CHUNK, :] + b)
    spread = _dot(jnp.concatenate([_pack3(expa, lane, 0.0), _pack3(wst, lane, 0.0)], axis=0), e3_ref[...])
    e_exp = spread[:CHUNK]
    w_exp = spread[CHUNK:]
    u3 = _pack3(acs, lane, 1.0)
    bt_terms = _split3(b.T)
    sub = lax.broadcasted_iota(jnp.int32, (HEADS_PER_GROUP, HEADS_PER_GROUP * CHUNK), 0)
    blk = lax.broadcasted_iota(jnp.int32, (HEADS_PER_GROUP, HEADS_PER_GROUP * CHUNK), 1) // CHUNK
    on_diag = sub == blk

    xs_f = xbc_ref[:, :D_INNER].astype(F32)
    xw_b = (xs_f * w_exp).astype(BF16)
    lane_lo = lane < SSM_HEAD_DIM
    ssq = jnp.zeros((CHUNK, 1), F32)
    gw = HEADS_PER_GROUP * SSM_HEAD_DIM
    for g in range(SSM_GROUPS):
        gc = slice(g * gw, (g + 1) * gw)
        bg_b = xbc_ref[:, D_INNER + g * D_STATE:D_INNER + (g + 1) * D_STATE]
        cg_b = xbc_ref[:, D_INNER + D_BC + g * D_STATE:D_INNER + D_BC + (g + 1) * D_STATE]
        cb_mat = _dot_nt(cg_b, bg_b)
        st = state_ref[:, gc]
        y_off = _dot(cg_b, st.astype(BF16))
        state_ref[:, gc] = st * e_exp[CHUNK - 1:CHUNK, gc] + _dot_tn(bg_b, xw_b[:, gc])
        dyn = [jnp.where(on_diag, jnp.concatenate([t[g * HEADS_PER_GROUP:(g + 1) * HEADS_PER_GROUP, :]]
                                                  * HEADS_PER_GROUP, axis=1), 0.0) for t in bt_terms]
        dyn.append(jnp.zeros_like(dyn[0]))
        v3 = jnp.concatenate([selc_ref[g], jnp.concatenate(dyn, axis=0).astype(BF16)], axis=0)
        seg = _dot(u3, v3)
        m = (jnp.exp(seg + cbias) * jnp.concatenate([cb_mat] * HEADS_PER_GROUP, axis=1)).astype(BF16)
        y_parts = []
        for pr in range(HEADS_PER_GROUP // 2):
            c0 = g * gw + pr * LANES
            xp = xbc_ref[:, c0:c0 + LANES]
            zero = jnp.zeros_like(xp)
            rhs = jnp.concatenate([jnp.where(lane_lo, xp, zero), jnp.where(lane_lo, zero, xp)], axis=0)
            y_parts.append(_dot(m[:, pr * 2 * CHUNK:(pr + 1) * 2 * CHUNK], rhs))
        y = jnp.concatenate(y_parts, axis=1) + y_off * e_exp[:, gc] + dskip_ref[:, gc] * xs_f[:, gc]
        y = y * zs_ref[:, gc].astype(F32)
        ssq = ssq + jnp.sum(y * y, axis=-1, keepdims=True)
        yg_ref[:, gc] = y
    inv = lax.rsqrt(ssq * (1.0 / D_INNER) + RMS_EPS)
    o_ref[...] = (yg_ref[...] * inv * gn_ref[...]).astype(BF16)


def _ssd_constants():
    tril3 = np.tile(np.tril(np.ones((CHUNK, CHUNK), np.float32)), (1, 3))
    e3 = np.zeros((LANES, D_INNER), np.float32)
    selc = np.zeros((SSM_GROUPS, 3 * REP, HEADS_PER_GROUP * CHUNK), np.float32)
    for t in range(3):
        for h in range(SSM_HEADS):
            e3[REP * t + h, h * SSM_HEAD_DIM:(h + 1) * SSM_HEAD_DIM] = 1.0
            g, k = divmod(h, HEADS_PER_GROUP)
            selc[g, REP * t + h, k * CHUNK:(k + 1) * CHUNK] = 1.0
    return jnp.asarray(tril3, BF16), jnp.asarray(e3, BF16), jnp.asarray(selc, BF16)


def _ssd(zs, xbc, dt, a_log, d_skip, gate_norm, state_init, bsz, nc, head):
    rows = zs.shape[0]
    tril3, e3, selc = _ssd_constants()
    cps = _chunks_per_step(nc)
    steps = nc // cps
    row_spec = lambda n: pl.BlockSpec((cps * CHUNK, n), lambda b, c: (b * steps + c, 0))
    out_specs = [row_spec(D_INNER)]
    out_shape = [jax.ShapeDtypeStruct((rows, D_INNER), BF16)]
    if head:
        out_specs.append(_resident((D_STATE, D_INNER)))
        out_shape.append(jax.ShapeDtypeStruct((D_STATE, D_INNER), F32))
    return pl.pallas_call(
        functools.partial(_ssd_kernel, head=head, cps=cps),
        grid=(bsz, steps),
        in_specs=[row_spec(D_INNER), row_spec(D_XBC), row_spec(LANES),
                  _resident(a_log.shape), _resident(d_skip.shape), _resident(gate_norm.shape),
                  _resident(tril3.shape), _resident(e3.shape), _resident(selc.shape),
                  _resident(state_init.shape)],
        out_specs=out_specs,
        out_shape=out_shape,
        scratch_shapes=[pltpu.VMEM((D_STATE, D_INNER), F32),
                        pltpu.VMEM((cps * CHUNK, D_INNER), F32)],
        compiler_params=_params(2),
        name="mamba_ssd",
    )(zs, xbc, dt, a_log, d_skip, gate_norm, tril3, e3, selc, state_init)


def _ffn_kernel(*refs, tiles_per_seq, tm, tn, head, mix_transposed):
    (res_ref, mix_ref, wmix_ref, gmix_ref, gpre_ref, wg_ref, wv_ref, cw_ref, cb_ref, wd_ref, gpost_ref,
     init_ref, o_ref) = refs[:13]
    tail_ref = refs[13] if head else None
    ext_ref, carry_ref, act_ref = refs[-3:]
    first = (pl.program_id(0) % tiles_per_seq) == 0
    mix = _dot_tn(mix_ref[...], wmix_ref[...]) if mix_transposed else _dot(mix_ref[...], wmix_ref[...])
    x = res_ref[...] + _rms(mix) * gmix_ref[...]
    xn = _normed_input(x, gpre_ref, head)

    def up_conv(w_ref, col):
        u = _dot(xn, w_ref[:, col % D_FF:col % D_FF + tn])
        return _causal_conv(u, first, init_ref, ext_ref, carry_ref, tail_ref, cw_ref, cb_ref, col, FFN_CONV)

    for j in range(D_FF // tn):
        gate = up_conv(wg_ref, j * tn)
        val = up_conv(wv_ref, D_FF + j * tn)
        act_ref[:, j * tn:(j + 1) * tn] = (_silu(gate) * val).astype(BF16)
    y = _dot(act_ref[...], wd_ref[...])
    o_ref[...] = x + _rms(y) * gpost_ref[...]


def _ffn(res, mix, wmix, gmix, gpre, w_up, w_down, layer, conv_w, conv_b, gpost, conv_init, seq_rows, head,
         mix_transposed):
    rows = res.shape[0]
    k = wmix.shape[0]
    tm, tn = _row_tile(head), FFN_TN
    assert seq_rows % tm == 0 and D_FF % tn == 0
    row_spec = pl.BlockSpec((tm, D_MODEL), lambda i: (i, 0))
    mix_spec = pl.BlockSpec((k, tm), lambda i: (0, i)) if mix_transposed else pl.BlockSpec((tm, k), lambda i: (i, 0))
    up_half = lambda half: pl.BlockSpec((None, D_MODEL, D_FF), lambda i: (layer, 0, half), pipeline_mode=pl.Buffered(1))
    down_spec = pl.BlockSpec((None, D_FF, D_MODEL), lambda i: (layer, 0, 0), pipeline_mode=pl.Buffered(1))
    out_specs = [row_spec]
    out_shape = [jax.ShapeDtypeStruct((rows, D_MODEL), F32)]
    if head:
        out_specs.append(_resident((HALO, 2 * D_FF)))
        out_shape.append(jax.ShapeDtypeStruct((HALO, 2 * D_FF), F32))
    return pl.pallas_call(
        functools.partial(_ffn_kernel, tiles_per_seq=seq_rows // tm, tm=tm, tn=tn, head=head,
                          mix_transposed=mix_transposed),
        grid=(rows // tm,),
        in_specs=[row_spec, mix_spec, _resident(wmix.shape), _resident(gmix.shape), _resident(gpre.shape),
                  up_half(0), up_half(1), _resident(conv_w.shape), _resident(conv_b.shape), down_spec,
                  _resident(gpost.shape), _resident(conv_init.shape)],
        out_specs=out_specs,
        out_shape=out_shape,
        scratch_shapes=[pltpu.VMEM((HALO + tm, tn), F32),
                        pltpu.VMEM((HALO, 2 * D_FF), F32),
                        pltpu.VMEM((tm, D_FF), BF16)],
        compiler_params=_params(1),
        name="mix_proj_conv_ffn",
    )(res, mix, wmix, gmix, gpre, w_up, w_up, conv_w, conv_b, w_down, gpost, conv_init)


def _qkv_proj_kernel(x_ref, gq_ref, gkv_ref, wq_ref, wk_ref, wv_ref, qt_ref, k_ref, vt_ref):
    xr = _rms(x_ref[...])
    xq = (xr * gq_ref[...]).astype(BF16)
    xkv = (xr * gkv_ref[...]).astype(BF16)
    scale = 1.0 / math.sqrt(ATTN_HEAD_DIM)
    qt_ref[...] = (_dot(xq, wq_ref[...]) * scale).T.astype(BF16)
    k_ref[...] = _dot(xkv, wk_ref[...]).astype(BF16)
    vt_ref[...] = _dot(xkv, wv_ref[...]).T.astype(BF16)


def _qkv_proj(h, gq, gkv, wq, wkv, head):
    rows = h.shape[0]
    tm = _row_tile(head)
    return pl.pallas_call(
        _qkv_proj_kernel,
        grid=(rows // tm,),
        in_specs=[pl.BlockSpec((tm, D_MODEL), lambda i: (i, 0)), _resident(gq.shape), _resident(gkv.shape),
                  _resident(wq.shape), pl.BlockSpec((D_MODEL, D_KV), lambda i: (0, 0), pipeline_mode=pl.Buffered(1)),
                  pl.BlockSpec((D_MODEL, D_KV), lambda i: (0, 1), pipeline_mode=pl.Buffered(1))],
        out_specs=[pl.BlockSpec((D_ATTN, tm), lambda i: (0, i)), pl.BlockSpec((tm, D_KV), lambda i: (i, 0)),
                   pl.BlockSpec((D_KV, tm), lambda i: (0, i))],
        out_shape=[jax.ShapeDtypeStruct((D_ATTN, rows), BF16), jax.ShapeDtypeStruct((rows, D_KV), BF16),
                   jax.ShapeDtypeStruct((D_KV, rows), BF16)],
        compiler_params=_params(1),
        name="qkv_proj",
    )(h, gq, gkv, wq, wkv, wkv)


def _attn_kernel(sink_ref, qt_ref, kc_ref, kp_ref, km_ref, vc_ref, vp_ref, vm_ref, ot_ref, *, head, cps):
    for sub in range(cps):
        rows = pl.ds(sub * CHUNK, CHUNK)
        if sub == 0:
            k_prev, v_prev, has_prev = kp_ref, vp_ref, pl.program_id(1) >= 1
        else:
            before = pl.ds((sub - 1) * CHUNK, CHUNK)
            k_prev, v_prev, has_prev = kc_ref.at[before], vc_ref.at[:, before], True
        _attn_chunk(sink_ref, qt_ref.at[:, rows], kc_ref.at[rows], k_prev, km_ref, vc_ref.at[:, rows], v_prev, vm_ref,
                    ot_ref.at[:, rows], has_prev, head)


def _attn_chunk(sink_ref, qt_ref, kc_ref, kp_ref, km_ref, vc_ref, vp_ref, vm_ref, ot_ref, has_prev, head):
    nkeys = N_META + 2 * CHUNK
    ki = lax.broadcasted_iota(jnp.int32, (nkeys, CHUNK), 0)
    qi = lax.broadcasted_iota(jnp.int32, (nkeys, CHUNK), 1)
    kprev = ki - N_META
    kcur = ki - (N_META + CHUNK)
    in_cur = jnp.logical_and(kcur >= 0, kcur <= qi)
    if head:
        visible = jnp.logical_and(in_cur, kcur >= PAD)
    else:
        in_prev = jnp.logical_and(jnp.logical_and(kprev >= 0, kprev < CHUNK), jnp.logical_and(kprev > qi, has_prev))
        visible = jnp.logical_or(jnp.logical_or(ki < N_META, in_prev), in_cur)
    bias = jnp.concatenate([jnp.where(visible, 0.0, NEG_INF)] * N_Q_HEADS, axis=1)
    k_all = jnp.concatenate([km_ref[PAD:CHUNK, :], kp_ref[...], kc_ref[...]], axis=0)
    qw = Q_PER_KV * CHUNK
    zeros = jnp.zeros((ATTN_HEAD_DIM, qw), BF16)

    def q_heads(kh):
        return jnp.concatenate([qt_ref[hq * ATTN_HEAD_DIM:(hq + 1) * ATTN_HEAD_DIM, :]
                                for hq in range(kh * Q_PER_KV, (kh + 1) * Q_PER_KV)], axis=1)

    halves = []
    for pair in range(N_KV_HEADS // 2):
        qbd = jnp.concatenate([jnp.concatenate([q_heads(2 * pair), zeros], axis=1),
                               jnp.concatenate([zeros, q_heads(2 * pair + 1)], axis=1)], axis=0)
        halves.append(_dot(k_all[:, pair * LANES:(pair + 1) * LANES], qbd))
    s = jnp.concatenate(halves, axis=1) + bias
    sink = jnp.concatenate([jnp.full((1, CHUNK), sink_ref[hq], F32) for hq in range(N_Q_HEADS)], axis=1)
    m = jnp.maximum(jnp.max(s, axis=0, keepdims=True), sink)
    p = jnp.exp(s - m)
    inv_den = 1.0 / (jnp.sum(p, axis=0, keepdims=True) + jnp.exp(sink - m))
    pb = p.astype(BF16)
    pad_rows = jnp.zeros((PAD, qw), BF16)
    for kh in range(N_KV_HEADS):
        rows = slice(kh * ATTN_HEAD_DIM, (kh + 1) * ATTN_HEAD_DIM)
        vt = jnp.concatenate([vm_ref[rows, :], vp_ref[rows, :], vc_ref[rows, :]], axis=1)
        p_all = jnp.concatenate([pad_rows, pb[:, kh * qw:(kh + 1) * qw]], axis=0)
        o = _dot(vt, p_all) * inv_den[:, kh * qw:(kh + 1) * qw]
        for g in range(Q_PER_KV):
            hq = kh * Q_PER_KV + g
            ot_ref[hq * ATTN_HEAD_DIM:(hq + 1) * ATTN_HEAD_DIM, :] = o[:, g * CHUNK:(g + 1) * CHUNK].astype(BF16)


def _attn(qt, k, vt, k_head, vt_head, sinks, bsz, nc, head):
    rows = qt.shape[1]
    cps = _chunks_per_step(nc)
    steps = nc // cps
    cur = lambda b, c: b * steps + c
    prev = lambda b, c: b * nc + jnp.maximum(c * cps - 1, 0)
    return pl.pallas_call(
        functools.partial(_attn_kernel, head=head, cps=cps),
        grid=(bsz, steps),
        in_specs=[pl.BlockSpec(memory_space=pltpu.SMEM),
                  pl.BlockSpec((D_ATTN, cps * CHUNK), lambda b, c: (0, cur(b, c))),
                  pl.BlockSpec((cps * CHUNK, D_KV), lambda b, c: (cur(b, c), 0)),
                  pl.BlockSpec((CHUNK, D_KV), lambda b, c: (prev(b, c), 0)), _resident(k_head.shape),
                  pl.BlockSpec((D_KV, cps * CHUNK), lambda b, c: (0, cur(b, c))),
                  pl.BlockSpec((D_KV, CHUNK), lambda b, c: (0, prev(b, c))), _resident(vt_head.shape)],
        out_specs=pl.BlockSpec((D_ATTN, cps * CHUNK), lambda b, c: (0, cur(b, c))),
        out_shape=jax.ShapeDtypeStruct((D_ATTN, rows), BF16),
        compiler_params=_params(2),
        name="swa_sink_attention",
    )(sinks, qt, k, k, k_head, vt, vt, vt_head)


def kernel(x, meta_tokens, a_norm_pre, a_w_in, a_conv_w, a_conv_b, a_dt_bias, a_a_log, a_d_skip, a_gate_norm, a_w_out, a_norm_post, kv_norm, w_kv, b_norm_pre, b_w_q, b_sinks, b_w_o, b_norm_post, f_norm_pre, f_w_up, f_conv_w, f_conv_b, f_w_down, f_norm_post):
    bsz, seq, _ = x.shape
    assert seq % BODY_TM == 0
    depth = f_norm_pre.shape[0]
    n_a = a_norm_pre.shape[0]

    def row(v):
        return v.astype(F32).reshape(1, -1)

    def rep3(v):
        return jnp.pad(jnp.concatenate([v] * 3, axis=1), ((0, 0), (0, LANES - 3 * REP)))

    passes = {True: (1, CHUNK, 1), False: (bsz, seq, seq // CHUNK)}
    hs = {True: jnp.concatenate([jnp.zeros((PAD, D_MODEL), F32), meta_tokens.astype(F32)], axis=0),
          False: x.astype(F32).reshape(bsz * seq, D_MODEL)}

    w_in_b, w_up_b, w_down_b = a_w_in.astype(BF16), f_w_up.astype(BF16), f_w_down.astype(BF16)
    kv = {}
    for i in range(depth):
        mixes = {}
        if i < n_a:
            j = i
            wdt = rep3(a_w_in[j][:, D_INNER + D_XBC:]).astype(BF16)
            w_mix, g_mix, mix_transposed = a_w_out[j].astype(BF16), row(a_norm_post[j]), False
            conv_init = jnp.zeros((HALO, D_XBC), F32)
            state_init = jnp.zeros((D_STATE, D_INNER), F32)
            for head in (True, False):
                nseq, seq_rows, nc = passes[head]
                outs = _in_proj(hs[head], row(a_norm_pre[j]), w_in_b, j, wdt, a_conv_w[j].astype(F32),
                                row(a_conv_b[j]), rep3(row(a_dt_bias[j])), conv_init, seq_rows, head)
                ys = _ssd(outs[0], outs[1], outs[2], rep3(row(a_a_log[j])),
                          row(jnp.repeat(a_d_skip[j], SSM_HEAD_DIM)), row(a_gate_norm[j]), state_init, nseq, nc, head)
                mixes[head] = ys[0]
                if head:
                    conv_init, state_init = outs[3], ys[1]
        else:
            j = i - n_a
            wq, wkv = b_w_q[j].astype(BF16), w_kv.astype(BF16)
            w_mix, g_mix, mix_transposed = b_w_o[j].astype(BF16), row(b_norm_post[j]), True
            for head in (True, False):
                nseq, seq_rows, nc = passes[head]
                qt, k_new, vt_new = _qkv_proj(hs[head], row(b_norm_pre[j]), row(kv_norm), wq, wkv, head)
                if j == 0:
                    kv[head] = (k_new, vt_new)
                mixes[head] = _attn(qt, kv[head][0], kv[head][1], kv[True][0], kv[True][1],
                                    b_sinks[j].astype(F32), nseq, nc, head)
        conv_init = jnp.zeros((HALO, 2 * D_FF), F32)
        for head in (True, False):
            nseq, seq_rows, nc = passes[head]
            outs = _ffn(hs[head], mixes[head], w_mix, g_mix, row(f_norm_pre[i]), w_up_b, w_down_b, i,
                        f_conv_w[i].astype(F32), row(f_conv_b[i]), row(f_norm_post[i]), conv_init, seq_rows, head,
                        mix_transposed)
            hs[head] = outs[0]
            if head:
                conv_init = outs[1]
    return hs[False].reshape(bsz, seq, D_MODEL)
```

```python
import functools
import math

import jax
import jax.numpy as jnp
import numpy as np
from jax import lax
from jax.experimental import pallas as pl
from jax.experimental.pallas import tpu as pltpu

F32 = jnp.float32
BF16 = jnp.bfloat16

D_MODEL = 1024
N_META = 16
CHUNK = 128
PAD = CHUNK - N_META

D_INNER = 2048
SSM_HEAD_DIM = 64
SSM_HEADS = 32
SSM_GROUPS = 4
HEADS_PER_GROUP = SSM_HEADS // SSM_GROUPS
D_STATE = 128
SSM_CONV = 4
D_BC = SSM_GROUPS * D_STATE
D_XBC = D_INNER + 2 * D_BC

ATTN_HEAD_DIM = 64
N_Q_HEADS = 16
N_KV_HEADS = 4
Q_PER_KV = N_Q_HEADS // N_KV_HEADS
D_ATTN = N_Q_HEADS * ATTN_HEAD_DIM
D_KV = N_KV_HEADS * ATTN_HEAD_DIM

D_FF = 2816
FFN_CONV = 3

RMS_EPS = 1e-6
NEG_INF = -1e30

LANES = 128
HALO = 8
REP = SSM_HEADS
assert 3 * REP <= LANES
VMEM_LIMIT = 56 * 1024 * 1024
BODY_TM = 4 * CHUNK
CHUNKS_PER_STEP = 4
IN_PROJ_TM = 8 * CHUNK
IN_PROJ_TN = 512
XBC_BLOCK = 1024
XBC_BLOCKS = D_XBC // XBC_BLOCK
assert D_INNER % XBC_BLOCK == 0 and D_XBC % XBC_BLOCK == 0
FFN_TN = 256


def _rms(x):
    return x * lax.rsqrt(jnp.mean(x * x, axis=-1, keepdims=True) + RMS_EPS)


def _silu(x):
    return x * jax.nn.sigmoid(x)


def _dot(a, b):
    return jnp.dot(a, b, preferred_element_type=F32)


def _dot_nt(a, b):
    return lax.dot_general(a, b, (((1,), (1,)), ((), ())), preferred_element_type=F32)


def _dot_tn(a, b):
    return lax.dot_general(a, b, (((0,), (0,)), ((), ())), preferred_element_type=F32)


def _resident(shape):
    return pl.BlockSpec(shape, lambda *_: (0,) * len(shape), pipeline_mode=pl.Buffered(1))


def _params(n_axes):
    return pltpu.CompilerParams(dimension_semantics=("arbitrary",) * n_axes,
                                vmem_limit_bytes=VMEM_LIMIT)


def _row_tile(head):
    return CHUNK if head else BODY_TM


def _chunks_per_step(nc):
    return CHUNKS_PER_STEP if nc % CHUNKS_PER_STEP == 0 else 1


def _conv_stage(u, first, init_ref, buf, carry_ref, tail_ref, col):
    tm, tn = u.shape
    cols = slice(col, col + tn)
    buf[0:HALO, :] = jnp.where(first, init_ref[:, cols], carry_ref[:, cols])
    buf[HALO:HALO + tm, :] = u
    carry_ref[:, cols] = u[tm - HALO:tm, :]
    if tail_ref is not None:
        tail_ref[:, cols] = u[tm - HALO:tm, :]


def _conv_finish(buf, cw_ref, cb_ref, col, taps, u=None):
    tm, tn = buf.shape[0] - HALO, buf.shape[1]
    cols = slice(col, col + tn)
    out = cb_ref[:, cols] + cw_ref[taps - 1:taps, cols] * (buf[HALO:HALO + tm, :] if u is None else u)
    for k in range(taps - 1):
        s = taps - 1 - k
        out = out + cw_ref[k:k + 1, cols] * buf[HALO - s:HALO - s + tm, :]
    return out


def _normed_input(x, gain_ref, head):
    xn = _rms(x) * gain_ref[...]
    if head:
        row = lax.broadcasted_iota(jnp.int32, (x.shape[0], 1), 0)
        xn = jnp.where(row >= PAD, xn, 0.0)
    return xn.astype(BF16)


def _in_proj_kernel(*refs, tiles_per_seq, tm, tn, head):
    x_ref, g_ref, wz_ref = refs[:3]
    wx_refs = refs[3:3 + XBC_BLOCKS]
    wdt_ref, cw_ref, cb_ref, dtb_ref, init_ref, zs_ref, xbc_ref, dt_ref = refs[3 + XBC_BLOCKS:11 + XBC_BLOCKS]
    tail_ref = refs[11 + XBC_BLOCKS] if head else None
    ext_ref, carry_ref = refs[-2:]
    first = (pl.program_id(0) % tiles_per_seq) == 0
    xn = _normed_input(x_ref[...], g_ref, head)
    valid = lax.broadcasted_iota(jnp.int32, (tm, 1), 0) >= PAD
    def z_block(n0):
        zs_ref[:, n0:n0 + tn] = _silu(_dot(xn, wz_ref[:, n0:n0 + tn])).astype(BF16)

    def stage(n0):
        w_ref, w0 = wx_refs[n0 // XBC_BLOCK], n0 % XBC_BLOCK
        _conv_stage(_dot(xn, w_ref[:, w0:w0 + tn]), first, init_ref, ext_ref.at[(n0 // tn) % 2], carry_ref,
                    tail_ref, n0)

    def finish(n0):
        y = _silu(_conv_finish(ext_ref.at[(n0 // tn) % 2], cw_ref, cb_ref, n0, SSM_CONV))
        if head and n0 >= D_INNER:
            y = jnp.where(valid, y, 0.0)
        xbc_ref[:, n0:n0 + tn] = y.astype(BF16)

    z_cols, x_cols = list(range(0, D_INNER, tn)), list(range(0, D_XBC, tn))
    stage(x_cols[0])
    for i, n0 in enumerate(x_cols):
        if i + 1 < len(x_cols):
            stage(x_cols[i + 1])
        finish(n0)
        if z_cols:
            z_block(z_cols.pop(0))
    for n0 in z_cols:
        z_block(n0)
    dt_raw = _dot(xn, wdt_ref[...]) + dtb_ref[...]
    dt = jnp.maximum(dt_raw, 0.0) + jnp.log1p(jnp.exp(-jnp.abs(dt_raw)))
    dt_ref[...] = jnp.where(valid, dt, 0.0) if head else dt


def _in_proj(h, gain, w_in, layer, wdt, conv_w, conv_b, dt_bias, conv_init, seq_rows, head):
    rows = h.shape[0]
    tm, tn = (CHUNK if head else IN_PROJ_TM), IN_PROJ_TN
    assert seq_rows % tm == 0 and D_INNER % tn == 0 and XBC_BLOCK % tn == 0
    row_spec = lambda n: pl.BlockSpec((tm, n), lambda i: (i, 0))
    w_block = lambda width, idx: pl.BlockSpec((None, D_MODEL, width), lambda i: (layer, 0, idx),
                                              pipeline_mode=pl.Buffered(1))
    out_specs = [row_spec(D_INNER), row_spec(D_XBC), row_spec(LANES)]
    out_shape = [jax.ShapeDtypeStruct((rows, D_INNER), BF16), jax.ShapeDtypeStruct((rows, D_XBC), BF16),
                 jax.ShapeDtypeStruct((rows, LANES), F32)]
    if head:
        out_specs.append(_resident((HALO, D_XBC)))
        out_shape.append(jax.ShapeDtypeStruct((HALO, D_XBC), F32))
    return pl.pallas_call(
        functools.partial(_in_proj_kernel, tiles_per_seq=seq_rows // tm, tm=tm, tn=tn, head=head),
        grid=(rows // tm,),
        in_specs=[row_spec(D_MODEL), _resident((1, D_MODEL)), w_block(D_INNER, 0)]
                 + [w_block(XBC_BLOCK, D_INNER // XBC_BLOCK + k) for k in range(XBC_BLOCKS)]
                 + [_resident(wdt.shape), _resident(conv_w.shape), _resident(conv_b.shape),
                    _resident(dt_bias.shape), _resident(conv_init.shape)],
        out_specs=out_specs,
        out_shape=out_shape,
        scratch_shapes=[pltpu.VMEM((2, HALO + tm, tn), F32), pltpu.VMEM((HALO, D_XBC), F32)],
        compiler_params=_params(1),
        name="mamba_in_proj",
    )(h, gain, w_in, *([w_in] * XBC_BLOCKS), wdt, conv_w, conv_b, dt_bias, conv_init)


def _split3(x):
    hi = x.astype(BF16).astype(F32)
    r = x - hi
    mid = r.astype(BF16).astype(F32)
    return hi, mid, r - mid


def _pack3(x, lane, fill):
    hi, mid, lo = _split3(x)
    packed = jnp.where(lane < REP, hi, jnp.where(lane < 2 * REP, mid, jnp.where(lane < 3 * REP, lo, fill)))
    return packed.astype(BF16)


def _ssd_kernel(*refs, head, cps):
    (zs_ref, xbc_ref, dt_ref, alog_ref, dskip_ref, gn_ref, tril3_ref, e3_ref, selc_ref, init_ref, o_ref) = refs[:11]
    final_ref = refs[11] if head else None
    state_ref, yg_ref = refs[-2:]

    @pl.when(pl.program_id(1) == 0)
    def _():
        state_ref[...] = init_ref[...]

    for sub in range(cps):
        rows = pl.ds(sub * CHUNK, CHUNK)
        _ssd_chunk(zs_ref.at[rows], xbc_ref.at[rows], dt_ref.at[rows], alog_ref, dskip_ref, gn_ref, tril3_ref,
                   e3_ref, selc_ref, o_ref.at[rows], state_ref, yg_ref.at[rows])
    if head:
        final_ref[...] = state_ref[...]


def _ssd_chunk(zs_ref, xbc_ref, dt_ref, alog_ref, dskip_ref, gn_ref, tril3_ref, e3_ref, selc_ref, o_ref,
               state_ref, yg_ref):
    lane = lax.broadcasted_iota(jnp.int32, (CHUNK, LANES), 1)
    li = lax.broadcasted_iota(jnp.int32, (CHUNK, CHUNK), 0)
    cbias = jnp.where(li >= lane, 0.0, NEG_INF)
    cbias = jnp.concatenate([cbias] * HEADS_PER_GROUP, axis=1)

    dt = dt_ref[...]
    a = dt * (-jnp.exp(alog_ref[...]))
    a3 = jnp.concatenate([t.astype(BF16) for t in _split3(a)], axis=0)
    acs = _dot(tril3_ref[...], a3)
    b = jnp.maximum(jnp.log(dt), NEG_INF) - acs
    expa = jnp.exp(acs)
    wst = jnp.exp(acs[CHUNK - 1:CHUNK, :] + b)
    spread = _dot(jnp.concatenate([_pack3(expa, lane, 0.0), _pack3(wst, lane, 0.0)], axis=0), e3_ref[...])
    e_exp = spread[:CHUNK]
    w_exp = spread[CHUNK:]
    u3 = _pack3(acs, lane, 1.0)
    bt_terms = _split3(b.T)
    sub = lax.broadcasted_iota(jnp.int32, (HEADS_PER_GROUP, HEADS_PER_GROUP * CHUNK), 0)
    blk = lax.broadcasted_iota(jnp.int32, (HEADS_PER_GROUP, HEADS_PER_GROUP * CHUNK), 1) // CHUNK
    on_diag = sub == blk

    xs_f = xbc_ref[:, :D_INNER].astype(F32)
    xw_b = (xs_f * w_exp).astype(BF16)
    lane_lo = lane < SSM_HEAD_DIM
    ssq = jnp.zeros((CHUNK, 1), F32)
    gw = HEADS_PER_GROUP * SSM_HEAD_DIM
    for g in range(SSM_GROUPS):
        gc = slice(g * gw, (g + 1) * gw)
        bg_b = xbc_ref[:, D_INNER + g * D_STATE:D_INNER + (g + 1) * D_STATE]
        cg_b = xbc_ref[:, D_INNER + D_BC + g * D_STATE:D_INNER + D_BC + (g + 1) * D_STATE]
        cb_mat = _dot_nt(cg_b, bg_b)
        st = state_ref[:, gc]
        y_off = _dot(cg_b, st.astype(BF16))
        state_ref[:, gc] = st * e_exp[CHUNK - 1:CHUNK, gc] + _dot_tn(bg_b, xw_b[:, gc])
        dyn = [jnp.where(on_diag, jnp.concatenate([t[g * HEADS_PER_GROUP:(g + 1) * HEADS_PER_GROUP, :]]
                                                  * HEADS_PER_GROUP, axis=1), 0.0) for t in bt_terms]
        dyn.append(jnp.zeros_like(dyn[0]))
        v3 = jnp.concatenate([selc_ref[g], jnp.concatenate(dyn, axis=0).astype(BF16)], axis=0)
        seg = _dot(u3, v3)
        m = (jnp.exp(seg + cbias) * jnp.concatenate([cb_mat] * HEADS_PER_GROUP, axis=1)).astype(BF16)
        y_parts = []
        for pr in range(HEADS_PER_GROUP // 2):
            c0 = g * gw + pr * LANES
            xp = xbc_ref[:, c0:c0 + LANES]
            zero = jnp.zeros_like(xp)
            rhs = jnp.concatenate([jnp.where(lane_lo, xp, zero), jnp.where(lane_lo, zero, xp)], axis=0)
            y_parts.append(_dot(m[:, pr * 2 * CHUNK:(pr + 1) * 2 * CHUNK], rhs))
        y = jnp.concatenate(y_parts, axis=1) + y_off * e_exp[:, gc] + dskip_ref[:, gc] * xs_f[:, gc]
        y = y * zs_ref[:, gc].astype(F32)
        ssq = ssq + jnp.sum(y * y, axis=-1, keepdims=True)
        yg_ref[:, gc] = y
    inv = lax.rsqrt(ssq * (1.0 / D_INNER) + RMS_EPS)
    o_ref[...] = (yg_ref[...] * inv * gn_ref[...]).astype(BF16)


def _ssd_constants():
    tril3 = np.tile(np.tril(np.ones((CHUNK, CHUNK), np.float32)), (1, 3))
    e3 = np.zeros((LANES, D_INNER), np.float32)
    selc = np.zeros((SSM_GROUPS, 3 * REP, HEADS_PER_GROUP * CHUNK), np.float32)
    for t in range(3):
        for h in range(SSM_HEADS):
            e3[REP * t + h, h * SSM_HEAD_DIM:(h + 1) * SSM_HEAD_DIM] = 1.0
            g, k = divmod(h, HEADS_PER_GROUP)
            selc[g, REP * t + h, k * CHUNK:(k + 1) * CHUNK] = 1.0
    return jnp.asarray(tril3, BF16), jnp.asarray(e3, BF16), jnp.asarray(selc, BF16)


def _ssd(zs, xbc, dt, a_log, d_skip, gate_norm, state_init, bsz, nc, head):
    rows = zs.shape[0]
    tril3, e3, selc = _ssd_constants()
    cps = _chunks_per_step(nc)
    steps = nc // cps
    row_spec = lambda n: pl.BlockSpec((cps * CHUNK, n), lambda b, c: (b * steps + c, 0))
    out_specs = [row_spec(D_INNER)]
    out_shape = [jax.ShapeDtypeStruct((rows, D_INNER), BF16)]
    if head:
        out_specs.append(_resident((D_STATE, D_INNER)))
        out_shape.append(jax.ShapeDtypeStruct((D_STATE, D_INNER), F32))
    return pl.pallas_call(
        functools.partial(_ssd_kernel, head=head, cps=cps),
        grid=(bsz, steps),
        in_specs=[row_spec(D_INNER), row_spec(D_XBC), row_spec(LANES),
                  _resident(a_log.shape), _resident(d_skip.shape), _resident(gate_norm.shape),
                  _resident(tril3.shape), _resident(e3.shape), _resident(selc.shape),
                  _resident(state_init.shape)],
        out_specs=out_specs,
        out_shape=out_shape,
        scratch_shapes=[pltpu.VMEM((D_STATE, D_INNER), F32),
                        pltpu.VMEM((cps * CHUNK, D_INNER), F32)],
        compiler_params=_params(2),
        name="mamba_ssd",
    )(zs, xbc, dt, a_log, d_skip, gate_norm, tril3, e3, selc, state_init)


def _ffn_kernel(*refs, tiles_per_seq, tm, tn, head, mix_transposed):
    (res_ref, mix_ref, wmix_ref, gmix_ref, gpre_ref, wg_ref, wv_ref, cw_ref, cb_ref, wd_ref, gpost_ref,
     init_ref, o_ref) = refs[:13]
    tail_ref = refs[13] if head else None
    ext_ref, carry_ref, act_ref = refs[-3:]
    first = (pl.program_id(0) % tiles_per_seq) == 0
    mix = _dot_tn(mix_ref[...], wmix_ref[...]) if mix_transposed else _dot(mix_ref[...], wmix_ref[...])
    x = res_ref[...] + _rms(mix) * gmix_ref[...]
    xn = _normed_input(x, gpre_ref, head)

    def up_conv(w_ref, col):
        u = _dot(xn, w_ref[:, col % D_FF:col % D_FF + tn])
        _conv_stage(u, first, init_ref, ext_ref, carry_ref, tail_ref, col)
        return _conv_finish(ext_ref, cw_ref, cb_ref, col, FFN_CONV, u)

    for j in range(D_FF // tn):
        gate = up_conv(wg_ref, j * tn)
        val = up_conv(wv_ref, D_FF + j * tn)
        act_ref[:, j * tn:(j + 1) * tn] = (_silu(gate) * val).astype(BF16)
    y = _dot(act_ref[...], wd_ref[...])
    o_ref[...] = x + _rms(y) * gpost_ref[...]


def _ffn(res, mix, wmix, gmix, gpre, w_up, w_down, layer, conv_w, conv_b, gpost, conv_init, seq_rows, head,
         mix_transposed):
    rows = res.shape[0]
    k = wmix.shape[0]
    tm, tn = _row_tile(head), FFN_TN
    assert seq_rows % tm == 0 and D_FF % tn == 0
    row_spec = pl.BlockSpec((tm, D_MODEL), lambda i: (i, 0))
    mix_spec = pl.BlockSpec((k, tm), lambda i: (0, i)) if mix_transposed else pl.BlockSpec((tm, k), lambda i: (i, 0))
    up_half = lambda half: pl.BlockSpec((None, D_MODEL, D_FF), lambda i: (layer, 0, half), pipeline_mode=pl.Buffered(1))
    down_spec = pl.BlockSpec((None, D_FF, D_MODEL), lambda i: (layer, 0, 0), pipeline_mode=pl.Buffered(1))
    out_specs = [row_spec]
    out_shape = [jax.ShapeDtypeStruct((rows, D_MODEL), F32)]
    if head:
        out_specs.append(_resident((HALO, 2 * D_FF)))
        out_shape.append(jax.ShapeDtypeStruct((HALO, 2 * D_FF), F32))
    return pl.pallas_call(
        functools.partial(_ffn_kernel, tiles_per_seq=seq_rows // tm, tm=tm, tn=tn, head=head,
                          mix_transposed=mix_transposed),
        grid=(rows // tm,),
        in_specs=[row_spec, mix_spec, _resident(wmix.shape), _resident(gmix.shape), _resident(gpre.shape),
                  up_half(0), up_half(1), _resident(conv_w.shape), _resident(conv_b.shape), down_spec,
                  _resident(gpost.shape), _resident(conv_init.shape)],
        out_specs=out_specs,
        out_shape=out_shape,
        scratch_shapes=[pltpu.VMEM((HALO + tm, tn), F32),
                        pltpu.VMEM((HALO, 2 * D_FF), F32),
                        pltpu.VMEM((tm, D_FF), BF16)],
        compiler_params=_params(1),
        name="mix_proj_conv_ffn",
    )(res, mix, wmix, gmix, gpre, w_up, w_up, conv_w, conv_b, w_down, gpost, conv_init)


def _qkv_proj_kernel(x_ref, gq_ref, gkv_ref, wq_ref, wk_ref, wv_ref, qt_ref, k_ref, vt_ref):
    xr = _rms(x_ref[...])
    xq = (xr * gq_ref[...]).astype(BF16)
    xkv = (xr * gkv_ref[...]).astype(BF16)
    scale = 1.0 / math.sqrt(ATTN_HEAD_DIM)
    qt_ref[...] = (_dot(xq, wq_ref[...]) * scale).T.astype(BF16)
    k_ref[...] = _dot(xkv, wk_ref[...]).astype(BF16)
    vt_ref[...] = _dot(xkv, wv_ref[...]).T.astype(BF16)


def _qkv_proj(h, gq, gkv, wq, wkv, head):
    rows = h.shape[0]
    tm = _row_tile(head)
    return pl.pallas_call(
        _qkv_proj_kernel,
        grid=(rows // tm,),
        in_specs=[pl.BlockSpec((tm, D_MODEL), lambda i: (i, 0)), _resident(gq.shape), _resident(gkv.shape),
                  _resident(wq.shape), pl.BlockSpec((D_MODEL, D_KV), lambda i: (0, 0), pipeline_mode=pl.Buffered(1)),
                  pl.BlockSpec((D_MODEL, D_KV), lambda i: (0, 1), pipeline_mode=pl.Buffered(1))],
        out_specs=[pl.BlockSpec((D_ATTN, tm), lambda i: (0, i)), pl.BlockSpec((tm, D_KV), lambda i: (i, 0)),
                   pl.BlockSpec((D_KV, tm), lambda i: (0, i))],
        out_shape=[jax.ShapeDtypeStruct((D_ATTN, rows), BF16), jax.ShapeDtypeStruct((rows, D_KV), BF16),
                   jax.ShapeDtypeStruct((D_KV, rows), BF16)],
        compiler_params=_params(1),
        name="qkv_proj",
    )(h, gq, gkv, wq, wkv, wkv)


def _attn_kernel(sink_ref, qt_ref, kc_ref, kp_ref, km_ref, vc_ref, vp_ref, vm_ref, ot_ref, *, head, cps):
    for sub in range(cps):
        rows = pl.ds(sub * CHUNK, CHUNK)
        if sub == 0:
            k_prev, v_prev, has_prev = kp_ref, vp_ref, pl.program_id(1) >= 1
        else:
            before = pl.ds((sub - 1) * CHUNK, CHUNK)
            k_prev, v_prev, has_prev = kc_ref.at[before], vc_ref.at[:, before], True
        _attn_chunk(sink_ref, qt_ref.at[:, rows], kc_ref.at[rows], k_prev, km_ref, vc_ref.at[:, rows], v_prev, vm_ref,
                    ot_ref.at[:, rows], has_prev, head)


def _attn_chunk(sink_ref, qt_ref, kc_ref, kp_ref, km_ref, vc_ref, vp_ref, vm_ref, ot_ref, has_prev, head):
    nkeys = N_META + 2 * CHUNK
    ki = lax.broadcasted_iota(jnp.int32, (nkeys, CHUNK), 0)
    qi = lax.broadcasted_iota(jnp.int32, (nkeys, CHUNK), 1)
    kprev = ki - N_META
    kcur = ki - (N_META + CHUNK)
    in_cur = jnp.logical_and(kcur >= 0, kcur <= qi)
    if head:
        visible = jnp.logical_and(in_cur, kcur >= PAD)
    else:
        in_prev = jnp.logical_and(jnp.logical_and(kprev >= 0, kprev < CHUNK), jnp.logical_and(kprev > qi, has_prev))
        visible = jnp.logical_or(jnp.logical_or(ki < N_META, in_prev), in_cur)
    bias = jnp.concatenate([jnp.where(visible, 0.0, NEG_INF)] * N_Q_HEADS, axis=1)
    k_all = jnp.concatenate([km_ref[PAD:CHUNK, :], kp_ref[...], kc_ref[...]], axis=0)
    qw = Q_PER_KV * CHUNK
    zeros = jnp.zeros((ATTN_HEAD_DIM, qw), BF16)

    def q_heads(kh):
        return jnp.concatenate([qt_ref[hq * ATTN_HEAD_DIM:(hq + 1) * ATTN_HEAD_DIM, :]
                                for hq in range(kh * Q_PER_KV, (kh + 1) * Q_PER_KV)], axis=1)

    halves = []
    for pair in range(N_KV_HEADS // 2):
        qbd = jnp.concatenate([jnp.concatenate([q_heads(2 * pair), zeros], axis=1),
                               jnp.concatenate([zeros, q_heads(2 * pair + 1)], axis=1)], axis=0)
        halves.append(_dot(k_all[:, pair * LANES:(pair + 1) * LANES], qbd))
    s = jnp.concatenate(halves, axis=1) + bias
    sink = jnp.concatenate([jnp.full((1, CHUNK), sink_ref[hq], F32) for hq in range(N_Q_HEADS)], axis=1)
    m = jnp.maximum(jnp.max(s, axis=0, keepdims=True), sink)
    p = jnp.exp(s - m)
    inv_den = 1.0 / (jnp.sum(p, axis=0, keepdims=True) + jnp.exp(sink - m))
    pb = p.astype(BF16)
    pad_rows = jnp.zeros((PAD, qw), BF16)
    for kh in range(N_KV_HEADS):
        rows = slice(kh * ATTN_HEAD_DIM, (kh + 1) * ATTN_HEAD_DIM)
        vt = jnp.concatenate([vm_ref[rows, :], vp_ref[rows, :], vc_ref[rows, :]], axis=1)
        p_all = jnp.concatenate([pad_rows, pb[:, kh * qw:(kh + 1) * qw]], axis=0)
        o = _dot(vt, p_all) * inv_den[:, kh * qw:(kh + 1) * qw]
        for g in range(Q_PER_KV):
            hq = kh * Q_PER_KV + g
            ot_ref[hq * ATTN_HEAD_DIM:(hq + 1) * ATTN_HEAD_DIM, :] = o[:, g * CHUNK:(g + 1) * CHUNK].astype(BF16)


def _attn(qt, k, vt, k_head, vt_head, sinks, bsz, nc, head):
    rows = qt.shape[1]
    cps = _chunks_per_step(nc)
    steps = nc // cps
    cur = lambda b, c: b * steps + c
    prev = lambda b, c: b * nc + jnp.maximum(c * cps - 1, 0)
    return pl.pallas_call(
        functools.partial(_attn_kernel, head=head, cps=cps),
        grid=(bsz, steps),
        in_specs=[pl.BlockSpec(memory_space=pltpu.SMEM),
                  pl.BlockSpec((D_ATTN, cps * CHUNK), lambda b, c: (0, cur(b, c))),
                  pl.BlockSpec((cps * CHUNK, D_KV), lambda b, c: (cur(b, c), 0)),
                  pl.BlockSpec((CHUNK, D_KV), lambda b, c: (prev(b, c), 0)), _resident(k_head.shape),
                  pl.BlockSpec((D_KV, cps * CHUNK), lambda b, c: (0, cur(b, c))),
                  pl.BlockSpec((D_KV, CHUNK), lambda b, c: (0, prev(b, c))), _resident(vt_head.shape)],
        out_specs=pl.BlockSpec((D_ATTN, cps * CHUNK), lambda b, c: (0, cur(b, c))),
        out_shape=jax.ShapeDtypeStruct((D_ATTN, rows), BF16),
        compiler_params=_params(2),
        name="swa_sink_attention",
    )(sinks, qt, k, k, k_head, vt, vt, vt_head)


def kernel(x, meta_tokens, a_norm_pre, a_w_in, a_conv_w, a_conv_b, a_dt_bias, a_a_log, a_d_skip, a_gate_norm, a_w_out, a_norm_post, kv_norm, w_kv, b_norm_pre, b_w_q, b_sinks, b_w_o, b_norm_post, f_norm_pre, f_w_up, f_conv_w, f_conv_b, f_w_down, f_norm_post):
    bsz, seq, _ = x.shape
    assert seq % BODY_TM == 0
    depth = f_norm_pre.shape[0]
    n_a = a_norm_pre.shape[0]

    def row(v):
        return v.astype(F32).reshape(1, -1)

    def rep3(v):
        return jnp.pad(jnp.concatenate([v] * 3, axis=1), ((0, 0), (0, LANES - 3 * REP)))

    passes = {True: (1, CHUNK, 1), False: (bsz, seq, seq // CHUNK)}
    hs = {True: jnp.concatenate([jnp.zeros((PAD, D_MODEL), F32), meta_tokens.astype(F32)], axis=0),
          False: x.astype(F32).reshape(bsz * seq, D_MODEL)}

    w_in_b, w_up_b, w_down_b = a_w_in.astype(BF16), f_w_up.astype(BF16), f_w_down.astype(BF16)
    kv = {}
    for i in range(depth):
        mixes = {}
        if i < n_a:
            j = i
            wdt = rep3(a_w_in[j][:, D_INNER + D_XBC:]).astype(BF16)
            w_mix, g_mix, mix_transposed = a_w_out[j].astype(BF16), row(a_norm_post[j]), False
            conv_init = jnp.zeros((HALO, D_XBC), F32)
            state_init = jnp.zeros((D_STATE, D_INNER), F32)
            for head in (True, False):
                nseq, seq_rows, nc = passes[head]
                outs = _in_proj(hs[head], row(a_norm_pre[j]), w_in_b, j, wdt, a_conv_w[j].astype(F32),
                                row(a_conv_b[j]), rep3(row(a_dt_bias[j])), conv_init, seq_rows, head)
                ys = _ssd(outs[0], outs[1], outs[2], rep3(row(a_a_log[j])),
                          row(jnp.repeat(a_d_skip[j], SSM_HEAD_DIM)), row(a_gate_norm[j]), state_init, nseq, nc, head)
                mixes[head] = ys[0]
                if head:
                    conv_init, state_init = outs[3], ys[1]
        else:
            j = i - n_a
            wq, wkv = b_w_q[j].astype(BF16), w_kv.astype(BF16)
            w_mix, g_mix, mix_transposed = b_w_o[j].astype(BF16), row(b_norm_post[j]), True
            for head in (True, False):
                nseq, seq_rows, nc = passes[head]
                qt, k_new, vt_new = _qkv_proj(hs[head], row(b_norm_pre[j]), row(kv_norm), wq, wkv, head)
                if j == 0:
                    kv[head] = (k_new, vt_new)
                mixes[head] = _attn(qt, kv[head][0], kv[head][1], kv[True][0], kv[True][1],
                                    b_sinks[j].astype(F32), nseq, nc, head)
        conv_init = jnp.zeros((HALO, 2 * D_FF), F32)
        for head in (True, False):
            nseq, seq_rows, nc = passes[head]
            outs = _ffn(hs[head], mixes[head], w_mix, g_mix, row(f_norm_pre[i]), w_up_b, w_down_b, i,
                        f_conv_w[i].astype(F32), row(f_conv_b[i]), row(f_norm_post[i]), conv_init, seq_rows, head,
                        mix_transposed)
            hs[head] = outs[0]
            if head:
                conv_init = outs[1]
    return hs[False].reshape(bsz, seq, D_MODEL)
```

```python
import functools
import math

import jax
import jax.numpy as jnp
import numpy as np
from jax import lax
from jax.experimental import pallas as pl
from jax.experimental.pallas import tpu as pltpu

F32 = jnp.float32
BF16 = jnp.bfloat16

D_MODEL = 1024
N_META = 16
CHUNK = 128
PAD = CHUNK - N_META

D_INNER = 2048
SSM_HEAD_DIM = 64
SSM_HEADS = 32
SSM_GROUPS = 4
HEADS_PER_GROUP = SSM_HEADS // SSM_GROUPS
D_STATE = 128
SSM_CONV = 4
D_BC = SSM_GROUPS * D_STATE
D_XBC = D_INNER + 2 * D_BC

ATTN_HEAD_DIM = 64
N_Q_HEADS = 16
N_KV_HEADS = 4
Q_PER_KV = N_Q_HEADS // N_KV_HEADS
D_ATTN = N_Q_HEADS * ATTN_HEAD_DIM
D_KV = N_KV_HEADS * ATTN_HEAD_DIM

D_FF = 2816
FFN_CONV = 3

RMS_EPS = 1e-6
NEG_INF = -1e30

LANES = 128
HALO = 8
REP = SSM_HEADS
assert 3 * REP <= LANES
VMEM_LIMIT = 56 * 1024 * 1024
BODY_TM = 4 * CHUNK
CHUNKS_PER_STEP = 4
IN_PROJ_TM = 8 * CHUNK
IN_PROJ_TN = 512
XBC_BLOCK = 1024
XBC_BLOCKS = D_XBC // XBC_BLOCK
assert D_INNER % XBC_BLOCK == 0 and D_XBC % XBC_BLOCK == 0
FFN_TN = 256


def _rms(x):
    return x * lax.rsqrt(jnp.mean(x * x, axis=-1, keepdims=True) + RMS_EPS)


def _silu(x):
    return x * jax.nn.sigmoid(x)


def _dot(a, b):
    return jnp.dot(a, b, preferred_element_type=F32)


def _dot_nt(a, b):
    return lax.dot_general(a, b, (((1,), (1,)), ((), ())), preferred_element_type=F32)


def _dot_tn(a, b):
    return lax.dot_general(a, b, (((0,), (0,)), ((), ())), preferred_element_type=F32)


def _resident(shape):
    return pl.BlockSpec(shape, lambda *_: (0,) * len(shape), pipeline_mode=pl.Buffered(1))


def _params(n_axes):
    return pltpu.CompilerParams(dimension_semantics=("arbitrary",) * n_axes,
                                vmem_limit_bytes=VMEM_LIMIT)


def _row_tile(head):
    return CHUNK if head else BODY_TM


def _chunks_per_step(nc):
    return CHUNKS_PER_STEP if nc % CHUNKS_PER_STEP == 0 else 1


def _conv_stage(u, first, init_ref, buf, carry_ref, tail_ref, col):
    tm, tn = u.shape
    cols = slice(col, col + tn)
    buf[0:HALO, :] = jnp.where(first, init_ref[:, cols], carry_ref[:, cols])
    buf[HALO:HALO + tm, :] = u
    carry_ref[:, cols] = u[tm - HALO:tm, :]
    if tail_ref is not None:
        tail_ref[:, cols] = u[tm - HALO:tm, :]


def _conv_finish(buf, cw_ref, cb_ref, col, taps, u=None):
    tm, tn = buf.shape[0] - HALO, buf.shape[1]
    cols = slice(col, col + tn)
    out = cb_ref[:, cols] + cw_ref[taps - 1:taps, cols] * (buf[HALO:HALO + tm, :] if u is None else u)
    for k in range(taps - 1):
        s = taps - 1 - k
        out = out + cw_ref[k:k + 1, cols] * buf[HALO - s:HALO - s + tm, :]
    return out


def _normed_input(x, gain_ref, head):
    xn = _rms(x) * gain_ref[...]
    if head:
        row = lax.broadcasted_iota(jnp.int32, (x.shape[0], 1), 0)
        xn = jnp.where(row >= PAD, xn, 0.0)
    return xn.astype(BF16)


def _in_proj_kernel(*refs, tiles_per_seq, tm, tn, head):
    x_ref, g_ref, wz_ref = refs[:3]
    wx_refs = refs[3:3 + XBC_BLOCKS]
    wdt_ref, cw_ref, cb_ref, dtb_ref, init_ref, zs_ref, xbc_ref, dt_ref = refs[3 + XBC_BLOCKS:11 + XBC_BLOCKS]
    tail_ref = refs[11 + XBC_BLOCKS] if head else None
    ext_ref, carry_ref = refs[-2:]
    first = (pl.program_id(0) % tiles_per_seq) == 0
    xn = _normed_input(x_ref[...], g_ref, head)
    valid = lax.broadcasted_iota(jnp.int32, (tm, 1), 0) >= PAD
    def z_block(n0):
        zs_ref[:, n0:n0 + tn] = _silu(_dot(xn, wz_ref[:, n0:n0 + tn])).astype(BF16)

    def stage(n0):
        w_ref, w0 = wx_refs[n0 // XBC_BLOCK], n0 % XBC_BLOCK
        _conv_stage(_dot(xn, w_ref[:, w0:w0 + tn]), first, init_ref, ext_ref.at[(n0 // tn) % 2], carry_ref,
                    tail_ref, n0)

    def finish(n0):
        y = _silu(_conv_finish(ext_ref.at[(n0 // tn) % 2], cw_ref, cb_ref, n0, SSM_CONV))
        if head and n0 >= D_INNER:
            y = jnp.where(valid, y, 0.0)
        xbc_ref[:, n0:n0 + tn] = y.astype(BF16)

    z_cols, x_cols = list(range(0, D_INNER, tn)), list(range(0, D_XBC, tn))
    stage(x_cols[0])
    for i, n0 in enumerate(x_cols):
        if i + 1 < len(x_cols):
            stage(x_cols[i + 1])
        finish(n0)
        if z_cols:
            z_block(z_cols.pop(0))
    for n0 in z_cols:
        z_block(n0)
    dt_raw = _dot(xn, wdt_ref[...]) + dtb_ref[...]
    dt = jnp.maximum(dt_raw, 0.0) + jnp.log1p(jnp.exp(-jnp.abs(dt_raw)))
    dt_ref[...] = jnp.where(valid, dt, 0.0) if head else dt


def _in_proj(h, gain, w_in, layer, wdt, conv_w, conv_b, dt_bias, conv_init, seq_rows, head):
    rows = h.shape[0]
    tm, tn = (CHUNK if head else IN_PROJ_TM), IN_PROJ_TN
    assert seq_rows % tm == 0 and D_INNER % tn == 0 and XBC_BLOCK % tn == 0
    row_spec = lambda n: pl.BlockSpec((tm, n), lambda i: (i, 0))
    w_block = lambda width, idx: pl.BlockSpec((None, D_MODEL, width), lambda i: (layer, 0, idx),
                                              pipeline_mode=pl.Buffered(1))
    out_specs = [row_spec(D_INNER), row_spec(D_XBC), row_spec(LANES)]
    out_shape = [jax.ShapeDtypeStruct((rows, D_INNER), BF16), jax.ShapeDtypeStruct((rows, D_XBC), BF16),
                 jax.ShapeDtypeStruct((rows, LANES), F32)]
    if head:
        out_specs.append(_resident((HALO, D_XBC)))
        out_shape.append(jax.ShapeDtypeStruct((HALO, D_XBC), F32))
    return pl.pallas_call(
        functools.partial(_in_proj_kernel, tiles_per_seq=seq_rows // tm, tm=tm, tn=tn, head=head),
        grid=(rows // tm,),
        in_specs=[row_spec(D_MODEL), _resident((1, D_MODEL)), w_block(D_INNER, 0)]
                 + [w_block(XBC_BLOCK, D_INNER // XBC_BLOCK + k) for k in range(XBC_BLOCKS)]
                 + [_resident(wdt.shape), _resident(conv_w.shape), _resident(conv_b.shape),
                    _resident(dt_bias.shape), _resident(conv_init.shape)],
        out_specs=out_specs,
        out_shape=out_shape,
        scratch_shapes=[pltpu.VMEM((2, HALO + tm, tn), F32), pltpu.VMEM((HALO, D_XBC), F32)],
        compiler_params=_params(1),
        name="mamba_in_proj",
    )(h, gain, w_in, *([w_in] * XBC_BLOCKS), wdt, conv_w, conv_b, dt_bias, conv_init)


def _split3(x):
    hi = x.astype(BF16).astype(F32)
    r = x - hi
    mid = r.astype(BF16).astype(F32)
    return hi, mid, r - mid


def _pack3(x, lane, fill):
    hi, mid, lo = _split3(x)
    packed = jnp.where(lane < REP, hi, jnp.where(lane < 2 * REP, mid, jnp.where(lane < 3 * REP, lo, fill)))
    return packed.astype(BF16)


def _ssd_kernel(*refs, head, cps):
    (zs_ref, xbc_ref, dt_ref, alog_ref, dskip_ref, gn_ref, tril3_ref, e3_ref, selc_ref, init_ref, o_ref) = refs[:11]
    final_ref = refs[11] if head else None
    state_ref, yg_ref, cb_ref, yoff_ref, seg_ref = refs[-5:]

    @pl.when(pl.program_id(1) == 0)
    def _():
        state_ref[...] = init_ref[...]

    for sub in range(cps):
        rows = pl.ds(sub * CHUNK, CHUNK)
        _ssd_chunk(zs_ref.at[rows], xbc_ref.at[rows], dt_ref.at[rows], alog_ref, dskip_ref, gn_ref, tril3_ref,
                   e3_ref, selc_ref, o_ref.at[rows], state_ref, yg_ref.at[rows], cb_ref, yoff_ref, seg_ref)
    if head:
        final_ref[...] = state_ref[...]


def _ssd_chunk(zs_ref, xbc_ref, dt_ref, alog_ref, dskip_ref, gn_ref, tril3_ref, e3_ref, selc_ref, o_ref,
               state_ref, yg_ref, cb_ref, yoff_ref, seg_ref):
    lane = lax.broadcasted_iota(jnp.int32, (CHUNK, LANES), 1)
    li = lax.broadcasted_iota(jnp.int32, (CHUNK, CHUNK), 0)
    cbias = jnp.where(li >= lane, 0.0, NEG_INF)
    cbias = jnp.concatenate([cbias] * HEADS_PER_GROUP, axis=1)

    dt = dt_ref[...]
    a = dt * (-jnp.exp(alog_ref[...]))
    a3 = jnp.concatenate([t.astype(BF16) for t in _split3(a)], axis=0)
    acs = _dot(tril3_ref[...], a3)
    b = jnp.maximum(jnp.log(dt), NEG_INF) - acs
    expa = jnp.exp(acs)
    wst = jnp.exp(acs[CHUNK - 1:CHUNK, :] + b)
    spread = _dot(jnp.concatenate([_pack3(expa, lane, 0.0), _pack3(wst, lane, 0.0)], axis=0), e3_ref[...])
    e_exp = spread[:CHUNK]
    w_exp = spread[CHUNK:]
    u3 = _pack3(acs, lane, 1.0)
    bt_terms = _split3(b.T)
    sub = lax.broadcasted_iota(jnp.int32, (HEADS_PER_GROUP, HEADS_PER_GROUP * CHUNK), 0)
    blk = lax.broadcasted_iota(jnp.int32, (HEADS_PER_GROUP, HEADS_PER_GROUP * CHUNK), 1) // CHUNK
    on_diag = sub == blk

    xs_f = xbc_ref[:, :D_INNER].astype(F32)
    xw_b = (xs_f * w_exp).astype(BF16)
    lane_lo = lane < SSM_HEAD_DIM
    ssq = jnp.zeros((CHUNK, 1), F32)
    gw = HEADS_PER_GROUP * SSM_HEAD_DIM
    def stage(g):
        gc = slice(g * gw, (g + 1) * gw)
        bg_b = xbc_ref[:, D_INNER + g * D_STATE:D_INNER + (g + 1) * D_STATE]
        cg_b = xbc_ref[:, D_INNER + D_BC + g * D_STATE:D_INNER + D_BC + (g + 1) * D_STATE]
        cb_ref[g % 2] = _dot_nt(cg_b, bg_b)
        st = state_ref[:, gc]
        yoff_ref[g % 2] = _dot(cg_b, st.astype(BF16))
        state_ref[:, gc] = st * e_exp[CHUNK - 1:CHUNK, gc] + _dot_tn(bg_b, xw_b[:, gc])
        dyn = [jnp.where(on_diag, jnp.concatenate([t[g * HEADS_PER_GROUP:(g + 1) * HEADS_PER_GROUP, :]]
                                                  * HEADS_PER_GROUP, axis=1), 0.0) for t in bt_terms]
        dyn.append(jnp.zeros_like(dyn[0]))
        v3 = jnp.concatenate([selc_ref[g], jnp.concatenate(dyn, axis=0).astype(BF16)], axis=0)
        seg_ref[g % 2] = _dot(u3, v3)

    stage(0)
    for g in range(SSM_GROUPS):
        if g + 1 < SSM_GROUPS:
            stage(g + 1)
        gc = slice(g * gw, (g + 1) * gw)
        cb_mat, y_off, seg = cb_ref[g % 2], yoff_ref[g % 2], seg_ref[g % 2]
        m = (jnp.exp(seg + cbias) * jnp.concatenate([cb_mat] * HEADS_PER_GROUP, axis=1)).astype(BF16)
        y_parts = []
        for pr in range(HEADS_PER_GROUP // 2):
            c0 = g * gw + pr * LANES
            xp = xbc_ref[:, c0:c0 + LANES]
            zero = jnp.zeros_like(xp)
            rhs = jnp.concatenate([jnp.where(lane_lo, xp, zero), jnp.where(lane_lo, zero, xp)], axis=0)
            y_parts.append(_dot(m[:, pr * 2 * CHUNK:(pr + 1) * 2 * CHUNK], rhs))
        y = jnp.concatenate(y_parts, axis=1) + y_off * e_exp[:, gc] + dskip_ref[:, gc] * xs_f[:, gc]
        y = y * zs_ref[:, gc].astype(F32)
        ssq = ssq + jnp.sum(y * y, axis=-1, keepdims=True)
        yg_ref[:, gc] = y
    inv = lax.rsqrt(ssq * (1.0 / D_INNER) + RMS_EPS)
    o_ref[...] = (yg_ref[...] * inv * gn_ref[...]).astype(BF16)


def _ssd_constants():
    tril3 = np.tile(np.tril(np.ones((CHUNK, CHUNK), np.float32)), (1, 3))
    e3 = np.zeros((LANES, D_INNER), np.float32)
    selc = np.zeros((SSM_GROUPS, 3 * REP, HEADS_PER_GROUP * CHUNK), np.float32)
    for t in range(3):
        for h in range(SSM_HEADS):
            e3[REP * t + h, h * SSM_HEAD_DIM:(h + 1) * SSM_HEAD_DIM] = 1.0
            g, k = divmod(h, HEADS_PER_GROUP)
            selc[g, REP * t + h, k * CHUNK:(k + 1) * CHUNK] = 1.0
    return jnp.asarray(tril3, BF16), jnp.asarray(e3, BF16), jnp.asarray(selc, BF16)


def _ssd(zs, xbc, dt, a_log, d_skip, gate_norm, state_init, bsz, nc, head):
    rows = zs.shape[0]
    tril3, e3, selc = _ssd_constants()
    cps = _chunks_per_step(nc)
    steps = nc // cps
    row_spec = lambda n: pl.BlockSpec((cps * CHUNK, n), lambda b, c: (b * steps + c, 0))
    out_specs = [row_spec(D_INNER)]
    out_shape = [jax.ShapeDtypeStruct((rows, D_INNER), BF16)]
    if head:
        out_specs.append(_resident((D_STATE, D_INNER)))
        out_shape.append(jax.ShapeDtypeStruct((D_STATE, D_INNER), F32))
    return pl.pallas_call(
        functools.partial(_ssd_kernel, head=head, cps=cps),
        grid=(bsz, steps),
        in_specs=[row_spec(D_INNER), row_spec(D_XBC), row_spec(LANES),
                  _resident(a_log.shape), _resident(d_skip.shape), _resident(gate_norm.shape),
                  _resident(tril3.shape), _resident(e3.shape), _resident(selc.shape),
                  _resident(state_init.shape)],
        out_specs=out_specs,
        out_shape=out_shape,
        scratch_shapes=[pltpu.VMEM((D_STATE, D_INNER), F32),
                        pltpu.VMEM((cps * CHUNK, D_INNER), F32),
                        pltpu.VMEM((2, CHUNK, CHUNK), F32),
                        pltpu.VMEM((2, CHUNK, HEADS_PER_GROUP * SSM_HEAD_DIM), F32),
                        pltpu.VMEM((2, CHUNK, HEADS_PER_GROUP * CHUNK), F32)],
        compiler_params=_params(2),
        name="mamba_ssd",
    )(zs, xbc, dt, a_log, d_skip, gate_norm, tril3, e3, selc, state_init)


def _ffn_kernel(*refs, tiles_per_seq, tm, tn, head, mix_transposed):
    (res_ref, mix_ref, wmix_ref, gmix_ref, gpre_ref, wg_ref, wv_ref, cw_ref, cb_ref, wd_ref, gpost_ref,
     init_ref, o_ref) = refs[:13]
    tail_ref = refs[13] if head else None
    ext_ref, carry_ref, act_ref = refs[-3:]
    first = (pl.program_id(0) % tiles_per_seq) == 0
    mix = _dot_tn(mix_ref[...], wmix_ref[...]) if mix_transposed else _dot(mix_ref[...], wmix_ref[...])
    x = res_ref[...] + _rms(mix) * gmix_ref[...]
    xn = _normed_input(x, gpre_ref, head)

    def up_conv(w_ref, col):
        u = _dot(xn, w_ref[:, col % D_FF:col % D_FF + tn])
        _conv_stage(u, first, init_ref, ext_ref, carry_ref, tail_ref, col)
        return _conv_finish(ext_ref, cw_ref, cb_ref, col, FFN_CONV, u)

    for j in range(D_FF // tn):
        gate = up_conv(wg_ref, j * tn)
        val = up_conv(wv_ref, D_FF + j * tn)
        act_ref[:, j * tn:(j + 1) * tn] = (_silu(gate) * val).astype(BF16)
    y = _dot(act_ref[...], wd_ref[...])
    o_ref[...] = x + _rms(y) * gpost_ref[...]


def _ffn(res, mix, wmix, gmix, gpre, w_up, w_down, layer, conv_w, conv_b, gpost, conv_init, seq_rows, head,
         mix_transposed):
    rows = res.shape[0]
    k = wmix.shape[0]
    tm, tn = _row_tile(head), FFN_TN
    assert seq_rows % tm == 0 and D_FF % tn == 0
    row_spec = pl.BlockSpec((tm, D_MODEL), lambda i: (i, 0))
    mix_spec = pl.BlockSpec((k, tm), lambda i: (0, i)) if mix_transposed else pl.BlockSpec((tm, k), lambda i: (i, 0))
    up_half = lambda half: pl.BlockSpec((None, D_MODEL, D_FF), lambda i: (layer, 0, half), pipeline_mode=pl.Buffered(1))
    down_spec = pl.BlockSpec((None, D_FF, D_MODEL), lambda i: (layer, 0, 0), pipeline_mode=pl.Buffered(1))
    out_specs = [row_spec]
    out_shape = [jax.ShapeDtypeStruct((rows, D_MODEL), F32)]
    if head:
        out_specs.append(_resident((HALO, 2 * D_FF)))
        out_shape.append(jax.ShapeDtypeStruct((HALO, 2 * D_FF), F32))
    return pl.pallas_call(
        functools.partial(_ffn_kernel, tiles_per_seq=seq_rows // tm, tm=tm, tn=tn, head=head,
                          mix_transposed=mix_transposed),
        grid=(rows // tm,),
        in_specs=[row_spec, mix_spec, _resident(wmix.shape), _resident(gmix.shape), _resident(gpre.shape),
                  up_half(0), up_half(1), _resident(conv_w.shape), _resident(conv_b.shape), down_spec,
                  _resident(gpost.shape), _resident(conv_init.shape)],
        out_specs=out_specs,
        out_shape=out_shape,
        scratch_shapes=[pltpu.VMEM((HALO + tm, tn), F32),
                        pltpu.VMEM((HALO, 2 * D_FF), F32),
                        pltpu.VMEM((tm, D_FF), BF16)],
        compiler_params=_params(1),
        name="mix_proj_conv_ffn",
    )(res, mix, wmix, gmix, gpre, w_up, w_up, conv_w, conv_b, w_down, gpost, conv_init)


def _qkv_proj_kernel(x_ref, gq_ref, gkv_ref, wq_ref, wk_ref, wv_ref, qt_ref, k_ref, vt_ref):
    xr = _rms(x_ref[...])
    xq = (xr * gq_ref[...]).astype(BF16)
    xkv = (xr * gkv_ref[...]).astype(BF16)
    scale = 1.0 / math.sqrt(ATTN_HEAD_DIM)
    qt_ref[...] = (_dot(xq, wq_ref[...]) * scale).T.astype(BF16)
    k_ref[...] = _dot(xkv, wk_ref[...]).astype(BF16)
    vt_ref[...] = _dot(xkv, wv_ref[...]).T.astype(BF16)


def _qkv_proj(h, gq, gkv, wq, wkv, head):
    rows = h.shape[0]
    tm = _row_tile(head)
    return pl.pallas_call(
        _qkv_proj_kernel,
        grid=(rows // tm,),
        in_specs=[pl.BlockSpec((tm, D_MODEL), lambda i: (i, 0)), _resident(gq.shape), _resident(gkv.shape),
                  _resident(wq.shape), pl.BlockSpec((D_MODEL, D_KV), lambda i: (0, 0), pipeline_mode=pl.Buffered(1)),
                  pl.BlockSpec((D_MODEL, D_KV), lambda i: (0, 1), pipeline_mode=pl.Buffered(1))],
        out_specs=[pl.BlockSpec((D_ATTN, tm), lambda i: (0, i)), pl.BlockSpec((tm, D_KV), lambda i: (i, 0)),
                   pl.BlockSpec((D_KV, tm), lambda i: (0, i))],
        out_shape=[jax.ShapeDtypeStruct((D_ATTN, rows), BF16), jax.ShapeDtypeStruct((rows, D_KV), BF16),
                   jax.ShapeDtypeStruct((D_KV, rows), BF16)],
        compiler_params=_params(1),
        name="qkv_proj",
    )(h, gq, gkv, wq, wkv, wkv)


def _attn_kernel(sink_ref, qt_ref, kc_ref, kp_ref, km_ref, vc_ref, vp_ref, vm_ref, ot_ref, s_ref, *, head, cps):
    def views(sub):
        rows = pl.ds(sub * CHUNK, CHUNK)
        if sub == 0:
            return rows, kp_ref, vp_ref, pl.program_id(1) >= 1
        before = pl.ds((sub - 1) * CHUNK, CHUNK)
        return rows, kc_ref.at[before], vc_ref.at[:, before], True

    def scores(sub):
        rows, k_prev, _, _ = views(sub)
        _attn_scores(qt_ref.at[:, rows], kc_ref.at[rows], k_prev, km_ref, s_ref.at[sub % 2])

    def finish(sub):
        rows, _, v_prev, has_prev = views(sub)
        _attn_finish(sink_ref, s_ref.at[sub % 2], vc_ref.at[:, rows], v_prev, vm_ref, ot_ref.at[:, rows], has_prev, head)

    scores(0)
    for sub in range(cps):
        if sub + 1 < cps:
            scores(sub + 1)
        finish(sub)


def _attn_scores(qt_ref, kc_ref, kp_ref, km_ref, s_ref):
    k_all = jnp.concatenate([km_ref[PAD:CHUNK, :], kp_ref[...], kc_ref[...]], axis=0)
    qw = Q_PER_KV * CHUNK
    zeros = jnp.zeros((ATTN_HEAD_DIM, qw), BF16)

    def q_heads(kh):
        return jnp.concatenate([qt_ref[hq * ATTN_HEAD_DIM:(hq + 1) * ATTN_HEAD_DIM, :]
                                for hq in range(kh * Q_PER_KV, (kh + 1) * Q_PER_KV)], axis=1)

    for pair in range(N_KV_HEADS // 2):
        qbd = jnp.concatenate([jnp.concatenate([q_heads(2 * pair), zeros], axis=1),
                               jnp.concatenate([zeros, q_heads(2 * pair + 1)], axis=1)], axis=0)
        s_ref[:, pair * 2 * qw:(pair + 1) * 2 * qw] = _dot(k_all[:, pair * LANES:(pair + 1) * LANES], qbd)


def _attn_finish(sink_ref, s_ref, vc_ref, vp_ref, vm_ref, ot_ref, has_prev, head):
    nkeys = N_META + 2 * CHUNK
    ki = lax.broadcasted_iota(jnp.int32, (nkeys, CHUNK), 0)
    qi = lax.broadcasted_iota(jnp.int32, (nkeys, CHUNK), 1)
    kprev = ki - N_META
    kcur = ki - (N_META + CHUNK)
    in_cur = jnp.logical_and(kcur >= 0, kcur <= qi)
    if head:
        visible = jnp.logical_and(in_cur, kcur >= PAD)
    else:
        in_prev = jnp.logical_and(jnp.logical_and(kprev >= 0, kprev < CHUNK), jnp.logical_and(kprev > qi, has_prev))
        visible = jnp.logical_or(jnp.logical_or(ki < N_META, in_prev), in_cur)
    bias = jnp.concatenate([jnp.where(visible, 0.0, NEG_INF)] * N_Q_HEADS, axis=1)
    qw = Q_PER_KV * CHUNK
    s = s_ref[...] + bias
    sink = jnp.concatenate([jnp.full((1, CHUNK), sink_ref[hq], F32) for hq in range(N_Q_HEADS)], axis=1)
    m = jnp.maximum(jnp.max(s, axis=0, keepdims=True), sink)
    p = jnp.exp(s - m)
    inv_den = 1.0 / (jnp.sum(p, axis=0, keepdims=True) + jnp.exp(sink - m))
    pb = p.astype(BF16)
    pad_rows = jnp.zeros((PAD, qw), BF16)
    for kh in range(N_KV_HEADS):
        rows = slice(kh * ATTN_HEAD_DIM, (kh + 1) * ATTN_HEAD_DIM)
        vt = jnp.concatenate([vm_ref[rows, :], vp_ref[rows, :], vc_ref[rows, :]], axis=1)
        p_all = jnp.concatenate([pad_rows, pb[:, kh * qw:(kh + 1) * qw]], axis=0)
        o = _dot(vt, p_all) * inv_den[:, kh * qw:(kh + 1) * qw]
        for g in range(Q_PER_KV):
            hq = kh * Q_PER_KV + g
            ot_ref[hq * ATTN_HEAD_DIM:(hq + 1) * ATTN_HEAD_DIM, :] = o[:, g * CHUNK:(g + 1) * CHUNK].astype(BF16)


def _attn(qt, k, vt, k_head, vt_head, sinks, bsz, nc, head):
    rows = qt.shape[1]
    cps = _chunks_per_step(nc)
    steps = nc // cps
    cur = lambda b, c: b * steps + c
    prev = lambda b, c: b * nc + jnp.maximum(c * cps - 1, 0)
    return pl.pallas_call(
        functools.partial(_attn_kernel, head=head, cps=cps),
        grid=(bsz, steps),
        in_specs=[pl.BlockSpec(memory_space=pltpu.SMEM),
                  pl.BlockSpec((D_ATTN, cps * CHUNK), lambda b, c: (0, cur(b, c))),
                  pl.BlockSpec((cps * CHUNK, D_KV), lambda b, c: (cur(b, c), 0)),
                  pl.BlockSpec((CHUNK, D_KV), lambda b, c: (prev(b, c), 0)), _resident(k_head.shape),
                  pl.BlockSpec((D_KV, cps * CHUNK), lambda b, c: (0, cur(b, c))),
                  pl.BlockSpec((D_KV, CHUNK), lambda b, c: (0, prev(b, c))), _resident(vt_head.shape)],
        out_specs=pl.BlockSpec((D_ATTN, cps * CHUNK), lambda b, c: (0, cur(b, c))),
        out_shape=jax.ShapeDtypeStruct((D_ATTN, rows), BF16),
        scratch_shapes=[pltpu.VMEM((2, N_META + 2 * CHUNK, N_Q_HEADS * CHUNK), F32)],
        compiler_params=_params(2),
        name="swa_sink_attention",
    )(sinks, qt, k, k, k_head, vt, vt, vt_head)


def kernel(x, meta_tokens, a_norm_pre, a_w_in, a_conv_w, a_conv_b, a_dt_bias, a_a_log, a_d_skip, a_gate_norm, a_w_out, a_norm_post, kv_norm, w_kv, b_norm_pre, b_w_q, b_sinks, b_w_o, b_norm_post, f_norm_pre, f_w_up, f_conv_w, f_conv_b, f_w_down, f_norm_post):
    bsz, seq, _ = x.shape
    assert seq % BODY_TM == 0
    depth = f_norm_pre.shape[0]
    n_a = a_norm_pre.shape[0]

    def row(v):
        return v.astype(F32).reshape(1, -1)

    def rep3(v):
        return jnp.pad(jnp.concatenate([v] * 3, axis=1), ((0, 0), (0, LANES - 3 * REP)))

    passes = {True: (1, CHUNK, 1), False: (bsz, seq, seq // CHUNK)}
    hs = {True: jnp.concatenate([jnp.zeros((PAD, D_MODEL), F32), meta_tokens.astype(F32)], axis=0),
          False: x.astype(F32).reshape(bsz * seq, D_MODEL)}

    w_in_b, w_up_b, w_down_b = a_w_in.astype(BF16), f_w_up.astype(BF16), f_w_down.astype(BF16)
    kv = {}
    for i in range(depth):
        mixes = {}
        if i < n_a:
            j = i
            wdt = rep3(a_w_in[j][:, D_INNER + D_XBC:]).astype(BF16)
            w_mix, g_mix, mix_transposed = a_w_out[j].astype(BF16), row(a_norm_post[j]), False
            conv_init = jnp.zeros((HALO, D_XBC), F32)
            state_init = jnp.zeros((D_STATE, D_INNER), F32)
            for head in (True, False):
                nseq, seq_rows, nc = passes[head]
                outs = _in_proj(hs[head], row(a_norm_pre[j]), w_in_b, j, wdt, a_conv_w[j].astype(F32),
                                row(a_conv_b[j]), rep3(row(a_dt_bias[j])), conv_init, seq_rows, head)
                ys = _ssd(outs[0], outs[1], outs[2], rep3(row(a_a_log[j])),
                          row(jnp.repeat(a_d_skip[j], SSM_HEAD_DIM)), row(a_gate_norm[j]), state_init, nseq, nc, head)
                mixes[head] = ys[0]
                if head:
                    conv_init, state_init = outs[3], ys[1]
        else:
            j = i - n_a
            wq, wkv = b_w_q[j].astype(BF16), w_kv.astype(BF16)
            w_mix, g_mix, mix_transposed = b_w_o[j].astype(BF16), row(b_norm_post[j]), True
            for head in (True, False):
                nseq, seq_rows, nc = passes[head]
                qt, k_new, vt_new = _qkv_proj(hs[head], row(b_norm_pre[j]), row(kv_norm), wq, wkv, head)
                if j == 0:
                    kv[head] = (k_new, vt_new)
                mixes[head] = _attn(qt, kv[head][0], kv[head][1], kv[True][0], kv[True][1],
                                    b_sinks[j].astype(F32), nseq, nc, head)
        conv_init = jnp.zeros((HALO, 2 * D_FF), F32)
        for head in (True, False):
            nseq, seq_rows, nc = passes[head]
            outs = _ffn(hs[head], mixes[head], w_mix, g_mix, row(f_norm_pre[i]), w_up_b, w_down_b, i,
                        f_conv_w[i].astype(F32), row(f_conv_b[i]), row(f_norm_post[i]), conv_init, seq_rows, head,
                        mix_transposed)
            hs[head] = outs[0]
            if head:
                conv_init = outs[1]
    return hs[False].reshape(bsz, seq, D_MODEL)
```

```python
import functools
import math

import jax
import jax.numpy as jnp
import numpy as np
from jax import lax
from jax.experimental import pallas as pl
from jax.experimental.pallas import tpu as pltpu

F32 = jnp.float32
BF16 = jnp.bfloat16

D_MODEL = 1024
N_META = 16
CHUNK = 128
PAD = CHUNK - N_META

D_INNER = 2048
SSM_HEAD_DIM = 64
SSM_HEADS = 32
SSM_GROUPS = 4
HEADS_PER_GROUP = SSM_HEADS // SSM_GROUPS
D_STATE = 128
SSM_CONV = 4
D_BC = SSM_GROUPS * D_STATE
D_XBC = D_INNER + 2 * D_BC

ATTN_HEAD_DIM = 64
N_Q_HEADS = 16
N_KV_HEADS = 4
Q_PER_KV = N_Q_HEADS // N_KV_HEADS
D_ATTN = N_Q_HEADS * ATTN_HEAD_DIM
D_KV = N_KV_HEADS * ATTN_HEAD_DIM

D_FF = 2816
FFN_CONV = 3

RMS_EPS = 1e-6
NEG_INF = -1e30

LANES = 128
HALO = 8
REP = SSM_HEADS
assert 3 * REP <= LANES
VMEM_LIMIT = 56 * 1024 * 1024
BODY_TM = 4 * CHUNK
CHUNKS_PER_STEP = 4
IN_PROJ_TM = 8 * CHUNK
IN_PROJ_TN = 512
XBC_BLOCK = 1024
XBC_BLOCKS = D_XBC // XBC_BLOCK
assert D_INNER % XBC_BLOCK == 0 and D_XBC % XBC_BLOCK == 0
FFN_TN = 256


def _rms(x):
    return x * lax.rsqrt(jnp.mean(x * x, axis=-1, keepdims=True) + RMS_EPS)


def _silu(x):
    return x * jax.nn.sigmoid(x)


def _dot(a, b):
    return jnp.dot(a, b, preferred_element_type=F32)


def _dot_nt(a, b):
    return lax.dot_general(a, b, (((1,), (1,)), ((), ())), preferred_element_type=F32)


def _dot_tn(a, b):
    return lax.dot_general(a, b, (((0,), (0,)), ((), ())), preferred_element_type=F32)


def _resident(shape):
    return pl.BlockSpec(shape, lambda *_: (0,) * len(shape), pipeline_mode=pl.Buffered(1))


def _params(n_axes):
    return pltpu.CompilerParams(dimension_semantics=("arbitrary",) * n_axes,
                                vmem_limit_bytes=VMEM_LIMIT)


def _row_tile(head):
    return CHUNK if head else BODY_TM


def _chunks_per_step(nc):
    return CHUNKS_PER_STEP if nc % CHUNKS_PER_STEP == 0 else 1


def _conv_stage(u, first, init_ref, buf, carry_ref, tail_ref, col):
    tm, tn = u.shape
    cols = slice(col, col + tn)
    buf[0:HALO, :] = jnp.where(first, init_ref[:, cols], carry_ref[:, cols])
    buf[HALO:HALO + tm, :] = u
    carry_ref[:, cols] = u[tm - HALO:tm, :]
    if tail_ref is not None:
        tail_ref[:, cols] = u[tm - HALO:tm, :]


def _conv_finish(buf, cw_ref, cb_ref, col, taps, u=None):
    tm, tn = buf.shape[0] - HALO, buf.shape[1]
    cols = slice(col, col + tn)
    out = cb_ref[:, cols] + cw_ref[taps - 1:taps, cols] * (buf[HALO:HALO + tm, :] if u is None else u)
    for k in range(taps - 1):
        s = taps - 1 - k
        out = out + cw_ref[k:k + 1, cols] * buf[HALO - s:HALO - s + tm, :]
    return out


def _normed_input(x, gain_ref, head):
    xn = _rms(x) * gain_ref[...]
    if head:
        row = lax.broadcasted_iota(jnp.int32, (x.shape[0], 1), 0)
        xn = jnp.where(row >= PAD, xn, 0.0)
    return xn.astype(BF16)


def _in_proj_kernel(*refs, tiles_per_seq, tm, tn, head):
    x_ref, g_ref, wz_ref = refs[:3]
    wx_refs = refs[3:3 + XBC_BLOCKS]
    wdt_ref, cw_ref, cb_ref, dtb_ref, init_ref, zs_ref, xbc_ref, dt_ref = refs[3 + XBC_BLOCKS:11 + XBC_BLOCKS]
    tail_ref = refs[11 + XBC_BLOCKS] if head else None
    ext_ref, carry_ref = refs[-2:]
    first = (pl.program_id(0) % tiles_per_seq) == 0
    xn = _normed_input(x_ref[...], g_ref, head)
    valid = lax.broadcasted_iota(jnp.int32, (tm, 1), 0) >= PAD
    def z_block(n0):
        zs_ref[:, n0:n0 + tn] = _silu(_dot(xn, wz_ref[:, n0:n0 + tn])).astype(BF16)

    def stage(n0):
        w_ref, w0 = wx_refs[n0 // XBC_BLOCK], n0 % XBC_BLOCK
        _conv_stage(_dot(xn, w_ref[:, w0:w0 + tn]), first, init_ref, ext_ref.at[(n0 // tn) % 2], carry_ref,
                    tail_ref, n0)

    def finish(n0):
        y = _silu(_conv_finish(ext_ref.at[(n0 // tn) % 2], cw_ref, cb_ref, n0, SSM_CONV))
        if head and n0 >= D_INNER:
            y = jnp.where(valid, y, 0.0)
        xbc_ref[:, n0:n0 + tn] = y.astype(BF16)

    z_cols, x_cols = list(range(0, D_INNER, tn)), list(range(0, D_XBC, tn))
    stage(x_cols[0])
    for i, n0 in enumerate(x_cols):
        if i + 1 < len(x_cols):
            stage(x_cols[i + 1])
        finish(n0)
        if z_cols:
            z_block(z_cols.pop(0))
    for n0 in z_cols:
        z_block(n0)
    dt_raw = _dot(xn, wdt_ref[...]) + dtb_ref[...]
    dt = jnp.maximum(dt_raw, 0.0) + jnp.log1p(jnp.exp(-jnp.abs(dt_raw)))
    dt_ref[...] = jnp.where(valid, dt, 0.0) if head else dt


def _in_proj(h, gain, w_in, layer, wdt, conv_w, conv_b, dt_bias, conv_init, seq_rows, head):
    rows = h.shape[0]
    tm, tn = (CHUNK if head else IN_PROJ_TM), IN_PROJ_TN
    assert seq_rows % tm == 0 and D_INNER % tn == 0 and XBC_BLOCK % tn == 0
    row_spec = lambda n: pl.BlockSpec((tm, n), lambda i: (i, 0))
    w_block = lambda width, idx: pl.BlockSpec((None, D_MODEL, width), lambda i: (layer, 0, idx),
                                              pipeline_mode=pl.Buffered(1))
    out_specs = [row_spec(D_INNER), row_spec(D_XBC), row_spec(LANES)]
    out_shape = [jax.ShapeDtypeStruct((rows, D_INNER), BF16), jax.ShapeDtypeStruct((rows, D_XBC), BF16),
                 jax.ShapeDtypeStruct((rows, LANES), F32)]
    if head:
        out_specs.append(_resident((HALO, D_XBC)))
        out_shape.append(jax.ShapeDtypeStruct((HALO, D_XBC), F32))
    return pl.pallas_call(
        functools.partial(_in_proj_kernel, tiles_per_seq=seq_rows // tm, tm=tm, tn=tn, head=head),
        grid=(rows // tm,),
        in_specs=[row_spec(D_MODEL), _resident((1, D_MODEL)), w_block(D_INNER, 0)]
                 + [w_block(XBC_BLOCK, D_INNER // XBC_BLOCK + k) for k in range(XBC_BLOCKS)]
                 + [_resident(wdt.shape), _resident(conv_w.shape), _resident(conv_b.shape),
                    _resident(dt_bias.shape), _resident(conv_init.shape)],
        out_specs=out_specs,
        out_shape=out_shape,
        scratch_shapes=[pltpu.VMEM((2, HALO + tm, tn), F32), pltpu.VMEM((HALO, D_XBC), F32)],
        compiler_params=_params(1),
        name="mamba_in_proj",
    )(h, gain, w_in, *([w_in] * XBC_BLOCKS), wdt, conv_w, conv_b, dt_bias, conv_init)


def _split3(x):
    hi = x.astype(BF16).astype(F32)
    r = x - hi
    mid = r.astype(BF16).astype(F32)
    return hi, mid, r - mid


def _pack3(x, lane, fill):
    hi, mid, lo = _split3(x)
    packed = jnp.where(lane < REP, hi, jnp.where(lane < 2 * REP, mid, jnp.where(lane < 3 * REP, lo, fill)))
    return packed.astype(BF16)


def _ssd_kernel(*refs, head, cps):
    (zs_ref, xbc_ref, dt_ref, alog_ref, dskip_ref, gn_ref, tril3_ref, e3_ref, selc_ref, init_ref, o_ref) = refs[:11]
    final_ref = refs[11] if head else None
    state_ref, yg_ref, cb_ref, yoff_ref, seg_ref = refs[-5:]

    @pl.when(pl.program_id(1) == 0)
    def _():
        state_ref[...] = init_ref[...]

    for sub in range(cps):
        rows = pl.ds(sub * CHUNK, CHUNK)
        _ssd_chunk(zs_ref.at[rows], xbc_ref.at[rows], dt_ref.at[rows], alog_ref, dskip_ref, gn_ref, tril3_ref,
                   e3_ref, selc_ref, o_ref.at[rows], state_ref, yg_ref.at[rows], cb_ref, yoff_ref, seg_ref)
    if head:
        final_ref[...] = state_ref[...]


def _ssd_chunk(zs_ref, xbc_ref, dt_ref, alog_ref, dskip_ref, gn_ref, tril3_ref, e3_ref, selc_ref, o_ref,
               state_ref, yg_ref, cb_ref, yoff_ref, seg_ref):
    lane = lax.broadcasted_iota(jnp.int32, (CHUNK, LANES), 1)
    li = lax.broadcasted_iota(jnp.int32, (CHUNK, CHUNK), 0)
    cbias = jnp.where(li >= lane, 0.0, NEG_INF)
    cbias = jnp.concatenate([cbias] * HEADS_PER_GROUP, axis=1)

    dt = dt_ref[...]
    a = dt * (-jnp.exp(alog_ref[...]))
    a3 = jnp.concatenate([t.astype(BF16) for t in _split3(a)], axis=0)
    acs = _dot(tril3_ref[...], a3)
    b = jnp.maximum(jnp.log(dt), NEG_INF) - acs
    expa = jnp.exp(acs)
    wst = jnp.exp(acs[CHUNK - 1:CHUNK, :] + b)
    spread = _dot(jnp.concatenate([_pack3(expa, lane, 0.0), _pack3(wst, lane, 0.0)], axis=0), e3_ref[...])
    e_exp = spread[:CHUNK]
    w_exp = spread[CHUNK:]
    u3 = _pack3(acs, lane, 1.0)
    bt_terms = _split3(b.T)
    sub = lax.broadcasted_iota(jnp.int32, (HEADS_PER_GROUP, HEADS_PER_GROUP * CHUNK), 0)
    blk = lax.broadcasted_iota(jnp.int32, (HEADS_PER_GROUP, HEADS_PER_GROUP * CHUNK), 1) // CHUNK
    on_diag = sub == blk

    xs_f = xbc_ref[:, :D_INNER].astype(F32)
    xw_b = (xs_f * w_exp).astype(BF16)
    lane_lo = lane < SSM_HEAD_DIM
    ssq = jnp.zeros((CHUNK, 1), F32)
    gw = HEADS_PER_GROUP * SSM_HEAD_DIM
    def stage(g):
        gc = slice(g * gw, (g + 1) * gw)
        bg_b = xbc_ref[:, D_INNER + g * D_STATE:D_INNER + (g + 1) * D_STATE]
        cg_b = xbc_ref[:, D_INNER + D_BC + g * D_STATE:D_INNER + D_BC + (g + 1) * D_STATE]
        cb_ref[g % 2] = _dot_nt(cg_b, bg_b)
        st = state_ref[:, gc]
        yoff_ref[g % 2] = _dot(cg_b, st.astype(BF16))
        state_ref[:, gc] = st * e_exp[CHUNK - 1:CHUNK, gc] + _dot_tn(bg_b, xw_b[:, gc])
        dyn = [jnp.where(on_diag, jnp.concatenate([t[g * HEADS_PER_GROUP:(g + 1) * HEADS_PER_GROUP, :]]
                                                  * HEADS_PER_GROUP, axis=1), 0.0) for t in bt_terms]
        dyn.append(jnp.zeros_like(dyn[0]))
        v3 = jnp.concatenate([selc_ref[g], jnp.concatenate(dyn, axis=0).astype(BF16)], axis=0)
        seg_ref[g % 2] = _dot(u3, v3)

    stage(0)
    for g in range(SSM_GROUPS):
        if g + 1 < SSM_GROUPS:
            stage(g + 1)
        gc = slice(g * gw, (g + 1) * gw)
        cb_mat, y_off, seg = cb_ref[g % 2], yoff_ref[g % 2], seg_ref[g % 2]
        m = (jnp.exp(seg + cbias) * jnp.concatenate([cb_mat] * HEADS_PER_GROUP, axis=1)).astype(BF16)
        y_parts = []
        for pr in range(HEADS_PER_GROUP // 2):
            c0 = g * gw + pr * LANES
            xp = xbc_ref[:, c0:c0 + LANES]
            zero = jnp.zeros_like(xp)
            rhs = jnp.concatenate([jnp.where(lane_lo, xp, zero), jnp.where(lane_lo, zero, xp)], axis=0)
            y_parts.append(_dot(m[:, pr * 2 * CHUNK:(pr + 1) * 2 * CHUNK], rhs))
        y = jnp.concatenate(y_parts, axis=1) + y_off * e_exp[:, gc] + dskip_ref[:, gc] * xs_f[:, gc]
        y = y * zs_ref[:, gc].astype(F32)
        ssq = ssq + jnp.sum(y * y, axis=-1, keepdims=True)
        yg_ref[:, gc] = y
    inv = lax.rsqrt(ssq * (1.0 / D_INNER) + RMS_EPS)
    o_ref[...] = (yg_ref[...] * inv * gn_ref[...]).astype(BF16)


def _ssd_constants():
    tril3 = np.tile(np.tril(np.ones((CHUNK, CHUNK), np.float32)), (1, 3))
    e3 = np.zeros((LANES, D_INNER), np.float32)
    selc = np.zeros((SSM_GROUPS, 3 * REP, HEADS_PER_GROUP * CHUNK), np.float32)
    for t in range(3):
        for h in range(SSM_HEADS):
            e3[REP * t + h, h * SSM_HEAD_DIM:(h + 1) * SSM_HEAD_DIM] = 1.0
            g, k = divmod(h, HEADS_PER_GROUP)
            selc[g, REP * t + h, k * CHUNK:(k + 1) * CHUNK] = 1.0
    return jnp.asarray(tril3, BF16), jnp.asarray(e3, BF16), jnp.asarray(selc, BF16)


def _ssd(zs, xbc, dt, a_log, d_skip, gate_norm, state_init, bsz, nc, head):
    rows = zs.shape[0]
    tril3, e3, selc = _ssd_constants()
    cps = _chunks_per_step(nc)
    steps = nc // cps
    row_spec = lambda n: pl.BlockSpec((cps * CHUNK, n), lambda b, c: (b * steps + c, 0))
    out_specs = [row_spec(D_INNER)]
    out_shape = [jax.ShapeDtypeStruct((rows, D_INNER), BF16)]
    if head:
        out_specs.append(_resident((D_STATE, D_INNER)))
        out_shape.append(jax.ShapeDtypeStruct((D_STATE, D_INNER), F32))
    return pl.pallas_call(
        functools.partial(_ssd_kernel, head=head, cps=cps),
        grid=(bsz, steps),
        in_specs=[row_spec(D_INNER), row_spec(D_XBC), row_spec(LANES),
                  _resident(a_log.shape), _resident(d_skip.shape), _resident(gate_norm.shape),
                  _resident(tril3.shape), _resident(e3.shape), _resident(selc.shape),
                  _resident(state_init.shape)],
        out_specs=out_specs,
        out_shape=out_shape,
        scratch_shapes=[pltpu.VMEM((D_STATE, D_INNER), F32),
                        pltpu.VMEM((cps * CHUNK, D_INNER), F32),
                        pltpu.VMEM((2, CHUNK, CHUNK), F32),
                        pltpu.VMEM((2, CHUNK, HEADS_PER_GROUP * SSM_HEAD_DIM), F32),
                        pltpu.VMEM((2, CHUNK, HEADS_PER_GROUP * CHUNK), F32)],
        compiler_params=_params(2),
        name="mamba_ssd",
    )(zs, xbc, dt, a_log, d_skip, gate_norm, tril3, e3, selc, state_init)


def _ffn_kernel(*refs, tiles_per_seq, tm, tn, head, mix_transposed):
    (res_ref, mix_ref, wmix_ref, gmix_ref, gpre_ref, wg_ref, wv_ref, cw_ref, cb_ref, wd_ref, gpost_ref,
     init_ref, o_ref) = refs[:13]
    tail_ref = refs[13] if head else None
    ext_ref, carry_ref, act_ref = refs[-3:]
    first = (pl.program_id(0) % tiles_per_seq) == 0
    part = tm // 2 if (tm // 2) % LANES == 0 else tm
    mixes = [_dot_tn(mix_ref[:, r0:r0 + part], wmix_ref[...]) if mix_transposed
             else _dot(mix_ref[r0:r0 + part, :], wmix_ref[...]) for r0 in range(0, tm, part)]
    x = jnp.concatenate([res_ref[r0:r0 + part, :] + _rms(m) * gmix_ref[...]
                         for r0, m in zip(range(0, tm, part), mixes)], axis=0)
    xn = _normed_input(x, gpre_ref, head)

    def up_conv(w_ref, col):
        u = _dot(xn, w_ref[:, col % D_FF:col % D_FF + tn])
        _conv_stage(u, first, init_ref, ext_ref, carry_ref, tail_ref, col)
        return _conv_finish(ext_ref, cw_ref, cb_ref, col, FFN_CONV, u)

    for j in range(D_FF // tn):
        gate = up_conv(wg_ref, j * tn)
        val = up_conv(wv_ref, D_FF + j * tn)
        act_ref[:, j * tn:(j + 1) * tn] = (_silu(gate) * val).astype(BF16)
    ys = [_dot(act_ref[r0:r0 + part, :], wd_ref[...]) for r0 in range(0, tm, part)]
    for r0, y in zip(range(0, tm, part), ys):
        o_ref[r0:r0 + part, :] = x[r0:r0 + part, :] + _rms(y) * gpost_ref[...]


def _ffn(res, mix, wmix, gmix, gpre, w_up, w_down, layer, conv_w, conv_b, gpost, conv_init, seq_rows, head,
         mix_transposed):
    rows = res.shape[0]
    k = wmix.shape[0]
    tm, tn = _row_tile(head), FFN_TN
    assert seq_rows % tm == 0 and D_FF % tn == 0
    row_spec = pl.BlockSpec((tm, D_MODEL), lambda i: (i, 0))
    mix_spec = pl.BlockSpec((k, tm), lambda i: (0, i)) if mix_transposed else pl.BlockSpec((tm, k), lambda i: (i, 0))
    up_half = lambda half: pl.BlockSpec((None, D_MODEL, D_FF), lambda i: (layer, 0, half), pipeline_mode=pl.Buffered(1))
    down_spec = pl.BlockSpec((None, D_FF, D_MODEL), lambda i: (layer, 0, 0), pipeline_mode=pl.Buffered(1))
    out_specs = [row_spec]
    out_shape = [jax.ShapeDtypeStruct((rows, D_MODEL), F32)]
    if head:
        out_specs.append(_resident((HALO, 2 * D_FF)))
        out_shape.append(jax.ShapeDtypeStruct((HALO, 2 * D_FF), F32))
    return pl.pallas_call(
        functools.partial(_ffn_kernel, tiles_per_seq=seq_rows // tm, tm=tm, tn=tn, head=head,
                          mix_transposed=mix_transposed),
        grid=(rows // tm,),
        in_specs=[row_spec, mix_spec, _resident(wmix.shape), _resident(gmix.shape), _resident(gpre.shape),
                  up_half(0), up_half(1), _resident(conv_w.shape), _resident(conv_b.shape), down_spec,
                  _resident(gpost.shape), _resident(conv_init.shape)],
        out_specs=out_specs,
        out_shape=out_shape,
        scratch_shapes=[pltpu.VMEM((HALO + tm, tn), F32),
                        pltpu.VMEM((HALO, 2 * D_FF), F32),
                        pltpu.VMEM((tm, D_FF), BF16)],
        compiler_params=_params(1),
        name="mix_proj_conv_ffn",
    )(res, mix, wmix, gmix, gpre, w_up, w_up, conv_w, conv_b, w_down, gpost, conv_init)


def _qkv_proj_kernel(x_ref, gq_ref, gkv_ref, wq_ref, wk_ref, wv_ref, qt_ref, k_ref, vt_ref):
    xr = _rms(x_ref[...])
    xq = (xr * gq_ref[...]).astype(BF16)
    xkv = (xr * gkv_ref[...]).astype(BF16)
    scale = 1.0 / math.sqrt(ATTN_HEAD_DIM)
    qt_ref[...] = (_dot(xq, wq_ref[...]) * scale).T.astype(BF16)
    k_ref[...] = _dot(xkv, wk_ref[...]).astype(BF16)
    vt_ref[...] = _dot(xkv, wv_ref[...]).T.astype(BF16)


def _qkv_proj(h, gq, gkv, wq, wkv, head):
    rows = h.shape[0]
    tm = _row_tile(head)
    return pl.pallas_call(
        _qkv_proj_kernel,
        grid=(rows // tm,),
        in_specs=[pl.BlockSpec((tm, D_MODEL), lambda i: (i, 0)), _resident(gq.shape), _resident(gkv.shape),
                  _resident(wq.shape), pl.BlockSpec((D_MODEL, D_KV), lambda i: (0, 0), pipeline_mode=pl.Buffered(1)),
                  pl.BlockSpec((D_MODEL, D_KV), lambda i: (0, 1), pipeline_mode=pl.Buffered(1))],
        out_specs=[pl.BlockSpec((D_ATTN, tm), lambda i: (0, i)), pl.BlockSpec((tm, D_KV), lambda i: (i, 0)),
                   pl.BlockSpec((D_KV, tm), lambda i: (0, i))],
        out_shape=[jax.ShapeDtypeStruct((D_ATTN, rows), BF16), jax.ShapeDtypeStruct((rows, D_KV), BF16),
                   jax.ShapeDtypeStruct((D_KV, rows), BF16)],
        compiler_params=_params(1),
        name="qkv_proj",
    )(h, gq, gkv, wq, wkv, wkv)


def _attn_kernel(sink_ref, qt_ref, kc_ref, kp_ref, km_ref, vc_ref, vp_ref, vm_ref, ot_ref, s_ref, *, head, cps):
    def views(sub):
        rows = pl.ds(sub * CHUNK, CHUNK)
        if sub == 0:
            return rows, kp_ref, vp_ref, pl.program_id(1) >= 1
        before = pl.ds((sub - 1) * CHUNK, CHUNK)
        return rows, kc_ref.at[before], vc_ref.at[:, before], True

    def scores(sub):
        rows, k_prev, _, _ = views(sub)
        _attn_scores(qt_ref.at[:, rows], kc_ref.at[rows], k_prev, km_ref, s_ref.at[sub % 2])

    def finish(sub):
        rows, _, v_prev, has_prev = views(sub)
        _attn_finish(sink_ref, s_ref.at[sub % 2], vc_ref.at[:, rows], v_prev, vm_ref, ot_ref.at[:, rows], has_prev, head)

    scores(0)
    for sub in range(cps):
        if sub + 1 < cps:
            scores(sub + 1)
        finish(sub)


def _attn_scores(qt_ref, kc_ref, kp_ref, km_ref, s_ref):
    k_all = jnp.concatenate([km_ref[PAD:CHUNK, :], kp_ref[...], kc_ref[...]], axis=0)
    qw = Q_PER_KV * CHUNK
    zeros = jnp.zeros((ATTN_HEAD_DIM, qw), BF16)

    def q_heads(kh):
        return jnp.concatenate([qt_ref[hq * ATTN_HEAD_DIM:(hq + 1) * ATTN_HEAD_DIM, :]
                                for hq in range(kh * Q_PER_KV, (kh + 1) * Q_PER_KV)], axis=1)

    for pair in range(N_KV_HEADS // 2):
        qbd = jnp.concatenate([jnp.concatenate([q_heads(2 * pair), zeros], axis=1),
                               jnp.concatenate([zeros, q_heads(2 * pair + 1)], axis=1)], axis=0)
        s_ref[:, pair * 2 * qw:(pair + 1) * 2 * qw] = _dot(k_all[:, pair * LANES:(pair + 1) * LANES], qbd)


def _attn_finish(sink_ref, s_ref, vc_ref, vp_ref, vm_ref, ot_ref, has_prev, head):
    nkeys = N_META + 2 * CHUNK
    ki = lax.broadcasted_iota(jnp.int32, (nkeys, CHUNK), 0)
    qi = lax.broadcasted_iota(jnp.int32, (nkeys, CHUNK), 1)
    kprev = ki - N_META
    kcur = ki - (N_META + CHUNK)
    in_cur = jnp.logical_and(kcur >= 0, kcur <= qi)
    if head:
        visible = jnp.logical_and(in_cur, kcur >= PAD)
    else:
        in_prev = jnp.logical_and(jnp.logical_and(kprev >= 0, kprev < CHUNK), jnp.logical_and(kprev > qi, has_prev))
        visible = jnp.logical_or(jnp.logical_or(ki < N_META, in_prev), in_cur)
    bias = jnp.concatenate([jnp.where(visible, 0.0, NEG_INF)] * N_Q_HEADS, axis=1)
    qw = Q_PER_KV * CHUNK
    s = s_ref[...] + bias
    sink = jnp.concatenate([jnp.full((1, CHUNK), sink_ref[hq], F32) for hq in range(N_Q_HEADS)], axis=1)
    m = jnp.maximum(jnp.max(s, axis=0, keepdims=True), sink)
    p = jnp.exp(s - m)
    inv_den = 1.0 / (jnp.sum(p, axis=0, keepdims=True) + jnp.exp(sink - m))
    pb = p.astype(BF16)
    pad_rows = jnp.zeros((PAD, qw), BF16)
    for kh in range(N_KV_HEADS):
        rows = slice(kh * ATTN_HEAD_DIM, (kh + 1) * ATTN_HEAD_DIM)
        vt = jnp.concatenate([vm_ref[rows, :], vp_ref[rows, :], vc_ref[rows, :]], axis=1)
        p_all = jnp.concatenate([pad_rows, pb[:, kh * qw:(kh + 1) * qw]], axis=0)
        o = _dot(vt, p_all) * inv_den[:, kh * qw:(kh + 1) * qw]
        for g in range(Q_PER_KV):
            hq = kh * Q_PER_KV + g
            ot_ref[hq * ATTN_HEAD_DIM:(hq + 1) * ATTN_HEAD_DIM, :] = o[:, g * CHUNK:(g + 1) * CHUNK].astype(BF16)


def _attn(qt, k, vt, k_head, vt_head, sinks, bsz, nc, head):
    rows = qt.shape[1]
    cps = _chunks_per_step(nc)
    steps = nc // cps
    cur = lambda b, c: b * steps + c
    prev = lambda b, c: b * nc + jnp.maximum(c * cps - 1, 0)
    return pl.pallas_call(
        functools.partial(_attn_kernel, head=head, cps=cps),
        grid=(bsz, steps),
        in_specs=[pl.BlockSpec(memory_space=pltpu.SMEM),
                  pl.BlockSpec((D_ATTN, cps * CHUNK), lambda b, c: (0, cur(b, c))),
                  pl.BlockSpec((cps * CHUNK, D_KV), lambda b, c: (cur(b, c), 0)),
                  pl.BlockSpec((CHUNK, D_KV), lambda b, c: (prev(b, c), 0)), _resident(k_head.shape),
                  pl.BlockSpec((D_KV, cps * CHUNK), lambda b, c: (0, cur(b, c))),
                  pl.BlockSpec((D_KV, CHUNK), lambda b, c: (0, prev(b, c))), _resident(vt_head.shape)],
        out_specs=pl.BlockSpec((D_ATTN, cps * CHUNK), lambda b, c: (0, cur(b, c))),
        out_shape=jax.ShapeDtypeStruct((D_ATTN, rows), BF16),
        scratch_shapes=[pltpu.VMEM((2, N_META + 2 * CHUNK, N_Q_HEADS * CHUNK), F32)],
        compiler_params=_params(2),
        name="swa_sink_attention",
    )(sinks, qt, k, k, k_head, vt, vt, vt_head)


def kernel(x, meta_tokens, a_norm_pre, a_w_in, a_conv_w, a_conv_b, a_dt_bias, a_a_log, a_d_skip, a_gate_norm, a_w_out, a_norm_post, kv_norm, w_kv, b_norm_pre, b_w_q, b_sinks, b_w_o, b_norm_post, f_norm_pre, f_w_up, f_conv_w, f_conv_b, f_w_down, f_norm_post):
    bsz, seq, _ = x.shape
    assert seq % BODY_TM == 0
    depth = f_norm_pre.shape[0]
    n_a = a_norm_pre.shape[0]

    def row(v):
        return v.astype(F32).reshape(1, -1)

    def rep3(v):
        return jnp.pad(jnp.concatenate([v] * 3, axis=1), ((0, 0), (0, LANES - 3 * REP)))

    passes = {True: (1, CHUNK, 1), False: (bsz, seq, seq // CHUNK)}
    hs = {True: jnp.concatenate([jnp.zeros((PAD, D_MODEL), F32), meta_tokens.astype(F32)], axis=0),
          False: x.astype(F32).reshape(bsz * seq, D_MODEL)}

    w_in_b, w_up_b, w_down_b = a_w_in.astype(BF16), f_w_up.astype(BF16), f_w_down.astype(BF16)
    kv = {}
    for i in range(depth):
        mixes = {}
        if i < n_a:
            j = i
            wdt = rep3(a_w_in[j][:, D_INNER + D_XBC:]).astype(BF16)
            w_mix, g_mix, mix_transposed = a_w_out[j].astype(BF16), row(a_norm_post[j]), False
            conv_init = jnp.zeros((HALO, D_XBC), F32)
            state_init = jnp.zeros((D_STATE, D_INNER), F32)
            for head in (True, False):
                nseq, seq_rows, nc = passes[head]
                outs = _in_proj(hs[head], row(a_norm_pre[j]), w_in_b, j, wdt, a_conv_w[j].astype(F32),
                                row(a_conv_b[j]), rep3(row(a_dt_bias[j])), conv_init, seq_rows, head)
                ys = _ssd(outs[0], outs[1], outs[2], rep3(row(a_a_log[j])),
                          row(jnp.repeat(a_d_skip[j], SSM_HEAD_DIM)), row(a_gate_norm[j]), state_init, nseq, nc, head)
                mixes[head] = ys[0]
                if head:
                    conv_init, state_init = outs[3], ys[1]
        else:
            j = i - n_a
            wq, wkv = b_w_q[j].astype(BF16), w_kv.astype(BF16)
            w_mix, g_mix, mix_transposed = b_w_o[j].astype(BF16), row(b_norm_post[j]), True
            for head in (True, False):
                nseq, seq_rows, nc = passes[head]
                qt, k_new, vt_new = _qkv_proj(hs[head], row(b_norm_pre[j]), row(kv_norm), wq, wkv, head)
                if j == 0:
                    kv[head] = (k_new, vt_new)
                mixes[head] = _attn(qt, kv[head][0], kv[head][1], kv[True][0], kv[True][1],
                                    b_sinks[j].astype(F32), nseq, nc, head)
        conv_init = jnp.zeros((HALO, 2 * D_FF), F32)
        for head in (True, False):
            nseq, seq_rows, nc = passes[head]
            outs = _ffn(hs[head], mixes[head], w_mix, g_mix, row(f_norm_pre[i]), w_up_b, w_down_b, i,
                        f_conv_w[i].astype(F32), row(f_conv_b[i]), row(f_norm_post[i]), conv_init, seq_rows, head,
                        mix_transposed)
            hs[head] = outs[0]
            if head:
                conv_init = outs[1]
    return hs[False].reshape(bsz, seq, D_MODEL)
```

```python
import functools
import math

import jax
import jax.numpy as jnp
import numpy as np
from jax import lax
from jax.experimental import pallas as pl
from jax.experimental.pallas import tpu as pltpu

F32 = jnp.float32
BF16 = jnp.bfloat16

D_MODEL = 1024
N_META = 16
CHUNK = 128
PAD = CHUNK - N_META

D_INNER = 2048
SSM_HEAD_DIM = 64
SSM_HEADS = 32
SSM_GROUPS = 4
HEADS_PER_GROUP = SSM_HEADS // SSM_GROUPS
D_STATE = 128
SSM_CONV = 4
D_BC = SSM_GROUPS * D_STATE
D_XBC = D_INNER + 2 * D_BC

ATTN_HEAD_DIM = 64
N_Q_HEADS = 16
N_KV_HEADS = 4
Q_PER_KV = N_Q_HEADS // N_KV_HEADS
D_ATTN = N_Q_HEADS * ATTN_HEAD_DIM
D_KV = N_KV_HEADS * ATTN_HEAD_DIM

D_FF = 2816
FFN_CONV = 3

RMS_EPS = 1e-6
NEG_INF = -1e30

LANES = 128
HALO = 8
REP = SSM_HEADS
assert 3 * REP <= LANES
VMEM_LIMIT = 56 * 1024 * 1024
BODY_TM = 4 * CHUNK
CHUNKS_PER_STEP = 4
IN_PROJ_TM = 8 * CHUNK
IN_PROJ_TN = 512
XBC_BLOCK = 1024
XBC_BLOCKS = D_XBC // XBC_BLOCK
assert D_INNER % XBC_BLOCK == 0 and D_XBC % XBC_BLOCK == 0
FFN_TN = 256


def _rms(x):
    return x * lax.rsqrt(jnp.mean(x * x, axis=-1, keepdims=True) + RMS_EPS)


def _silu(x):
    return x * jax.nn.sigmoid(x)


def _dot(a, b):
    return jnp.dot(a, b, preferred_element_type=F32)


def _dot_nt(a, b):
    return lax.dot_general(a, b, (((1,), (1,)), ((), ())), preferred_element_type=F32)


def _dot_tn(a, b):
    return lax.dot_general(a, b, (((0,), (0,)), ((), ())), preferred_element_type=F32)


def _resident(shape):
    return pl.BlockSpec(shape, lambda *_: (0,) * len(shape), pipeline_mode=pl.Buffered(1))


def _params(n_axes):
    return pltpu.CompilerParams(dimension_semantics=("arbitrary",) * n_axes,
                                vmem_limit_bytes=VMEM_LIMIT)


def _row_tile(head):
    return CHUNK if head else BODY_TM


def _chunks_per_step(nc):
    return CHUNKS_PER_STEP if nc % CHUNKS_PER_STEP == 0 else 1


def _conv_stage(u, first, init_ref, buf, carry_ref, tail_ref, col):
    tm, tn = u.shape
    cols = slice(col, col + tn)
    buf[0:HALO, :] = jnp.where(first, init_ref[:, cols], carry_ref[:, cols])
    buf[HALO:HALO + tm, :] = u
    carry_ref[:, cols] = u[tm - HALO:tm, :]
    if tail_ref is not None:
        tail_ref[:, cols] = u[tm - HALO:tm, :]


def _conv_finish(buf, cw_ref, cb_ref, col, taps, u=None):
    tm, tn = buf.shape[0] - HALO, buf.shape[1]
    cols = slice(col, col + tn)
    out = cb_ref[:, cols] + cw_ref[taps - 1:taps, cols] * (buf[HALO:HALO + tm, :] if u is None else u)
    for k in range(taps - 1):
        s = taps - 1 - k
        out = out + cw_ref[k:k + 1, cols] * buf[HALO - s:HALO - s + tm, :]
    return out


def _normed_input(x, gain_ref, head):
    xn = _rms(x) * gain_ref[...]
    if head:
        row = lax.broadcasted_iota(jnp.int32, (x.shape[0], 1), 0)
        xn = jnp.where(row >= PAD, xn, 0.0)
    return xn.astype(BF16)


def _in_proj_kernel(*refs, tiles_per_seq, tm, tn, head):
    x_ref, g_ref, wz_ref = refs[:3]
    wx_refs = refs[3:3 + XBC_BLOCKS]
    wdt_ref, cw_ref, cb_ref, dtb_ref, init_ref, zs_ref, xbc_ref, dt_ref = refs[3 + XBC_BLOCKS:11 + XBC_BLOCKS]
    tail_ref = refs[11 + XBC_BLOCKS] if head else None
    ext_ref, carry_ref = refs[-2:]
    first = (pl.program_id(0) % tiles_per_seq) == 0
    xn = _normed_input(x_ref[...], g_ref, head)
    valid = lax.broadcasted_iota(jnp.int32, (tm, 1), 0) >= PAD
    def z_block(n0):
        zs_ref[:, n0:n0 + tn] = _silu(_dot(xn, wz_ref[:, n0:n0 + tn])).astype(BF16)

    def stage(n0):
        w_ref, w0 = wx_refs[n0 // XBC_BLOCK], n0 % XBC_BLOCK
        _conv_stage(_dot(xn, w_ref[:, w0:w0 + tn]), first, init_ref, ext_ref.at[(n0 // tn) % 2], carry_ref,
                    tail_ref, n0)

    def finish(n0):
        y = _silu(_conv_finish(ext_ref.at[(n0 // tn) % 2], cw_ref, cb_ref, n0, SSM_CONV))
        if head and n0 >= D_INNER:
            y = jnp.where(valid, y, 0.0)
        xbc_ref[:, n0:n0 + tn] = y.astype(BF16)

    z_cols, x_cols = list(range(0, D_INNER, tn)), list(range(0, D_XBC, tn))
    stage(x_cols[0])
    for i, n0 in enumerate(x_cols):
        if i + 1 < len(x_cols):
            stage(x_cols[i + 1])
        finish(n0)
        if z_cols:
            z_block(z_cols.pop(0))
    for n0 in z_cols:
        z_block(n0)
    dt_raw = _dot(xn, wdt_ref[...]) + dtb_ref[...]
    dt = jnp.maximum(dt_raw, 0.0) + jnp.log1p(jnp.exp(-jnp.abs(dt_raw)))
    dt_ref[...] = jnp.where(valid, dt, 0.0) if head else dt


def _in_proj(h, gain, w_in, layer, wdt, conv_w, conv_b, dt_bias, conv_init, seq_rows, head):
    rows = h.shape[0]
    tm, tn = (CHUNK if head else IN_PROJ_TM), IN_PROJ_TN
    assert seq_rows % tm == 0 and D_INNER % tn == 0 and XBC_BLOCK % tn == 0
    row_spec = lambda n: pl.BlockSpec((tm, n), lambda i: (i, 0))
    w_block = lambda width, idx: pl.BlockSpec((None, D_MODEL, width), lambda i: (layer, 0, idx),
                                              pipeline_mode=pl.Buffered(1))
    out_specs = [row_spec(D_INNER), row_spec(D_XBC), row_spec(LANES)]
    out_shape = [jax.ShapeDtypeStruct((rows, D_INNER), BF16), jax.ShapeDtypeStruct((rows, D_XBC), BF16),
                 jax.ShapeDtypeStruct((rows, LANES), F32)]
    if head:
        out_specs.append(_resident((HALO, D_XBC)))
        out_shape.append(jax.ShapeDtypeStruct((HALO, D_XBC), F32))
    return pl.pallas_call(
        functools.partial(_in_proj_kernel, tiles_per_seq=seq_rows // tm, tm=tm, tn=tn, head=head),
        grid=(rows // tm,),
        in_specs=[row_spec(D_MODEL), _resident((1, D_MODEL)), w_block(D_INNER, 0)]
                 + [w_block(XBC_BLOCK, D_INNER // XBC_BLOCK + k) for k in range(XBC_BLOCKS)]
                 + [_resident(wdt.shape), _resident(conv_w.shape), _resident(conv_b.shape),
                    _resident(dt_bias.shape), _resident(conv_init.shape)],
        out_specs=out_specs,
        out_shape=out_shape,
        scratch_shapes=[pltpu.VMEM((2, HALO + tm, tn), F32), pltpu.VMEM((HALO, D_XBC), F32)],
        compiler_params=_params(1),
        name="mamba_in_proj",
    )(h, gain, w_in, *([w_in] * XBC_BLOCKS), wdt, conv_w, conv_b, dt_bias, conv_init)


def _split3(x):
    hi = x.astype(BF16).astype(F32)
    r = x - hi
    mid = r.astype(BF16).astype(F32)
    return hi, mid, r - mid


def _pack3(x, lane, fill):
    hi, mid, lo = _split3(x)
    packed = jnp.where(lane < REP, hi, jnp.where(lane < 2 * REP, mid, jnp.where(lane < 3 * REP, lo, fill)))
    return packed.astype(BF16)


def _ssd_kernel(*refs, head, cps):
    (zs_ref, xbc_ref, dt_ref, alog_ref, dskip_ref, gn_ref, tril3_ref, e3_ref, selc_ref, init_ref, o_ref) = refs[:11]
    final_ref = refs[11] if head else None
    state_ref, yg_ref, cb_ref, yoff_ref, seg_ref = refs[-5:]

    @pl.when(pl.program_id(1) == 0)
    def _():
        state_ref[...] = init_ref[...]

    for sub in range(cps):
        rows = pl.ds(sub * CHUNK, CHUNK)
        _ssd_chunk(zs_ref.at[rows], xbc_ref.at[rows], dt_ref.at[rows], alog_ref, dskip_ref, gn_ref, tril3_ref,
                   e3_ref, selc_ref, o_ref.at[rows], state_ref, yg_ref.at[rows], cb_ref, yoff_ref, seg_ref)
    if head:
        final_ref[...] = state_ref[...]


def _ssd_chunk(zs_ref, xbc_ref, dt_ref, alog_ref, dskip_ref, gn_ref, tril3_ref, e3_ref, selc_ref, o_ref,
               state_ref, yg_ref, cb_ref, yoff_ref, seg_ref):
    lane = lax.broadcasted_iota(jnp.int32, (CHUNK, LANES), 1)
    li = lax.broadcasted_iota(jnp.int32, (CHUNK, CHUNK), 0)
    cbias = jnp.where(li >= lane, 0.0, NEG_INF)
    cbias = jnp.concatenate([cbias] * HEADS_PER_GROUP, axis=1)

    dt = dt_ref[...]
    a = dt * (-jnp.exp(alog_ref[...]))
    a3 = jnp.concatenate([t.astype(BF16) for t in _split3(a)], axis=0)
    acs = _dot(tril3_ref[...], a3)
    b = jnp.maximum(jnp.log(dt), NEG_INF) - acs
    expa = jnp.exp(acs)
    wst = jnp.exp(acs[CHUNK - 1:CHUNK, :] + b)
    spread = _dot(jnp.concatenate([_pack3(expa, lane, 0.0), _pack3(wst, lane, 0.0)], axis=0), e3_ref[...])
    e_exp = spread[:CHUNK]
    w_exp = spread[CHUNK:]
    u3 = _pack3(acs, lane, 1.0)
    bt_terms = _split3(b.T)
    sub = lax.broadcasted_iota(jnp.int32, (HEADS_PER_GROUP, HEADS_PER_GROUP * CHUNK), 0)
    blk = lax.broadcasted_iota(jnp.int32, (HEADS_PER_GROUP, HEADS_PER_GROUP * CHUNK), 1) // CHUNK
    on_diag = sub == blk
    lane_lo = lane < SSM_HEAD_DIM
    ssq = jnp.zeros((CHUNK, 1), F32)
    gw = HEADS_PER_GROUP * SSM_HEAD_DIM

    def stage(g):
        gc = slice(g * gw, (g + 1) * gw)
        bg_b = xbc_ref[:, D_INNER + g * D_STATE:D_INNER + (g + 1) * D_STATE]
        cg_b = xbc_ref[:, D_INNER + D_BC + g * D_STATE:D_INNER + D_BC + (g + 1) * D_STATE]
        cb_ref[g % 2] = _dot_nt(cg_b, bg_b)
        st = state_ref[:, gc]
        yoff_ref[g % 2] = _dot(cg_b, st.astype(BF16))
        xw_b = (xbc_ref[:, gc].astype(F32) * w_exp[:, gc]).astype(BF16)
        state_ref[:, gc] = st * e_exp[CHUNK - 1:CHUNK, gc] + _dot_tn(bg_b, xw_b)
        dyn = [jnp.where(on_diag, jnp.concatenate([t[g * HEADS_PER_GROUP:(g + 1) * HEADS_PER_GROUP, :]]
                                                  * HEADS_PER_GROUP, axis=1), 0.0) for t in bt_terms]
        dyn.append(jnp.zeros_like(dyn[0]))
        v3 = jnp.concatenate([selc_ref[g], jnp.concatenate(dyn, axis=0).astype(BF16)], axis=0)
        seg_ref[g % 2] = _dot(u3, v3)

    stage(0)
    for g in range(SSM_GROUPS):
        if g + 1 < SSM_GROUPS:
            stage(g + 1)
        gc = slice(g * gw, (g + 1) * gw)
        cb_mat, y_off, seg = cb_ref[g % 2], yoff_ref[g % 2], seg_ref[g % 2]
        m = (jnp.exp(seg + cbias) * jnp.concatenate([cb_mat] * HEADS_PER_GROUP, axis=1)).astype(BF16)
        y_parts = []
        for pr in range(HEADS_PER_GROUP // 2):
            c0 = g * gw + pr * LANES
            xp = xbc_ref[:, c0:c0 + LANES]
            zero = jnp.zeros_like(xp)
            rhs = jnp.concatenate([jnp.where(lane_lo, xp, zero), jnp.where(lane_lo, zero, xp)], axis=0)
            y_parts.append(_dot(m[:, pr * 2 * CHUNK:(pr + 1) * 2 * CHUNK], rhs))
        y = jnp.concatenate(y_parts, axis=1) + y_off * e_exp[:, gc] + dskip_ref[:, gc] * xbc_ref[:, gc].astype(F32)
        y = y * zs_ref[:, gc].astype(F32)
        ssq = ssq + jnp.sum(y * y, axis=-1, keepdims=True)
        yg_ref[:, gc] = y
    inv = lax.rsqrt(ssq * (1.0 / D_INNER) + RMS_EPS)
    o_ref[...] = (yg_ref[...] * inv * gn_ref[...]).astype(BF16)


def _ssd_constants():
    tril3 = np.tile(np.tril(np.ones((CHUNK, CHUNK), np.float32)), (1, 3))
    e3 = np.zeros((LANES, D_INNER), np.float32)
    selc = np.zeros((SSM_GROUPS, 3 * REP, HEADS_PER_GROUP * CHUNK), np.float32)
    for t in range(3):
        for h in range(SSM_HEADS):
            e3[REP * t + h, h * SSM_HEAD_DIM:(h + 1) * SSM_HEAD_DIM] = 1.0
            g, k = divmod(h, HEADS_PER_GROUP)
            selc[g, REP * t + h, k * CHUNK:(k + 1) * CHUNK] = 1.0
    return jnp.asarray(tril3, BF16), jnp.asarray(e3, BF16), jnp.asarray(selc, BF16)


def _ssd(zs, xbc, dt, a_log, d_skip, gate_norm, state_init, bsz, nc, head):
    rows = zs.shape[0]
    tril3, e3, selc = _ssd_constants()
    cps = _chunks_per_step(nc)
    steps = nc // cps
    row_spec = lambda n: pl.BlockSpec((cps * CHUNK, n), lambda b, c: (b * steps + c, 0))
    out_specs = [row_spec(D_INNER)]
    out_shape = [jax.ShapeDtypeStruct((rows, D_INNER), BF16)]
    if head:
        out_specs.append(_resident((D_STATE, D_INNER)))
        out_shape.append(jax.ShapeDtypeStruct((D_STATE, D_INNER), F32))
    return pl.pallas_call(
        functools.partial(_ssd_kernel, head=head, cps=cps),
        grid=(bsz, steps),
        in_specs=[row_spec(D_INNER), row_spec(D_XBC), row_spec(LANES),
                  _resident(a_log.shape), _resident(d_skip.shape), _resident(gate_norm.shape),
                  _resident(tril3.shape), _resident(e3.shape), _resident(selc.shape),
                  _resident(state_init.shape)],
        out_specs=out_specs,
        out_shape=out_shape,
        scratch_shapes=[pltpu.VMEM((D_STATE, D_INNER), F32),
                        pltpu.VMEM((cps * CHUNK, D_INNER), F32),
                        pltpu.VMEM((2, CHUNK, CHUNK), F32),
                        pltpu.VMEM((2, CHUNK, HEADS_PER_GROUP * SSM_HEAD_DIM), F32),
                        pltpu.VMEM((2, CHUNK, HEADS_PER_GROUP * CHUNK), F32)],
        compiler_params=_params(2),
        name="mamba_ssd",
    )(zs, xbc, dt, a_log, d_skip, gate_norm, tril3, e3, selc, state_init)


def _ffn_kernel(*refs, tiles_per_seq, tm, tn, head, mix_transposed):
    (res_ref, mix_ref, wmix_ref, gmix_ref, gpre_ref, wg_ref, wv_ref, cw_ref, cb_ref, wd_ref, gpost_ref,
     init_ref, o_ref) = refs[:13]
    tail_ref = refs[13] if head else None
    ext_ref, carry_ref, act_ref = refs[-3:]
    first = (pl.program_id(0) % tiles_per_seq) == 0
    part = tm // 2 if (tm // 2) % LANES == 0 else tm
    mixes = [_dot_tn(mix_ref[:, r0:r0 + part], wmix_ref[...]) if mix_transposed
             else _dot(mix_ref[r0:r0 + part, :], wmix_ref[...]) for r0 in range(0, tm, part)]
    x = jnp.concatenate([res_ref[r0:r0 + part, :] + _rms(m) * gmix_ref[...]
                         for r0, m in zip(range(0, tm, part), mixes)], axis=0)
    xn = _normed_input(x, gpre_ref, head)

    def up_conv(w_ref, col):
        u = _dot(xn, w_ref[:, col % D_FF:col % D_FF + tn])
        _conv_stage(u, first, init_ref, ext_ref, carry_ref, tail_ref, col)
        return _conv_finish(ext_ref, cw_ref, cb_ref, col, FFN_CONV, u)

    for j in range(D_FF // tn):
        gate = up_conv(wg_ref, j * tn)
        val = up_conv(wv_ref, D_FF + j * tn)
        act_ref[:, j * tn:(j + 1) * tn] = (_silu(gate) * val).astype(BF16)
    ys = [_dot(act_ref[r0:r0 + part, :], wd_ref[...]) for r0 in range(0, tm, part)]
    for r0, y in zip(range(0, tm, part), ys):
        o_ref[r0:r0 + part, :] = x[r0:r0 + part, :] + _rms(y) * gpost_ref[...]


def _ffn(res, mix, wmix, gmix, gpre, w_up, w_down, layer, conv_w, conv_b, gpost, conv_init, seq_rows, head,
         mix_transposed):
    rows = res.shape[0]
    k = wmix.shape[0]
    tm, tn = _row_tile(head), FFN_TN
    assert seq_rows % tm == 0 and D_FF % tn == 0
    row_spec = pl.BlockSpec((tm, D_MODEL), lambda i: (i, 0))
    mix_spec = pl.BlockSpec((k, tm), lambda i: (0, i)) if mix_transposed else pl.BlockSpec((tm, k), lambda i: (i, 0))
    up_half = lambda half: pl.BlockSpec((None, D_MODEL, D_FF), lambda i: (layer, 0, half), pipeline_mode=pl.Buffered(1))
    down_spec = pl.BlockSpec((None, D_FF, D_MODEL), lambda i: (layer, 0, 0), pipeline_mode=pl.Buffered(1))
    out_specs = [row_spec]
    out_shape = [jax.ShapeDtypeStruct((rows, D_MODEL), F32)]
    if head:
        out_specs.append(_resident((HALO, 2 * D_FF)))
        out_shape.append(jax.ShapeDtypeStruct((HALO, 2 * D_FF), F32))
    return pl.pallas_call(
        functools.partial(_ffn_kernel, tiles_per_seq=seq_rows // tm, tm=tm, tn=tn, head=head,
                          mix_transposed=mix_transposed),
        grid=(rows // tm,),
        in_specs=[row_spec, mix_spec, _resident(wmix.shape), _resident(gmix.shape), _resident(gpre.shape),
                  up_half(0), up_half(1), _resident(conv_w.shape), _resident(conv_b.shape), down_spec,
                  _resident(gpost.shape), _resident(conv_init.shape)],
        out_specs=out_specs,
        out_shape=out_shape,
        scratch_shapes=[pltpu.VMEM((HALO + tm, tn), F32),
                        pltpu.VMEM((HALO, 2 * D_FF), F32),
                        pltpu.VMEM((tm, D_FF), BF16)],
        compiler_params=_params(1),
        name="mix_proj_conv_ffn",
    )(res, mix, wmix, gmix, gpre, w_up, w_up, conv_w, conv_b, w_down, gpost, conv_init)


def _qkv_proj_kernel(x_ref, gq_ref, gkv_ref, wq_ref, wk_ref, wv_ref, qt_ref, k_ref, vt_ref):
    xr = _rms(x_ref[...])
    xq = (xr * gq_ref[...]).astype(BF16)
    xkv = (xr * gkv_ref[...]).astype(BF16)
    scale = 1.0 / math.sqrt(ATTN_HEAD_DIM)
    qt_ref[...] = (_dot(xq, wq_ref[...]) * scale).T.astype(BF16)
    k_ref[...] = _dot(xkv, wk_ref[...]).astype(BF16)
    vt_ref[...] = _dot(xkv, wv_ref[...]).T.astype(BF16)


def _qkv_proj(h, gq, gkv, wq, wkv, head):
    rows = h.shape[0]
    tm = _row_tile(head)
    return pl.pallas_call(
        _qkv_proj_kernel,
        grid=(rows // tm,),
        in_specs=[pl.BlockSpec((tm, D_MODEL), lambda i: (i, 0)), _resident(gq.shape), _resident(gkv.shape),
                  _resident(wq.shape), pl.BlockSpec((D_MODEL, D_KV), lambda i: (0, 0), pipeline_mode=pl.Buffered(1)),
                  pl.BlockSpec((D_MODEL, D_KV), lambda i: (0, 1), pipeline_mode=pl.Buffered(1))],
        out_specs=[pl.BlockSpec((D_ATTN, tm), lambda i: (0, i)), pl.BlockSpec((tm, D_KV), lambda i: (i, 0)),
                   pl.BlockSpec((D_KV, tm), lambda i: (0, i))],
        out_shape=[jax.ShapeDtypeStruct((D_ATTN, rows), BF16), jax.ShapeDtypeStruct((rows, D_KV), BF16),
                   jax.ShapeDtypeStruct((D_KV, rows), BF16)],
        compiler_params=_params(1),
        name="qkv_proj",
    )(h, gq, gkv, wq, wkv, wkv)


def _attn_kernel(sink_ref, qt_ref, kc_ref, kp_ref, km_ref, vc_ref, vp_ref, vm_ref, ot_ref, s_ref, *, head, cps):
    def views(sub):
        rows = pl.ds(sub * CHUNK, CHUNK)
        if sub == 0:
            return rows, kp_ref, vp_ref, pl.program_id(1) >= 1
        before = pl.ds((sub - 1) * CHUNK, CHUNK)
        return rows, kc_ref.at[before], vc_ref.at[:, before], True

    def scores(sub):
        rows, k_prev, _, _ = views(sub)
        _attn_scores(qt_ref.at[:, rows], kc_ref.at[rows], k_prev, km_ref, s_ref.at[sub % 2])

    def finish(sub):
        rows, _, v_prev, has_prev = views(sub)
        _attn_finish(sink_ref, s_ref.at[sub % 2], vc_ref.at[:, rows], v_prev, vm_ref, ot_ref.at[:, rows], has_prev, head)

    scores(0)
    for sub in range(cps):
        if sub + 1 < cps:
            scores(sub + 1)
        finish(sub)


def _attn_scores(qt_ref, kc_ref, kp_ref, km_ref, s_ref):
    k_all = jnp.concatenate([km_ref[PAD:CHUNK, :], kp_ref[...], kc_ref[...]], axis=0)
    qw = Q_PER_KV * CHUNK
    zeros = jnp.zeros((ATTN_HEAD_DIM, qw), BF16)

    def q_heads(kh):
        return jnp.concatenate([qt_ref[hq * ATTN_HEAD_DIM:(hq + 1) * ATTN_HEAD_DIM, :]
                                for hq in range(kh * Q_PER_KV, (kh + 1) * Q_PER_KV)], axis=1)

    for pair in range(N_KV_HEADS // 2):
        qbd = jnp.concatenate([jnp.concatenate([q_heads(2 * pair), zeros], axis=1),
                               jnp.concatenate([zeros, q_heads(2 * pair + 1)], axis=1)], axis=0)
        s_ref[:, pair * 2 * qw:(pair + 1) * 2 * qw] = _dot(k_all[:, pair * LANES:(pair + 1) * LANES], qbd)


def _attn_finish(sink_ref, s_ref, vc_ref, vp_ref, vm_ref, ot_ref, has_prev, head):
    nkeys = N_META + 2 * CHUNK
    ki = lax.broadcasted_iota(jnp.int32, (nkeys, CHUNK), 0)
    qi = lax.broadcasted_iota(jnp.int32, (nkeys, CHUNK), 1)
    kprev = ki - N_META
    kcur = ki - (N_META + CHUNK)
    in_cur = jnp.logical_and(kcur >= 0, kcur <= qi)
    if head:
        visible = jnp.logical_and(in_cur, kcur >= PAD)
    else:
        in_prev = jnp.logical_and(jnp.logical_and(kprev >= 0, kprev < CHUNK), jnp.logical_and(kprev > qi, has_prev))
        visible = jnp.logical_or(jnp.logical_or(ki < N_META, in_prev), in_cur)
    bias = jnp.concatenate([jnp.where(visible, 0.0, NEG_INF)] * N_Q_HEADS, axis=1)
    qw = Q_PER_KV * CHUNK
    s = s_ref[...] + bias
    sink = jnp.concatenate([jnp.full((1, CHUNK), sink_ref[hq], F32) for hq in range(N_Q_HEADS)], axis=1)
    m = jnp.maximum(jnp.max(s, axis=0, keepdims=True), sink)
    p = jnp.exp(s - m)
    inv_den = 1.0 / (jnp.sum(p, axis=0, keepdims=True) + jnp.exp(sink - m))
    pb = p.astype(BF16)
    pad_rows = jnp.zeros((PAD, qw), BF16)
    for kh in range(N_KV_HEADS):
        rows = slice(kh * ATTN_HEAD_DIM, (kh + 1) * ATTN_HEAD_DIM)
        vt = jnp.concatenate([vm_ref[rows, :], vp_ref[rows, :], vc_ref[rows, :]], axis=1)
        p_all = jnp.concatenate([pad_rows, pb[:, kh * qw:(kh + 1) * qw]], axis=0)
        o = _dot(vt, p_all) * inv_den[:, kh * qw:(kh + 1) * qw]
        for g in range(Q_PER_KV):
            hq = kh * Q_PER_KV + g
            ot_ref[hq * ATTN_HEAD_DIM:(hq + 1) * ATTN_HEAD_DIM, :] = o[:, g * CHUNK:(g + 1) * CHUNK].astype(BF16)


def _attn(qt, k, vt, k_head, vt_head, sinks, bsz, nc, head):
    rows = qt.shape[1]
    cps = _chunks_per_step(nc)
    steps = nc // cps
    cur = lambda b, c: b * steps + c
    prev = lambda b, c: b * nc + jnp.maximum(c * cps - 1, 0)
    return pl.pallas_call(
        functools.partial(_attn_kernel, head=head, cps=cps),
        grid=(bsz, steps),
        in_specs=[pl.BlockSpec(memory_space=pltpu.SMEM),
                  pl.BlockSpec((D_ATTN, cps * CHUNK), lambda b, c: (0, cur(b, c))),
                  pl.BlockSpec((cps * CHUNK, D_KV), lambda b, c: (cur(b, c), 0)),
                  pl.BlockSpec((CHUNK, D_KV), lambda b, c: (prev(b, c), 0)), _resident(k_head.shape),
                  pl.BlockSpec((D_KV, cps * CHUNK), lambda b, c: (0, cur(b, c))),
                  pl.BlockSpec((D_KV, CHUNK), lambda b, c: (0, prev(b, c))), _resident(vt_head.shape)],
        out_specs=pl.BlockSpec((D_ATTN, cps * CHUNK), lambda b, c: (0, cur(b, c))),
        out_shape=jax.ShapeDtypeStruct((D_ATTN, rows), BF16),
        scratch_shapes=[pltpu.VMEM((2, N_META + 2 * CHUNK, N_Q_HEADS * CHUNK), F32)],
        compiler_params=_params(2),
        name="swa_sink_attention",
    )(sinks, qt, k, k, k_head, vt, vt, vt_head)


def kernel(x, meta_tokens, a_norm_pre, a_w_in, a_conv_w, a_conv_b, a_dt_bias, a_a_log, a_d_skip, a_gate_norm, a_w_out, a_norm_post, kv_norm, w_kv, b_norm_pre, b_w_q, b_sinks, b_w_o, b_norm_post, f_norm_pre, f_w_up, f_conv_w, f_conv_b, f_w_down, f_norm_post):
    bsz, seq, _ = x.shape
    assert seq % BODY_TM == 0
    depth = f_norm_pre.shape[0]
    n_a = a_norm_pre.shape[0]

    def row(v):
        return v.astype(F32).reshape(1, -1)

    def rep3(v):
        return jnp.pad(jnp.concatenate([v] * 3, axis=1), ((0, 0), (0, LANES - 3 * REP)))

    passes = {True: (1, CHUNK, 1), False: (bsz, seq, seq // CHUNK)}
    hs = {True: jnp.concatenate([jnp.zeros((PAD, D_MODEL), F32), meta_tokens.astype(F32)], axis=0),
          False: x.astype(F32).reshape(bsz * seq, D_MODEL)}

    w_in_b, w_up_b, w_down_b = a_w_in.astype(BF16), f_w_up.astype(BF16), f_w_down.astype(BF16)
    kv = {}
    for i in range(depth):
        mixes = {}
        if i < n_a:
            j = i
            wdt = rep3(a_w_in[j][:, D_INNER + D_XBC:]).astype(BF16)
            w_mix, g_mix, mix_transposed = a_w_out[j].astype(BF16), row(a_norm_post[j]), False
            conv_init = jnp.zeros((HALO, D_XBC), F32)
            state_init = jnp.zeros((D_STATE, D_INNER), F32)
            for head in (True, False):
                nseq, seq_rows, nc = passes[head]
                outs = _in_proj(hs[head], row(a_norm_pre[j]), w_in_b, j, wdt, a_conv_w[j].astype(F32),
                                row(a_conv_b[j]), rep3(row(a_dt_bias[j])), conv_init, seq_rows, head)
                ys = _ssd(outs[0], outs[1], outs[2], rep3(row(a_a_log[j])),
                          row(jnp.repeat(a_d_skip[j], SSM_HEAD_DIM)), row(a_gate_norm[j]), state_init, nseq, nc, head)
                mixes[head] = ys[0]
                if head:
                    conv_init, state_init = outs[3], ys[1]
        else:
            j = i - n_a
            wq, wkv = b_w_q[j].astype(BF16), w_kv.astype(BF16)
            w_mix, g_mix, mix_transposed = b_w_o[j].astype(BF16), row(b_norm_post[j]), True
            for head in (True, False):
                nseq, seq_rows, nc = passes[head]
                qt, k_new, vt_new = _qkv_proj(hs[head], row(b_norm_pre[j]), row(kv_norm), wq, wkv, head)
                if j == 0:
                    kv[head] = (k_new, vt_new)
                mixes[head] = _attn(qt, kv[head][0], kv[head][1], kv[True][0], kv[True][1],
                                    b_sinks[j].astype(F32), nseq, nc, head)
        conv_init = jnp.zeros((HALO, 2 * D_FF), F32)
        for head in (True, False):
            nseq, seq_rows, nc = passes[head]
            outs = _ffn(hs[head], mixes[head], w_mix, g_mix, row(f_norm_pre[i]), w_up_b, w_down_b, i,
                        f_conv_w[i].astype(F32), row(f_conv_b[i]), row(f_norm_post[i]), conv_init, seq_rows, head,
                        mix_transposed)
            hs[head] = outs[0]
            if head:
                conv_init = outs[1]
    return hs[False].reshape(bsz, seq, D_MODEL)
```

```python
import functools
import math

import jax
import jax.numpy as jnp
import numpy as np
from jax import lax
from jax.experimental import pallas as pl
from jax.experimental.pallas import tpu as pltpu

F32 = jnp.float32
BF16 = jnp.bfloat16

D_MODEL = 1024
N_META = 16
CHUNK = 128
PAD = CHUNK - N_META

D_INNER = 2048
SSM_HEAD_DIM = 64
SSM_HEADS = 32
SSM_GROUPS = 4
HEADS_PER_GROUP = SSM_HEADS // SSM_GROUPS
D_STATE = 128
SSM_CONV = 4
D_BC = SSM_GROUPS * D_STATE
D_XBC = D_INNER + 2 * D_BC

ATTN_HEAD_DIM = 64
N_Q_HEADS = 16
N_KV_HEADS = 4
Q_PER_KV = N_Q_HEADS // N_KV_HEADS
D_ATTN = N_Q_HEADS * ATTN_HEAD_DIM
D_KV = N_KV_HEADS * ATTN_HEAD_DIM

D_FF = 2816
FFN_CONV = 3

RMS_EPS = 1e-6
NEG_INF = -1e30

LANES = 128
HALO = 8
REP = SSM_HEADS
assert 3 * REP <= LANES
VMEM_LIMIT = 56 * 1024 * 1024
BODY_TM = 4 * CHUNK
CHUNKS_PER_STEP = 4
PROJ_TM = 8 * CHUNK
IN_PROJ_TN = 512
XBC_BLOCK = 1024
XBC_BLOCKS = D_XBC // XBC_BLOCK
assert D_INNER % XBC_BLOCK == 0 and D_XBC % XBC_BLOCK == 0
FFN_TN = 256


def _rms(x):
    return x * lax.rsqrt(jnp.mean(x * x, axis=-1, keepdims=True) + RMS_EPS)


def _silu(x):
    return x * jax.nn.sigmoid(x)


def _dot(a, b):
    return jnp.dot(a, b, preferred_element_type=F32)


def _dot_nt(a, b):
    return lax.dot_general(a, b, (((1,), (1,)), ((), ())), preferred_element_type=F32)


def _dot_tn(a, b):
    return lax.dot_general(a, b, (((0,), (0,)), ((), ())), preferred_element_type=F32)


def _resident(shape):
    return pl.BlockSpec(shape, lambda *_: (0,) * len(shape), pipeline_mode=pl.Buffered(1))


def _params(n_axes):
    return pltpu.CompilerParams(dimension_semantics=("arbitrary",) * n_axes,
                                vmem_limit_bytes=VMEM_LIMIT)


def _row_tile(head):
    return CHUNK if head else BODY_TM


def _chunks_per_step(nc):
    return CHUNKS_PER_STEP if nc % CHUNKS_PER_STEP == 0 else 1


def _conv_stage(u, first, init_ref, buf, carry_ref, tail_ref, col):
    tm, tn = u.shape
    cols = slice(col, col + tn)
    buf[0:HALO, :] = jnp.where(first, init_ref[:, cols], carry_ref[:, cols])
    buf[HALO:HALO + tm, :] = u
    carry_ref[:, cols] = u[tm - HALO:tm, :]
    if tail_ref is not None:
        tail_ref[:, cols] = u[tm - HALO:tm, :]


def _conv_finish(buf, cw_ref, cb_ref, col, taps, u=None):
    tm, tn = buf.shape[0] - HALO, buf.shape[1]
    cols = slice(col, col + tn)
    out = cb_ref[:, cols] + cw_ref[taps - 1:taps, cols] * (buf[HALO:HALO + tm, :] if u is None else u)
    for k in range(taps - 1):
        s = taps - 1 - k
        out = out + cw_ref[k:k + 1, cols] * buf[HALO - s:HALO - s + tm, :]
    return out


def _normed_input(x, gain_ref, head):
    xn = _rms(x) * gain_ref[...]
    if head:
        row = lax.broadcasted_iota(jnp.int32, (x.shape[0], 1), 0)
        xn = jnp.where(row >= PAD, xn, 0.0)
    return xn.astype(BF16)


def _in_proj_kernel(*refs, tiles_per_seq, tm, tn, head):
    x_ref, g_ref, wz_ref = refs[:3]
    wx_refs = refs[3:3 + XBC_BLOCKS]
    wdt_ref, cw_ref, cb_ref, dtb_ref, init_ref, zs_ref, xbc_ref, dt_ref = refs[3 + XBC_BLOCKS:11 + XBC_BLOCKS]
    tail_ref = refs[11 + XBC_BLOCKS] if head else None
    ext_ref, carry_ref = refs[-2:]
    first = (pl.program_id(0) % tiles_per_seq) == 0
    xn = _normed_input(x_ref[...], g_ref, head)
    valid = lax.broadcasted_iota(jnp.int32, (tm, 1), 0) >= PAD
    def z_block(n0):
        zs_ref[:, n0:n0 + tn] = _silu(_dot(xn, wz_ref[:, n0:n0 + tn])).astype(BF16)

    def stage(n0):
        w_ref, w0 = wx_refs[n0 // XBC_BLOCK], n0 % XBC_BLOCK
        _conv_stage(_dot(xn, w_ref[:, w0:w0 + tn]), first, init_ref, ext_ref.at[(n0 // tn) % 2], carry_ref,
                    tail_ref, n0)

    def finish(n0):
        y = _silu(_conv_finish(ext_ref.at[(n0 // tn) % 2], cw_ref, cb_ref, n0, SSM_CONV))
        if head and n0 >= D_INNER:
            y = jnp.where(valid, y, 0.0)
        xbc_ref[:, n0:n0 + tn] = y.astype(BF16)

    z_cols, x_cols = list(range(0, D_INNER, tn)), list(range(0, D_XBC, tn))
    stage(x_cols[0])
    for i, n0 in enumerate(x_cols):
        if i + 1 < len(x_cols):
            stage(x_cols[i + 1])
        finish(n0)
        if z_cols:
            z_block(z_cols.pop(0))
    for n0 in z_cols:
        z_block(n0)
    dt_raw = _dot(xn, wdt_ref[...]) + dtb_ref[...]
    dt = jnp.maximum(dt_raw, 0.0) + jnp.log1p(jnp.exp(-jnp.abs(dt_raw)))
    dt_ref[...] = jnp.where(valid, dt, 0.0) if head else dt


def _in_proj(h, gain, w_in, layer, wdt, conv_w, conv_b, dt_bias, conv_init, seq_rows, head):
    rows = h.shape[0]
    tm, tn = (CHUNK if head else PROJ_TM), IN_PROJ_TN
    assert seq_rows % tm == 0 and D_INNER % tn == 0 and XBC_BLOCK % tn == 0
    row_spec = lambda n: pl.BlockSpec((tm, n), lambda i: (i, 0))
    w_block = lambda width, idx: pl.BlockSpec((None, D_MODEL, width), lambda i: (layer, 0, idx),
                                              pipeline_mode=pl.Buffered(1))
    out_specs = [row_spec(D_INNER), row_spec(D_XBC), row_spec(LANES)]
    out_shape = [jax.ShapeDtypeStruct((rows, D_INNER), BF16), jax.ShapeDtypeStruct((rows, D_XBC), BF16),
                 jax.ShapeDtypeStruct((rows, LANES), F32)]
    if head:
        out_specs.append(_resident((HALO, D_XBC)))
        out_shape.append(jax.ShapeDtypeStruct((HALO, D_XBC), F32))
    return pl.pallas_call(
        functools.partial(_in_proj_kernel, tiles_per_seq=seq_rows // tm, tm=tm, tn=tn, head=head),
        grid=(rows // tm,),
        in_specs=[row_spec(D_MODEL), _resident((1, D_MODEL)), w_block(D_INNER, 0)]
                 + [w_block(XBC_BLOCK, D_INNER // XBC_BLOCK + k) for k in range(XBC_BLOCKS)]
                 + [_resident(wdt.shape), _resident(conv_w.shape), _resident(conv_b.shape),
                    _resident(dt_bias.shape), _resident(conv_init.shape)],
        out_specs=out_specs,
        out_shape=out_shape,
        scratch_shapes=[pltpu.VMEM((2, HALO + tm, tn), F32), pltpu.VMEM((HALO, D_XBC), F32)],
        compiler_params=_params(1),
        name="mamba_in_proj",
    )(h, gain, w_in, *([w_in] * XBC_BLOCKS), wdt, conv_w, conv_b, dt_bias, conv_init)


def _split3(x):
    hi = x.astype(BF16).astype(F32)
    r = x - hi
    mid = r.astype(BF16).astype(F32)
    return hi, mid, r - mid


def _pack3(x, lane, fill):
    hi, mid, lo = _split3(x)
    packed = jnp.where(lane < REP, hi, jnp.where(lane < 2 * REP, mid, jnp.where(lane < 3 * REP, lo, fill)))
    return packed.astype(BF16)


def _ssd_kernel(*refs, head, cps):
    (zs_ref, xbc_ref, dt_ref, alog_ref, dskip_ref, gn_ref, tril3_ref, e3_ref, selc_ref, init_ref, o_ref) = refs[:11]
    final_ref = refs[11] if head else None
    state_ref, yg_ref, cb_ref, yoff_ref, seg_ref = refs[-5:]

    @pl.when(pl.program_id(1) == 0)
    def _():
        state_ref[...] = init_ref[...]

    for sub in range(cps):
        rows = pl.ds(sub * CHUNK, CHUNK)
        _ssd_chunk(zs_ref.at[rows], xbc_ref.at[rows], dt_ref.at[rows], alog_ref, dskip_ref, gn_ref, tril3_ref,
                   e3_ref, selc_ref, o_ref.at[rows], state_ref, yg_ref.at[rows], cb_ref, yoff_ref, seg_ref)
    if head:
        final_ref[...] = state_ref[...]


def _ssd_chunk(zs_ref, xbc_ref, dt_ref, alog_ref, dskip_ref, gn_ref, tril3_ref, e3_ref, selc_ref, o_ref,
               state_ref, yg_ref, cb_ref, yoff_ref, seg_ref):
    lane = lax.broadcasted_iota(jnp.int32, (CHUNK, LANES), 1)
    li = lax.broadcasted_iota(jnp.int32, (CHUNK, CHUNK), 0)
    cbias = jnp.where(li >= lane, 0.0, NEG_INF)
    cbias = jnp.concatenate([cbias] * HEADS_PER_GROUP, axis=1)

    dt = dt_ref[...]
    a = dt * (-jnp.exp(alog_ref[...]))
    a3 = jnp.concatenate([t.astype(BF16) for t in _split3(a)], axis=0)
    acs = _dot(tril3_ref[...], a3)
    b = jnp.maximum(jnp.log(dt), NEG_INF) - acs
    expa = jnp.exp(acs)
    wst = jnp.exp(acs[CHUNK - 1:CHUNK, :] + b)
    spread = _dot(jnp.concatenate([_pack3(expa, lane, 0.0), _pack3(wst, lane, 0.0)], axis=0), e3_ref[...])
    e_exp = spread[:CHUNK]
    w_exp = spread[CHUNK:]
    u3 = _pack3(acs, lane, 1.0)
    bt_terms = _split3(b.T)
    sub = lax.broadcasted_iota(jnp.int32, (HEADS_PER_GROUP, HEADS_PER_GROUP * CHUNK), 0)
    blk = lax.broadcasted_iota(jnp.int32, (HEADS_PER_GROUP, HEADS_PER_GROUP * CHUNK), 1) // CHUNK
    on_diag = sub == blk
    lane_lo = lane < SSM_HEAD_DIM
    ssq = jnp.zeros((CHUNK, 1), F32)
    gw = HEADS_PER_GROUP * SSM_HEAD_DIM

    def stage(g):
        gc = slice(g * gw, (g + 1) * gw)
        bg_b = xbc_ref[:, D_INNER + g * D_STATE:D_INNER + (g + 1) * D_STATE]
        cg_b = xbc_ref[:, D_INNER + D_BC + g * D_STATE:D_INNER + D_BC + (g + 1) * D_STATE]
        cb_ref[g % 2] = _dot_nt(cg_b, bg_b)
        st = state_ref[:, gc]
        yoff_ref[g % 2] = _dot(cg_b, st.astype(BF16))
        xw_b = (xbc_ref[:, gc].astype(F32) * w_exp[:, gc]).astype(BF16)
        state_ref[:, gc] = st * e_exp[CHUNK - 1:CHUNK, gc] + _dot_tn(bg_b, xw_b)
        dyn = [jnp.where(on_diag, jnp.concatenate([t[g * HEADS_PER_GROUP:(g + 1) * HEADS_PER_GROUP, :]]
                                                  * HEADS_PER_GROUP, axis=1), 0.0) for t in bt_terms]
        dyn.append(jnp.zeros_like(dyn[0]))
        v3 = jnp.concatenate([selc_ref[g], jnp.concatenate(dyn, axis=0).astype(BF16)], axis=0)
        seg_ref[g % 2] = _dot(u3, v3)

    stage(0)
    for g in range(SSM_GROUPS):
        if g + 1 < SSM_GROUPS:
            stage(g + 1)
        gc = slice(g * gw, (g + 1) * gw)
        cb_mat, y_off, seg = cb_ref[g % 2], yoff_ref[g % 2], seg_ref[g % 2]
        m = (jnp.exp(seg + cbias) * jnp.concatenate([cb_mat] * HEADS_PER_GROUP, axis=1)).astype(BF16)
        y_parts = []
        for pr in range(HEADS_PER_GROUP // 2):
            c0 = g * gw + pr * LANES
            xp = xbc_ref[:, c0:c0 + LANES]
            zero = jnp.zeros_like(xp)
            rhs = jnp.concatenate([jnp.where(lane_lo, xp, zero), jnp.where(lane_lo, zero, xp)], axis=0)
            y_parts.append(_dot(m[:, pr * 2 * CHUNK:(pr + 1) * 2 * CHUNK], rhs))
        y = jnp.concatenate(y_parts, axis=1) + y_off * e_exp[:, gc] + dskip_ref[:, gc] * xbc_ref[:, gc].astype(F32)
        y = y * zs_ref[:, gc].astype(F32)
        ssq = ssq + jnp.sum(y * y, axis=-1, keepdims=True)
        yg_ref[:, gc] = y
    inv = lax.rsqrt(ssq * (1.0 / D_INNER) + RMS_EPS)
    o_ref[...] = (yg_ref[...] * inv * gn_ref[...]).astype(BF16)


def _ssd_constants():
    tril3 = np.tile(np.tril(np.ones((CHUNK, CHUNK), np.float32)), (1, 3))
    e3 = np.zeros((LANES, D_INNER), np.float32)
    selc = np.zeros((SSM_GROUPS, 3 * REP, HEADS_PER_GROUP * CHUNK), np.float32)
    for t in range(3):
        for h in range(SSM_HEADS):
            e3[REP * t + h, h * SSM_HEAD_DIM:(h + 1) * SSM_HEAD_DIM] = 1.0
            g, k = divmod(h, HEADS_PER_GROUP)
            selc[g, REP * t + h, k * CHUNK:(k + 1) * CHUNK] = 1.0
    return jnp.asarray(tril3, BF16), jnp.asarray(e3, BF16), jnp.asarray(selc, BF16)


def _ssd(zs, xbc, dt, a_log, d_skip, gate_norm, state_init, bsz, nc, head):
    rows = zs.shape[0]
    tril3, e3, selc = _ssd_constants()
    cps = _chunks_per_step(nc)
    steps = nc // cps
    row_spec = lambda n: pl.BlockSpec((cps * CHUNK, n), lambda b, c: (b * steps + c, 0))
    out_specs = [row_spec(D_INNER)]
    out_shape = [jax.ShapeDtypeStruct((rows, D_INNER), BF16)]
    if head:
        out_specs.append(_resident((D_STATE, D_INNER)))
        out_shape.append(jax.ShapeDtypeStruct((D_STATE, D_INNER), F32))
    return pl.pallas_call(
        functools.partial(_ssd_kernel, head=head, cps=cps),
        grid=(bsz, steps),
        in_specs=[row_spec(D_INNER), row_spec(D_XBC), row_spec(LANES),
                  _resident(a_log.shape), _resident(d_skip.shape), _resident(gate_norm.shape),
                  _resident(tril3.shape), _resident(e3.shape), _resident(selc.shape),
                  _resident(state_init.shape)],
        out_specs=out_specs,
        out_shape=out_shape,
        scratch_shapes=[pltpu.VMEM((D_STATE, D_INNER), F32),
                        pltpu.VMEM((cps * CHUNK, D_INNER), F32),
                        pltpu.VMEM((2, CHUNK, CHUNK), F32),
                        pltpu.VMEM((2, CHUNK, HEADS_PER_GROUP * SSM_HEAD_DIM), F32),
                        pltpu.VMEM((2, CHUNK, HEADS_PER_GROUP * CHUNK), F32)],
        compiler_params=_params(2),
        name="mamba_ssd",
    )(zs, xbc, dt, a_log, d_skip, gate_norm, tril3, e3, selc, state_init)


def _ffn_kernel(*refs, tiles_per_seq, tm, tn, head, mix_transposed):
    (res_ref, mix_ref, wmix_ref, gmix_ref, gpre_ref, wg_ref, wv_ref, cw_ref, cb_ref, wd_ref, gpost_ref,
     init_ref, o_ref) = refs[:13]
    tail_ref = refs[13] if head else None
    ext_ref, carry_ref, act_ref = refs[-3:]
    first = (pl.program_id(0) % tiles_per_seq) == 0
    part = tm // 2 if (tm // 2) % LANES == 0 else tm
    mixes = [_dot_tn(mix_ref[:, r0:r0 + part], wmix_ref[...]) if mix_transposed
             else _dot(mix_ref[r0:r0 + part, :], wmix_ref[...]) for r0 in range(0, tm, part)]
    x = jnp.concatenate([res_ref[r0:r0 + part, :] + _rms(m) * gmix_ref[...]
                         for r0, m in zip(range(0, tm, part), mixes)], axis=0)
    xn = _normed_input(x, gpre_ref, head)

    def up_conv(w_ref, col):
        u = _dot(xn, w_ref[:, col % D_FF:col % D_FF + tn])
        _conv_stage(u, first, init_ref, ext_ref, carry_ref, tail_ref, col)
        return _conv_finish(ext_ref, cw_ref, cb_ref, col, FFN_CONV, u)

    for j in range(D_FF // tn):
        gate = up_conv(wg_ref, j * tn)
        val = up_conv(wv_ref, D_FF + j * tn)
        act_ref[:, j * tn:(j + 1) * tn] = (_silu(gate) * val).astype(BF16)
    ys = [_dot(act_ref[r0:r0 + part, :], wd_ref[...]) for r0 in range(0, tm, part)]
    for r0, y in zip(range(0, tm, part), ys):
        o_ref[r0:r0 + part, :] = x[r0:r0 + part, :] + _rms(y) * gpost_ref[...]


def _ffn(res, mix, wmix, gmix, gpre, w_up, w_down, layer, conv_w, conv_b, gpost, conv_init, seq_rows, head,
         mix_transposed):
    rows = res.shape[0]
    k = wmix.shape[0]
    tm, tn = _row_tile(head), FFN_TN
    assert seq_rows % tm == 0 and D_FF % tn == 0
    row_spec = pl.BlockSpec((tm, D_MODEL), lambda i: (i, 0))
    mix_spec = pl.BlockSpec((k, tm), lambda i: (0, i)) if mix_transposed else pl.BlockSpec((tm, k), lambda i: (i, 0))
    up_half = lambda half: pl.BlockSpec((None, D_MODEL, D_FF), lambda i: (layer, 0, half), pipeline_mode=pl.Buffered(1))
    down_spec = pl.BlockSpec((None, D_FF, D_MODEL), lambda i: (layer, 0, 0), pipeline_mode=pl.Buffered(1))
    out_specs = [row_spec]
    out_shape = [jax.ShapeDtypeStruct((rows, D_MODEL), F32)]
    if head:
        out_specs.append(_resident((HALO, 2 * D_FF)))
        out_shape.append(jax.ShapeDtypeStruct((HALO, 2 * D_FF), F32))
    return pl.pallas_call(
        functools.partial(_ffn_kernel, tiles_per_seq=seq_rows // tm, tm=tm, tn=tn, head=head,
                          mix_transposed=mix_transposed),
        grid=(rows // tm,),
        in_specs=[row_spec, mix_spec, _resident(wmix.shape), _resident(gmix.shape), _resident(gpre.shape),
                  up_half(0), up_half(1), _resident(conv_w.shape), _resident(conv_b.shape), down_spec,
                  _resident(gpost.shape), _resident(conv_init.shape)],
        out_specs=out_specs,
        out_shape=out_shape,
        scratch_shapes=[pltpu.VMEM((HALO + tm, tn), F32),
                        pltpu.VMEM((HALO, 2 * D_FF), F32),
                        pltpu.VMEM((tm, D_FF), BF16)],
        compiler_params=_params(1),
        name="mix_proj_conv_ffn",
    )(res, mix, wmix, gmix, gpre, w_up, w_up, conv_w, conv_b, w_down, gpost, conv_init)


def _qkv_proj_kernel(x_ref, gq_ref, gkv_ref, wq_ref, wk_ref, wv_ref, qt_ref, k_ref, vt_ref):
    xr = _rms(x_ref[...])
    xq = (xr * gq_ref[...]).astype(BF16)
    xkv = (xr * gkv_ref[...]).astype(BF16)
    scale = 1.0 / math.sqrt(ATTN_HEAD_DIM)
    qt_ref[...] = (_dot(xq, wq_ref[...]) * scale).T.astype(BF16)
    k_ref[...] = _dot(xkv, wk_ref[...]).astype(BF16)
    vt_ref[...] = _dot(xkv, wv_ref[...]).T.astype(BF16)


def _qkv_proj(h, gq, gkv, wq, wkv, head):
    rows = h.shape[0]
    tm = CHUNK if head else PROJ_TM
    return pl.pallas_call(
        _qkv_proj_kernel,
        grid=(rows // tm,),
        in_specs=[pl.BlockSpec((tm, D_MODEL), lambda i: (i, 0)), _resident(gq.shape), _resident(gkv.shape),
                  _resident(wq.shape), pl.BlockSpec((D_MODEL, D_KV), lambda i: (0, 0), pipeline_mode=pl.Buffered(1)),
                  pl.BlockSpec((D_MODEL, D_KV), lambda i: (0, 1), pipeline_mode=pl.Buffered(1))],
        out_specs=[pl.BlockSpec((D_ATTN, tm), lambda i: (0, i)), pl.BlockSpec((tm, D_KV), lambda i: (i, 0)),
                   pl.BlockSpec((D_KV, tm), lambda i: (0, i))],
        out_shape=[jax.ShapeDtypeStruct((D_ATTN, rows), BF16), jax.ShapeDtypeStruct((rows, D_KV), BF16),
                   jax.ShapeDtypeStruct((D_KV, rows), BF16)],
        compiler_params=_params(1),
        name="qkv_proj",
    )(h, gq, gkv, wq, wkv, wkv)


def _attn_kernel(sink_ref, qt_ref, kc_ref, kp_ref, km_ref, vc_ref, vp_ref, vm_ref, ot_ref, s_ref, *, head, cps):
    def views(sub):
        rows = pl.ds(sub * CHUNK, CHUNK)
        if sub == 0:
            return rows, kp_ref, vp_ref, pl.program_id(1) >= 1
        before = pl.ds((sub - 1) * CHUNK, CHUNK)
        return rows, kc_ref.at[before], vc_ref.at[:, before], True

    def scores(sub):
        rows, k_prev, _, _ = views(sub)
        _attn_scores(qt_ref.at[:, rows], kc_ref.at[rows], k_prev, km_ref, s_ref.at[sub % 2])

    def finish(sub):
        rows, _, v_prev, has_prev = views(sub)
        _attn_finish(sink_ref, s_ref.at[sub % 2], vc_ref.at[:, rows], v_prev, vm_ref, ot_ref.at[:, rows], has_prev, head)

    scores(0)
    for sub in range(cps):
        if sub + 1 < cps:
            scores(sub + 1)
        finish(sub)


def _attn_scores(qt_ref, kc_ref, kp_ref, km_ref, s_ref):
    k_all = jnp.concatenate([km_ref[PAD:CHUNK, :], kp_ref[...], kc_ref[...]], axis=0)
    qw = Q_PER_KV * CHUNK
    zeros = jnp.zeros((ATTN_HEAD_DIM, qw), BF16)

    def q_heads(kh):
        return jnp.concatenate([qt_ref[hq * ATTN_HEAD_DIM:(hq + 1) * ATTN_HEAD_DIM, :]
                                for hq in range(kh * Q_PER_KV, (kh + 1) * Q_PER_KV)], axis=1)

    for pair in range(N_KV_HEADS // 2):
        qbd = jnp.concatenate([jnp.concatenate([q_heads(2 * pair), zeros], axis=1),
                               jnp.concatenate([zeros, q_heads(2 * pair + 1)], axis=1)], axis=0)
        s_ref[:, pair * 2 * qw:(pair + 1) * 2 * qw] = _dot(k_all[:, pair * LANES:(pair + 1) * LANES], qbd)


def _attn_finish(sink_ref, s_ref, vc_ref, vp_ref, vm_ref, ot_ref, has_prev, head):
    nkeys = N_META + 2 * CHUNK
    ki = lax.broadcasted_iota(jnp.int32, (nkeys, CHUNK), 0)
    qi = lax.broadcasted_iota(jnp.int32, (nkeys, CHUNK), 1)
    kprev = ki - N_META
    kcur = ki - (N_META + CHUNK)
    in_cur = jnp.logical_and(kcur >= 0, kcur <= qi)
    if head:
        visible = jnp.logical_and(in_cur, kcur >= PAD)
    else:
        in_prev = jnp.logical_and(jnp.logical_and(kprev >= 0, kprev < CHUNK), jnp.logical_and(kprev > qi, has_prev))
        visible = jnp.logical_or(jnp.logical_or(ki < N_META, in_prev), in_cur)
    bias = jnp.concatenate([jnp.where(visible, 0.0, NEG_INF)] * N_Q_HEADS, axis=1)
    qw = Q_PER_KV * CHUNK
    s = s_ref[...] + bias
    sink = jnp.concatenate([jnp.full((1, CHUNK), sink_ref[hq], F32) for hq in range(N_Q_HEADS)], axis=1)
    m = jnp.maximum(jnp.max(s, axis=0, keepdims=True), sink)
    p = jnp.exp(s - m)
    inv_den = 1.0 / (jnp.sum(p, axis=0, keepdims=True) + jnp.exp(sink - m))
    pb = p.astype(BF16)
    pad_rows = jnp.zeros((PAD, qw), BF16)
    for kh in range(N_KV_HEADS):
        rows = slice(kh * ATTN_HEAD_DIM, (kh + 1) * ATTN_HEAD_DIM)
        vt = jnp.concatenate([vm_ref[rows, :], vp_ref[rows, :], vc_ref[rows, :]], axis=1)
        p_all = jnp.concatenate([pad_rows, pb[:, kh * qw:(kh + 1) * qw]], axis=0)
        o = _dot(vt, p_all) * inv_den[:, kh * qw:(kh + 1) * qw]
        for g in range(Q_PER_KV):
            hq = kh * Q_PER_KV + g
            ot_ref[hq * ATTN_HEAD_DIM:(hq + 1) * ATTN_HEAD_DIM, :] = o[:, g * CHUNK:(g + 1) * CHUNK].astype(BF16)


def _attn(qt, k, vt, k_head, vt_head, sinks, bsz, nc, head):
    rows = qt.shape[1]
    cps = _chunks_per_step(nc)
    steps = nc // cps
    cur = lambda b, c: b * steps + c
    prev = lambda b, c: b * nc + jnp.maximum(c * cps - 1, 0)
    return pl.pallas_call(
        functools.partial(_attn_kernel, head=head, cps=cps),
        grid=(bsz, steps),
        in_specs=[pl.BlockSpec(memory_space=pltpu.SMEM),
                  pl.BlockSpec((D_ATTN, cps * CHUNK), lambda b, c: (0, cur(b, c))),
                  pl.BlockSpec((cps * CHUNK, D_KV), lambda b, c: (cur(b, c), 0)),
                  pl.BlockSpec((CHUNK, D_KV), lambda b, c: (prev(b, c), 0)), _resident(k_head.shape),
                  pl.BlockSpec((D_KV, cps * CHUNK), lambda b, c: (0, cur(b, c))),
                  pl.BlockSpec((D_KV, CHUNK), lambda b, c: (0, prev(b, c))), _resident(vt_head.shape)],
        out_specs=pl.BlockSpec((D_ATTN, cps * CHUNK), lambda b, c: (0, cur(b, c))),
        out_shape=jax.ShapeDtypeStruct((D_ATTN, rows), BF16),
        scratch_shapes=[pltpu.VMEM((2, N_META + 2 * CHUNK, N_Q_HEADS * CHUNK), F32)],
        compiler_params=_params(2),
        name="swa_sink_attention",
    )(sinks, qt, k, k, k_head, vt, vt, vt_head)


def kernel(x, meta_tokens, a_norm_pre, a_w_in, a_conv_w, a_conv_b, a_dt_bias, a_a_log, a_d_skip, a_gate_norm, a_w_out, a_norm_post, kv_norm, w_kv, b_norm_pre, b_w_q, b_sinks, b_w_o, b_norm_post, f_norm_pre, f_w_up, f_conv_w, f_conv_b, f_w_down, f_norm_post):
    bsz, seq, _ = x.shape
    assert seq % BODY_TM == 0
    depth = f_norm_pre.shape[0]
    n_a = a_norm_pre.shape[0]

    def row(v):
        return v.astype(F32).reshape(1, -1)

    def rep3(v):
        return jnp.pad(jnp.concatenate([v] * 3, axis=1), ((0, 0), (0, LANES - 3 * REP)))

    passes = {True: (1, CHUNK, 1), False: (bsz, seq, seq // CHUNK)}
    hs = {True: jnp.concatenate([jnp.zeros((PAD, D_MODEL), F32), meta_tokens.astype(F32)], axis=0),
          False: x.astype(F32).reshape(bsz * seq, D_MODEL)}

    w_in_b, w_up_b, w_down_b = a_w_in.astype(BF16), f_w_up.astype(BF16), f_w_down.astype(BF16)
    kv = {}
    for i in range(depth):
        mixes = {}
        if i < n_a:
            j = i
            wdt = rep3(a_w_in[j][:, D_INNER + D_XBC:]).astype(BF16)
            w_mix, g_mix, mix_transposed = a_w_out[j].astype(BF16), row(a_norm_post[j]), False
            conv_init = jnp.zeros((HALO, D_XBC), F32)
            state_init = jnp.zeros((D_STATE, D_INNER), F32)
            for head in (True, False):
                nseq, seq_rows, nc = passes[head]
                outs = _in_proj(hs[head], row(a_norm_pre[j]), w_in_b, j, wdt, a_conv_w[j].astype(F32),
                                row(a_conv_b[j]), rep3(row(a_dt_bias[j])), conv_init, seq_rows, head)
                ys = _ssd(outs[0], outs[1], outs[2], rep3(row(a_a_log[j])),
                          row(jnp.repeat(a_d_skip[j], SSM_HEAD_DIM)), row(a_gate_norm[j]), state_init, nseq, nc, head)
                mixes[head] = ys[0]
                if head:
                    conv_init, state_init = outs[3], ys[1]
        else:
            j = i - n_a
            wq, wkv = b_w_q[j].astype(BF16), w_kv.astype(BF16)
            w_mix, g_mix, mix_transposed = b_w_o[j].astype(BF16), row(b_norm_post[j]), True
            for head in (True, False):
                nseq, seq_rows, nc = passes[head]
                qt, k_new, vt_new = _qkv_proj(hs[head], row(b_norm_pre[j]), row(kv_norm), wq, wkv, head)
                if j == 0:
                    kv[head] = (k_new, vt_new)
                mixes[head] = _attn(qt, kv[head][0], kv[head][1], kv[True][0], kv[True][1],
                                    b_sinks[j].astype(F32), nseq, nc, head)
        conv_init = jnp.zeros((HALO, 2 * D_FF), F32)
        for head in (True, False):
            nseq, seq_rows, nc = passes[head]
            outs = _ffn(hs[head], mixes[head], w_mix, g_mix, row(f_norm_pre[i]), w_up_b, w_down_b, i,
                        f_conv_w[i].astype(F32), row(f_conv_b[i]), row(f_norm_post[i]), conv_init, seq_rows, head,
                        mix_transposed)
            hs[head] = outs[0]
            if head:
                conv_init = outs[1]
    return hs[False].reshape(bsz, seq, D_MODEL)
```

```python
import functools
import math

import jax
import jax.numpy as jnp
import numpy as np
from jax import lax
from jax.experimental import pallas as pl
from jax.experimental.pallas import tpu as pltpu

F32 = jnp.float32
BF16 = jnp.bfloat16

D_MODEL = 1024
N_META = 16
CHUNK = 128
PAD = CHUNK - N_META

D_INNER = 2048
SSM_HEAD_DIM = 64
SSM_HEADS = 32
SSM_GROUPS = 4
HEADS_PER_GROUP = SSM_HEADS // SSM_GROUPS
D_STATE = 128
SSM_CONV = 4
D_BC = SSM_GROUPS * D_STATE
D_XBC = D_INNER + 2 * D_BC

ATTN_HEAD_DIM = 64
N_Q_HEADS = 16
N_KV_HEADS = 4
Q_PER_KV = N_Q_HEADS // N_KV_HEADS
D_ATTN = N_Q_HEADS * ATTN_HEAD_DIM
D_KV = N_KV_HEADS * ATTN_HEAD_DIM

D_FF = 2816
FFN_CONV = 3

RMS_EPS = 1e-6
LOG2E = math.log2(math.e)
NEG_INF = -1e30

LANES = 128
HALO = 8
REP = SSM_HEADS
assert 3 * REP <= LANES
VMEM_LIMIT = 56 * 1024 * 1024
BODY_TM = 4 * CHUNK
CHUNKS_PER_STEP = 4
IN_PROJ_TM = 8 * CHUNK
IN_PROJ_TN = 512
XBC_BLOCK = 1024
XBC_BLOCKS = D_XBC // XBC_BLOCK
assert D_INNER % XBC_BLOCK == 0 and D_XBC % XBC_BLOCK == 0
FFN_TN = 256


def _rms(x):
    return x * lax.rsqrt(jnp.mean(x * x, axis=-1, keepdims=True) + RMS_EPS)


def _silu(x):
    return x * jax.nn.sigmoid(x)


def _dot(a, b):
    return jnp.dot(a, b, preferred_element_type=F32)


def _dot_nt(a, b):
    return lax.dot_general(a, b, (((1,), (1,)), ((), ())), preferred_element_type=F32)


def _dot_tn(a, b):
    return lax.dot_general(a, b, (((0,), (0,)), ((), ())), preferred_element_type=F32)


def _resident(shape):
    return pl.BlockSpec(shape, lambda *_: (0,) * len(shape), pipeline_mode=pl.Buffered(1))


def _params(n_axes):
    return pltpu.CompilerParams(dimension_semantics=("arbitrary",) * n_axes,
                                vmem_limit_bytes=VMEM_LIMIT)


def _row_tile(head):
    return CHUNK if head else BODY_TM


def _chunks_per_step(nc):
    return CHUNKS_PER_STEP if nc % CHUNKS_PER_STEP == 0 else 1


def _conv_stage(u, first, init_ref, buf, carry_ref, tail_ref, col):
    tm, tn = u.shape
    cols = slice(col, col + tn)
    buf[0:HALO, :] = jnp.where(first, init_ref[:, cols], carry_ref[:, cols])
    buf[HALO:HALO + tm, :] = u
    carry_ref[:, cols] = u[tm - HALO:tm, :]
    if tail_ref is not None:
        tail_ref[:, cols] = u[tm - HALO:tm, :]


def _conv_finish(buf, cw_ref, cb_ref, col, taps, u=None):
    tm, tn = buf.shape[0] - HALO, buf.shape[1]
    cols = slice(col, col + tn)
    out = cb_ref[:, cols] + cw_ref[taps - 1:taps, cols] * (buf[HALO:HALO + tm, :] if u is None else u)
    for k in range(taps - 1):
        s = taps - 1 - k
        out = out + cw_ref[k:k + 1, cols] * buf[HALO - s:HALO - s + tm, :]
    return out


def _normed_input(x, gain_ref, head):
    xn = _rms(x) * gain_ref[...]
    if head:
        row = lax.broadcasted_iota(jnp.int32, (x.shape[0], 1), 0)
        xn = jnp.where(row >= PAD, xn, 0.0)
    return xn.astype(BF16)


def _in_proj_kernel(*refs, tiles_per_seq, tm, tn, head):
    x_ref, g_ref, wz_ref = refs[:3]
    wx_refs = refs[3:3 + XBC_BLOCKS]
    wdt_ref, cw_ref, cb_ref, dtb_ref, init_ref, zs_ref, xbc_ref, dt_ref = refs[3 + XBC_BLOCKS:11 + XBC_BLOCKS]
    tail_ref = refs[11 + XBC_BLOCKS] if head else None
    ext_ref, carry_ref = refs[-2:]
    first = (pl.program_id(0) % tiles_per_seq) == 0
    xn = _normed_input(x_ref[...], g_ref, head)
    valid = lax.broadcasted_iota(jnp.int32, (tm, 1), 0) >= PAD
    def z_block(n0):
        zs_ref[:, n0:n0 + tn] = _silu(_dot(xn, wz_ref[:, n0:n0 + tn])).astype(BF16)

    def stage(n0):
        w_ref, w0 = wx_refs[n0 // XBC_BLOCK], n0 % XBC_BLOCK
        _conv_stage(_dot(xn, w_ref[:, w0:w0 + tn]), first, init_ref, ext_ref.at[(n0 // tn) % 2], carry_ref,
                    tail_ref, n0)

    def finish(n0):
        y = _silu(_conv_finish(ext_ref.at[(n0 // tn) % 2], cw_ref, cb_ref, n0, SSM_CONV))
        if head and n0 >= D_INNER:
            y = jnp.where(valid, y, 0.0)
        xbc_ref[:, n0:n0 + tn] = y.astype(BF16)

    z_cols, x_cols = list(range(0, D_INNER, tn)), list(range(0, D_XBC, tn))
    stage(x_cols[0])
    for i, n0 in enumerate(x_cols):
        if i + 1 < len(x_cols):
            stage(x_cols[i + 1])
        finish(n0)
        if z_cols:
            z_block(z_cols.pop(0))
    for n0 in z_cols:
        z_block(n0)
    dt_raw = _dot(xn, wdt_ref[...]) + dtb_ref[...]
    dt = jnp.maximum(dt_raw, 0.0) + jnp.log1p(jnp.exp(-jnp.abs(dt_raw)))
    dt_ref[...] = jnp.where(valid, dt, 0.0) if head else dt


def _in_proj(h, gain, w_in, layer, wdt, conv_w, conv_b, dt_bias, conv_init, seq_rows, head):
    rows = h.shape[0]
    tm, tn = (CHUNK if head else IN_PROJ_TM), IN_PROJ_TN
    assert seq_rows % tm == 0 and D_INNER % tn == 0 and XBC_BLOCK % tn == 0
    row_spec = lambda n: pl.BlockSpec((tm, n), lambda i: (i, 0))
    w_block = lambda width, idx: pl.BlockSpec((None, D_MODEL, width), lambda i: (layer, 0, idx),
                                              pipeline_mode=pl.Buffered(1))
    out_specs = [row_spec(D_INNER), row_spec(D_XBC), row_spec(LANES)]
    out_shape = [jax.ShapeDtypeStruct((rows, D_INNER), BF16), jax.ShapeDtypeStruct((rows, D_XBC), BF16),
                 jax.ShapeDtypeStruct((rows, LANES), F32)]
    if head:
        out_specs.append(_resident((HALO, D_XBC)))
        out_shape.append(jax.ShapeDtypeStruct((HALO, D_XBC), F32))
    return pl.pallas_call(
        functools.partial(_in_proj_kernel, tiles_per_seq=seq_rows // tm, tm=tm, tn=tn, head=head),
        grid=(rows // tm,),
        in_specs=[row_spec(D_MODEL), _resident((1, D_MODEL)), w_block(D_INNER, 0)]
                 + [w_block(XBC_BLOCK, D_INNER // XBC_BLOCK + k) for k in range(XBC_BLOCKS)]
                 + [_resident(wdt.shape), _resident(conv_w.shape), _resident(conv_b.shape),
                    _resident(dt_bias.shape), _resident(conv_init.shape)],
        out_specs=out_specs,
        out_shape=out_shape,
        scratch_shapes=[pltpu.VMEM((2, HALO + tm, tn), F32), pltpu.VMEM((HALO, D_XBC), F32)],
        compiler_params=_params(1),
        name="mamba_in_proj",
    )(h, gain, w_in, *([w_in] * XBC_BLOCKS), wdt, conv_w, conv_b, dt_bias, conv_init)


def _split3(x):
    hi = x.astype(BF16).astype(F32)
    r = x - hi
    mid = r.astype(BF16).astype(F32)
    return hi, mid, r - mid


def _pack3(x, lane, fill):
    hi, mid, lo = _split3(x)
    packed = jnp.where(lane < REP, hi, jnp.where(lane < 2 * REP, mid, jnp.where(lane < 3 * REP, lo, fill)))
    return packed.astype(BF16)


def _ssd_kernel(*refs, head, cps):
    (zs_ref, xbc_ref, dt_ref, alog_ref, dskip_ref, gn_ref, tril3_ref, e3_ref, selc_ref, init_ref, o_ref) = refs[:11]
    final_ref = refs[11] if head else None
    state_ref, yg_ref, cb_ref, yoff_ref, seg_ref = refs[-5:]

    @pl.when(pl.program_id(1) == 0)
    def _():
        state_ref[...] = init_ref[...]

    for sub in range(cps):
        rows = pl.ds(sub * CHUNK, CHUNK)
        _ssd_chunk(zs_ref.at[rows], xbc_ref.at[rows], dt_ref.at[rows], alog_ref, dskip_ref, gn_ref, tril3_ref,
                   e3_ref, selc_ref, o_ref.at[rows], state_ref, yg_ref.at[rows], cb_ref, yoff_ref, seg_ref)
    if head:
        final_ref[...] = state_ref[...]


def _ssd_chunk(zs_ref, xbc_ref, dt_ref, alog_ref, dskip_ref, gn_ref, tril3_ref, e3_ref, selc_ref, o_ref,
               state_ref, yg_ref, cb_ref, yoff_ref, seg_ref):
    lane = lax.broadcasted_iota(jnp.int32, (CHUNK, LANES), 1)
    li = lax.broadcasted_iota(jnp.int32, (CHUNK, CHUNK), 0)
    cbias = jnp.where(li >= lane, 0.0, NEG_INF)
    cbias = jnp.concatenate([cbias] * HEADS_PER_GROUP, axis=1)

    dt = dt_ref[...]
    a = dt * (-jnp.exp(alog_ref[...]))
    a3 = jnp.concatenate([t.astype(BF16) for t in _split3(a)], axis=0)
    acs = _dot(tril3_ref[...], a3)
    b = jnp.maximum(jnp.log(dt), NEG_INF) - acs
    expa = jnp.exp(acs)
    wst = jnp.exp(acs[CHUNK - 1:CHUNK, :] + b)
    spread = _dot(jnp.concatenate([_pack3(expa, lane, 0.0), _pack3(wst, lane, 0.0)], axis=0), e3_ref[...])
    e_exp = spread[:CHUNK]
    w_exp = spread[CHUNK:]
    u3 = _pack3(acs * LOG2E, lane, 1.0)
    bt_terms = _split3((b * LOG2E).T)
    sub = lax.broadcasted_iota(jnp.int32, (HEADS_PER_GROUP, HEADS_PER_GROUP * CHUNK), 0)
    blk = lax.broadcasted_iota(jnp.int32, (HEADS_PER_GROUP, HEADS_PER_GROUP * CHUNK), 1) // CHUNK
    on_diag = sub == blk
    lane_lo = lane < SSM_HEAD_DIM
    ssq = jnp.zeros((CHUNK, 1), F32)
    gw = HEADS_PER_GROUP * SSM_HEAD_DIM

    def stage(g):
        gc = slice(g * gw, (g + 1) * gw)
        bg_b = xbc_ref[:, D_INNER + g * D_STATE:D_INNER + (g + 1) * D_STATE]
        cg_b = xbc_ref[:, D_INNER + D_BC + g * D_STATE:D_INNER + D_BC + (g + 1) * D_STATE]
        cb_ref[g % 2] = _dot_nt(cg_b, bg_b)
        st = state_ref[:, gc]
        yoff_ref[g % 2] = _dot(cg_b, st.astype(BF16))
        xw_b = (xbc_ref[:, gc].astype(F32) * w_exp[:, gc]).astype(BF16)
        state_ref[:, gc] = st * e_exp[CHUNK - 1:CHUNK, gc] + _dot_tn(bg_b, xw_b)
        dyn = [jnp.where(on_diag, jnp.concatenate([t[g * HEADS_PER_GROUP:(g + 1) * HEADS_PER_GROUP, :]]
                                                  * HEADS_PER_GROUP, axis=1), 0.0) for t in bt_terms]
        dyn.append(jnp.zeros_like(dyn[0]))
        v3 = jnp.concatenate([selc_ref[g], jnp.concatenate(dyn, axis=0).astype(BF16)], axis=0)
        seg_ref[g % 2] = _dot(u3, v3)

    stage(0)
    for g in range(SSM_GROUPS):
        if g + 1 < SSM_GROUPS:
            stage(g + 1)
        gc = slice(g * gw, (g + 1) * gw)
        cb_mat, y_off, seg = cb_ref[g % 2], yoff_ref[g % 2], seg_ref[g % 2]
        m = (jnp.exp2(seg + cbias) * jnp.concatenate([cb_mat] * HEADS_PER_GROUP, axis=1)).astype(BF16)
        y_parts = []
        for pr in range(HEADS_PER_GROUP // 2):
            c0 = g * gw + pr * LANES
            xp = xbc_ref[:, c0:c0 + LANES]
            zero = jnp.zeros_like(xp)
            rhs = jnp.concatenate([jnp.where(lane_lo, xp, zero), jnp.where(lane_lo, zero, xp)], axis=0)
            y_parts.append(_dot(m[:, pr * 2 * CHUNK:(pr + 1) * 2 * CHUNK], rhs))
        y = jnp.concatenate(y_parts, axis=1) + y_off * e_exp[:, gc] + dskip_ref[:, gc] * xbc_ref[:, gc].astype(F32)
        y = y * zs_ref[:, gc].astype(F32)
        ssq = ssq + jnp.sum(y * y, axis=-1, keepdims=True)
        yg_ref[:, gc] = y
    inv = lax.rsqrt(ssq * (1.0 / D_INNER) + RMS_EPS)
    o_ref[...] = (yg_ref[...] * inv * gn_ref[...]).astype(BF16)


def _ssd_constants():
    tril3 = np.tile(np.tril(np.ones((CHUNK, CHUNK), np.float32)), (1, 3))
    e3 = np.zeros((LANES, D_INNER), np.float32)
    selc = np.zeros((SSM_GROUPS, 3 * REP, HEADS_PER_GROUP * CHUNK), np.float32)
    for t in range(3):
        for h in range(SSM_HEADS):
            e3[REP * t + h, h * SSM_HEAD_DIM:(h + 1) * SSM_HEAD_DIM] = 1.0
            g, k = divmod(h, HEADS_PER_GROUP)
            selc[g, REP * t + h, k * CHUNK:(k + 1) * CHUNK] = 1.0
    return jnp.asarray(tril3, BF16), jnp.asarray(e3, BF16), jnp.asarray(selc, BF16)


def _ssd(zs, xbc, dt, a_log, d_skip, gate_norm, state_init, bsz, nc, head):
    rows = zs.shape[0]
    tril3, e3, selc = _ssd_constants()
    cps = _chunks_per_step(nc)
    steps = nc // cps
    row_spec = lambda n: pl.BlockSpec((cps * CHUNK, n), lambda b, c: (b * steps + c, 0))
    out_specs = [row_spec(D_INNER)]
    out_shape = [jax.ShapeDtypeStruct((rows, D_INNER), BF16)]
    if head:
        out_specs.append(_resident((D_STATE, D_INNER)))
        out_shape.append(jax.ShapeDtypeStruct((D_STATE, D_INNER), F32))
    return pl.pallas_call(
        functools.partial(_ssd_kernel, head=head, cps=cps),
        grid=(bsz, steps),
        in_specs=[row_spec(D_INNER), row_spec(D_XBC), row_spec(LANES),
                  _resident(a_log.shape), _resident(d_skip.shape), _resident(gate_norm.shape),
                  _resident(tril3.shape), _resident(e3.shape), _resident(selc.shape),
                  _resident(state_init.shape)],
        out_specs=out_specs,
        out_shape=out_shape,
        scratch_shapes=[pltpu.VMEM((D_STATE, D_INNER), F32),
                        pltpu.VMEM((cps * CHUNK, D_INNER), F32),
                        pltpu.VMEM((2, CHUNK, CHUNK), F32),
                        pltpu.VMEM((2, CHUNK, HEADS_PER_GROUP * SSM_HEAD_DIM), F32),
                        pltpu.VMEM((2, CHUNK, HEADS_PER_GROUP * CHUNK), F32)],
        compiler_params=_params(2),
        name="mamba_ssd",
    )(zs, xbc, dt, a_log, d_skip, gate_norm, tril3, e3, selc, state_init)


def _ffn_kernel(*refs, tiles_per_seq, tm, tn, head, mix_transposed):
    (res_ref, mix_ref, wmix_ref, gmix_ref, gpre_ref, wg_ref, wv_ref, cw_ref, cb_ref, wd_ref, gpost_ref,
     init_ref, o_ref) = refs[:13]
    tail_ref = refs[13] if head else None
    ext_ref, carry_ref, act_ref = refs[-3:]
    first = (pl.program_id(0) % tiles_per_seq) == 0
    part = tm // 2 if (tm // 2) % LANES == 0 else tm
    mixes = [_dot_tn(mix_ref[:, r0:r0 + part], wmix_ref[...]) if mix_transposed
             else _dot(mix_ref[r0:r0 + part, :], wmix_ref[...]) for r0 in range(0, tm, part)]
    x = jnp.concatenate([res_ref[r0:r0 + part, :] + _rms(m) * gmix_ref[...]
                         for r0, m in zip(range(0, tm, part), mixes)], axis=0)
    xn = _normed_input(x, gpre_ref, head)

    def up_conv(w_ref, col):
        u = _dot(xn, w_ref[:, col % D_FF:col % D_FF + tn])
        _conv_stage(u, first, init_ref, ext_ref, carry_ref, tail_ref, col)
        return _conv_finish(ext_ref, cw_ref, cb_ref, col, FFN_CONV, u)

    for j in range(D_FF // tn):
        gate = up_conv(wg_ref, j * tn)
        val = up_conv(wv_ref, D_FF + j * tn)
        act_ref[:, j * tn:(j + 1) * tn] = (_silu(gate) * val).astype(BF16)
    ys = [_dot(act_ref[r0:r0 + part, :], wd_ref[...]) for r0 in range(0, tm, part)]
    for r0, y in zip(range(0, tm, part), ys):
        o_ref[r0:r0 + part, :] = x[r0:r0 + part, :] + _rms(y) * gpost_ref[...]


def _ffn(res, mix, wmix, gmix, gpre, w_up, w_down, layer, conv_w, conv_b, gpost, conv_init, seq_rows, head,
         mix_transposed):
    rows = res.shape[0]
    k = wmix.shape[0]
    tm, tn = _row_tile(head), FFN_TN
    assert seq_rows % tm == 0 and D_FF % tn == 0
    row_spec = pl.BlockSpec((tm, D_MODEL), lambda i: (i, 0))
    mix_spec = pl.BlockSpec((k, tm), lambda i: (0, i)) if mix_transposed else pl.BlockSpec((tm, k), lambda i: (i, 0))
    up_half = lambda half: pl.BlockSpec((None, D_MODEL, D_FF), lambda i: (layer, 0, half), pipeline_mode=pl.Buffered(1))
    down_spec = pl.BlockSpec((None, D_FF, D_MODEL), lambda i: (layer, 0, 0), pipeline_mode=pl.Buffered(1))
    out_specs = [row_spec]
    out_shape = [jax.ShapeDtypeStruct((rows, D_MODEL), F32)]
    if head:
        out_specs.append(_resident((HALO, 2 * D_FF)))
        out_shape.append(jax.ShapeDtypeStruct((HALO, 2 * D_FF), F32))
    return pl.pallas_call(
        functools.partial(_ffn_kernel, tiles_per_seq=seq_rows // tm, tm=tm, tn=tn, head=head,
                          mix_transposed=mix_transposed),
        grid=(rows // tm,),
        in_specs=[row_spec, mix_spec, _resident(wmix.shape), _resident(gmix.shape), _resident(gpre.shape),
                  up_half(0), up_half(1), _resident(conv_w.shape), _resident(conv_b.shape), down_spec,
                  _resident(gpost.shape), _resident(conv_init.shape)],
        out_specs=out_specs,
        out_shape=out_shape,
        scratch_shapes=[pltpu.VMEM((HALO + tm, tn), F32),
                        pltpu.VMEM((HALO, 2 * D_FF), F32),
                        pltpu.VMEM((tm, D_FF), BF16)],
        compiler_params=_params(1),
        name="mix_proj_conv_ffn",
    )(res, mix, wmix, gmix, gpre, w_up, w_up, conv_w, conv_b, w_down, gpost, conv_init)


def _qkv_proj_kernel(x_ref, gq_ref, gkv_ref, wq_ref, wk_ref, wv_ref, qt_ref, k_ref, vt_ref):
    xr = _rms(x_ref[...])
    xq = (xr * gq_ref[...]).astype(BF16)
    xkv = (xr * gkv_ref[...]).astype(BF16)
    scale = 1.0 / math.sqrt(ATTN_HEAD_DIM)
    qt_ref[...] = (_dot(xq, wq_ref[...]) * scale).T.astype(BF16)
    k_ref[...] = _dot(xkv, wk_ref[...]).astype(BF16)
    vt_ref[...] = _dot(xkv, wv_ref[...]).T.astype(BF16)


def _qkv_proj(h, gq, gkv, wq, wkv, head):
    rows = h.shape[0]
    tm = CHUNK if head else IN_PROJ_TM
    return pl.pallas_call(
        _qkv_proj_kernel,
        grid=(rows // tm,),
        in_specs=[pl.BlockSpec((tm, D_MODEL), lambda i: (i, 0)), _resident(gq.shape), _resident(gkv.shape),
                  _resident(wq.shape), pl.BlockSpec((D_MODEL, D_KV), lambda i: (0, 0), pipeline_mode=pl.Buffered(1)),
                  pl.BlockSpec((D_MODEL, D_KV), lambda i: (0, 1), pipeline_mode=pl.Buffered(1))],
        out_specs=[pl.BlockSpec((D_ATTN, tm), lambda i: (0, i)), pl.BlockSpec((tm, D_KV), lambda i: (i, 0)),
                   pl.BlockSpec((D_KV, tm), lambda i: (0, i))],
        out_shape=[jax.ShapeDtypeStruct((D_ATTN, rows), BF16), jax.ShapeDtypeStruct((rows, D_KV), BF16),
                   jax.ShapeDtypeStruct((D_KV, rows), BF16)],
        compiler_params=_params(1),
        name="qkv_proj",
    )(h, gq, gkv, wq, wkv, wkv)


def _attn_kernel(sink_ref, qt_ref, kc_ref, kp_ref, km_ref, vc_ref, vp_ref, vm_ref, ot_ref, s_ref, *, head, cps):
    def views(sub):
        rows = pl.ds(sub * CHUNK, CHUNK)
        if sub == 0:
            return rows, kp_ref, vp_ref, pl.program_id(1) >= 1
        before = pl.ds((sub - 1) * CHUNK, CHUNK)
        return rows, kc_ref.at[before], vc_ref.at[:, before], True

    def scores(sub):
        rows, k_prev, _, _ = views(sub)
        _attn_scores(qt_ref.at[:, rows], kc_ref.at[rows], k_prev, km_ref, s_ref.at[sub % 2])

    def finish(sub):
        rows, _, v_prev, has_prev = views(sub)
        _attn_finish(sink_ref, s_ref.at[sub % 2], vc_ref.at[:, rows], v_prev, vm_ref, ot_ref.at[:, rows], has_prev, head)

    scores(0)
    for sub in range(cps):
        if sub + 1 < cps:
            scores(sub + 1)
        finish(sub)


def _attn_scores(qt_ref, kc_ref, kp_ref, km_ref, s_ref):
    k_all = jnp.concatenate([km_ref[PAD:CHUNK, :], kp_ref[...], kc_ref[...]], axis=0)
    qw = Q_PER_KV * CHUNK
    zeros = jnp.zeros((ATTN_HEAD_DIM, qw), BF16)

    def q_heads(kh):
        return jnp.concatenate([qt_ref[hq * ATTN_HEAD_DIM:(hq + 1) * ATTN_HEAD_DIM, :]
                                for hq in range(kh * Q_PER_KV, (kh + 1) * Q_PER_KV)], axis=1)

    for pair in range(N_KV_HEADS // 2):
        qbd = jnp.concatenate([jnp.concatenate([q_heads(2 * pair), zeros], axis=1),
                               jnp.concatenate([zeros, q_heads(2 * pair + 1)], axis=1)], axis=0)
        s_ref[:, pair * 2 * qw:(pair + 1) * 2 * qw] = _dot(k_all[:, pair * LANES:(pair + 1) * LANES], qbd)


def _attn_finish(sink_ref, s_ref, vc_ref, vp_ref, vm_ref, ot_ref, has_prev, head):
    nkeys = N_META + 2 * CHUNK
    ki = lax.broadcasted_iota(jnp.int32, (nkeys, CHUNK), 0)
    qi = lax.broadcasted_iota(jnp.int32, (nkeys, CHUNK), 1)
    kprev = ki - N_META
    kcur = ki - (N_META + CHUNK)
    in_cur = jnp.logical_and(kcur >= 0, kcur <= qi)
    if head:
        visible = jnp.logical_and(in_cur, kcur >= PAD)
    else:
        in_prev = jnp.logical_and(jnp.logical_and(kprev >= 0, kprev < CHUNK), jnp.logical_and(kprev > qi, has_prev))
        visible = jnp.logical_or(jnp.logical_or(ki < N_META, in_prev), in_cur)
    bias = jnp.concatenate([jnp.where(visible, 0.0, NEG_INF)] * N_Q_HEADS, axis=1)
    qw = Q_PER_KV * CHUNK
    s = s_ref[...] + bias
    sink = jnp.concatenate([jnp.full((1, CHUNK), sink_ref[hq], F32) for hq in range(N_Q_HEADS)], axis=1)
    m = jnp.maximum(jnp.max(s, axis=0, keepdims=True), sink)
    p = jnp.exp(s - m)
    inv_den = 1.0 / (jnp.sum(p, axis=0, keepdims=True) + jnp.exp(sink - m))
    pb = p.astype(BF16)
    pad_rows = jnp.zeros((PAD, qw), BF16)
    for kh in range(N_KV_HEADS):
        rows = slice(kh * ATTN_HEAD_DIM, (kh + 1) * ATTN_HEAD_DIM)
        vt = jnp.concatenate([vm_ref[rows, :], vp_ref[rows, :], vc_ref[rows, :]], axis=1)
        p_all = jnp.concatenate([pad_rows, pb[:, kh * qw:(kh + 1) * qw]], axis=0)
        o = _dot(vt, p_all) * inv_den[:, kh * qw:(kh + 1) * qw]
        for g in range(Q_PER_KV):
            hq = kh * Q_PER_KV + g
            ot_ref[hq * ATTN_HEAD_DIM:(hq + 1) * ATTN_HEAD_DIM, :] = o[:, g * CHUNK:(g + 1) * CHUNK].astype(BF16)


def _attn(qt, k, vt, k_head, vt_head, sinks, bsz, nc, head):
    rows = qt.shape[1]
    cps = _chunks_per_step(nc)
    steps = nc // cps
    cur = lambda b, c: b * steps + c
    prev = lambda b, c: b * nc + jnp.maximum(c * cps - 1, 0)
    return pl.pallas_call(
        functools.partial(_attn_kernel, head=head, cps=cps),
        grid=(bsz, steps),
        in_specs=[pl.BlockSpec(memory_space=pltpu.SMEM),
                  pl.BlockSpec((D_ATTN, cps * CHUNK), lambda b, c: (0, cur(b, c))),
                  pl.BlockSpec((cps * CHUNK, D_KV), lambda b, c: (cur(b, c), 0)),
                  pl.BlockSpec((CHUNK, D_KV), lambda b, c: (prev(b, c), 0)), _resident(k_head.shape),
                  pl.BlockSpec((D_KV, cps * CHUNK), lambda b, c: (0, cur(b, c))),
                  pl.BlockSpec((D_KV, CHUNK), lambda b, c: (0, prev(b, c))), _resident(vt_head.shape)],
        out_specs=pl.BlockSpec((D_ATTN, cps * CHUNK), lambda b, c: (0, cur(b, c))),
        out_shape=jax.ShapeDtypeStruct((D_ATTN, rows), BF16),
        scratch_shapes=[pltpu.VMEM((2, N_META + 2 * CHUNK, N_Q_HEADS * CHUNK), F32)],
        compiler_params=_params(2),
        name="swa_sink_attention",
    )(sinks, qt, k, k, k_head, vt, vt, vt_head)


def kernel(x, meta_tokens, a_norm_pre, a_w_in, a_conv_w, a_conv_b, a_dt_bias, a_a_log, a_d_skip, a_gate_norm, a_w_out, a_norm_post, kv_norm, w_kv, b_norm_pre, b_w_q, b_sinks, b_w_o, b_norm_post, f_norm_pre, f_w_up, f_conv_w, f_conv_b, f_w_down, f_norm_post):
    bsz, seq, _ = x.shape
    assert seq % BODY_TM == 0
    depth = f_norm_pre.shape[0]
    n_a = a_norm_pre.shape[0]

    def row(v):
        return v.astype(F32).reshape(1, -1)

    def rep3(v):
        return jnp.pad(jnp.concatenate([v] * 3, axis=1), ((0, 0), (0, LANES - 3 * REP)))

    passes = {True: (1, CHUNK, 1), False: (bsz, seq, seq // CHUNK)}
    hs = {True: jnp.concatenate([jnp.zeros((PAD, D_MODEL), F32), meta_tokens.astype(F32)], axis=0),
          False: x.astype(F32).reshape(bsz * seq, D_MODEL)}

    w_in_b, w_up_b, w_down_b = a_w_in.astype(BF16), f_w_up.astype(BF16), f_w_down.astype(BF16)
    kv = {}
    for i in range(depth):
        mixes = {}
        if i < n_a:
            j = i
            wdt = rep3(a_w_in[j][:, D_INNER + D_XBC:]).astype(BF16)
            w_mix, g_mix, mix_transposed = a_w_out[j].astype(BF16), row(a_norm_post[j]), False
            conv_init = jnp.zeros((HALO, D_XBC), F32)
            state_init = jnp.zeros((D_STATE, D_INNER), F32)
            for head in (True, False):
                nseq, seq_rows, nc = passes[head]
                outs = _in_proj(hs[head], row(a_norm_pre[j]), w_in_b, j, wdt, a_conv_w[j].astype(F32),
                                row(a_conv_b[j]), rep3(row(a_dt_bias[j])), conv_init, seq_rows, head)
                ys = _ssd(outs[0], outs[1], outs[2], rep3(row(a_a_log[j])),
                          row(jnp.repeat(a_d_skip[j], SSM_HEAD_DIM)), row(a_gate_norm[j]), state_init, nseq, nc, head)
                mixes[head] = ys[0]
                if head:
                    conv_init, state_init = outs[3], ys[1]
        else:
            j = i - n_a
            wq, wkv = b_w_q[j].astype(BF16), w_kv.astype(BF16)
            w_mix, g_mix, mix_transposed = b_w_o[j].astype(BF16), row(b_norm_post[j]), True
            for head in (True, False):
                nseq, seq_rows, nc = passes[head]
                qt, k_new, vt_new = _qkv_proj(hs[head], row(b_norm_pre[j]), row(kv_norm), wq, wkv, head)
                if j == 0:
                    kv[head] = (k_new, vt_new)
                mixes[head] = _attn(qt, kv[head][0], kv[head][1], kv[True][0], kv[True][1],
                                    b_sinks[j].astype(F32), nseq, nc, head)
        conv_init = jnp.zeros((HALO, 2 * D_FF), F32)
        for head in (True, False):
            nseq, seq_rows, nc = passes[head]
            outs = _ffn(hs[head], mixes[head], w_mix, g_mix, row(f_norm_pre[i]), w_up_b, w_down_b, i,
                        f_conv_w[i].astype(F32), row(f_conv_b[i]), row(f_norm_post[i]), conv_init, seq_rows, head,
                        mix_transposed)
            hs[head] = outs[0]
            if head:
                conv_init = outs[1]
    return hs[False].reshape(bsz, seq, D_MODEL)
```

```python
import functools
import math

import jax
import jax.numpy as jnp
import numpy as np
from jax import lax
from jax.experimental import pallas as pl
from jax.experimental.pallas import tpu as pltpu

F32 = jnp.float32
BF16 = jnp.bfloat16

D_MODEL = 1024
N_META = 16
CHUNK = 128
PAD = CHUNK - N_META

D_INNER = 2048
SSM_HEAD_DIM = 64
SSM_HEADS = 32
SSM_GROUPS = 4
HEADS_PER_GROUP = SSM_HEADS // SSM_GROUPS
D_STATE = 128
SSM_CONV = 4
D_BC = SSM_GROUPS * D_STATE
D_XBC = D_INNER + 2 * D_BC

ATTN_HEAD_DIM = 64
N_Q_HEADS = 16
N_KV_HEADS = 4
Q_PER_KV = N_Q_HEADS // N_KV_HEADS
D_ATTN = N_Q_HEADS * ATTN_HEAD_DIM
D_KV = N_KV_HEADS * ATTN_HEAD_DIM

D_FF = 2816
FFN_CONV = 3

RMS_EPS = 1e-6
LOG2E = math.log2(math.e)
NEG_INF = -1e30

LANES = 128
HALO = 8
REP = SSM_HEADS
assert 3 * REP <= LANES
VMEM_LIMIT = 56 * 1024 * 1024
BODY_TM = 4 * CHUNK
CHUNKS_PER_STEP = 4
IN_PROJ_TM = 8 * CHUNK
IN_PROJ_TN = 512
XBC_BLOCK = 1024
XBC_BLOCKS = D_XBC // XBC_BLOCK
assert D_INNER % XBC_BLOCK == 0 and D_XBC % XBC_BLOCK == 0
FFN_TN = 256


def _rms(x):
    return x * lax.rsqrt(jnp.mean(x * x, axis=-1, keepdims=True) + RMS_EPS)


def _silu(x):
    return x * jax.nn.sigmoid(x)


def _dot(a, b):
    return jnp.dot(a, b, preferred_element_type=F32)


def _dot_nt(a, b):
    return lax.dot_general(a, b, (((1,), (1,)), ((), ())), preferred_element_type=F32)


def _dot_tn(a, b):
    return lax.dot_general(a, b, (((0,), (0,)), ((), ())), preferred_element_type=F32)


def _resident(shape):
    return pl.BlockSpec(shape, lambda *_: (0,) * len(shape), pipeline_mode=pl.Buffered(1))


def _params(n_axes):
    return pltpu.CompilerParams(dimension_semantics=("arbitrary",) * n_axes,
                                vmem_limit_bytes=VMEM_LIMIT)


def _row_tile(head):
    return CHUNK if head else BODY_TM


def _chunks_per_step(nc):
    return CHUNKS_PER_STEP if nc % CHUNKS_PER_STEP == 0 else 1


def _conv_stage(u, first, init_ref, buf, carry_ref, tail_ref, col):
    tm, tn = u.shape
    cols = slice(col, col + tn)
    buf[0:HALO, :] = jnp.where(first, init_ref[:, cols], carry_ref[:, cols])
    buf[HALO:HALO + tm, :] = u
    carry_ref[:, cols] = u[tm - HALO:tm, :]
    if tail_ref is not None:
        tail_ref[:, cols] = u[tm - HALO:tm, :]


def _conv_finish(buf, cw_ref, cb_ref, col, taps, u=None):
    tm, tn = buf.shape[0] - HALO, buf.shape[1]
    cols = slice(col, col + tn)
    out = cb_ref[:, cols] + cw_ref[taps - 1:taps, cols] * (buf[HALO:HALO + tm, :] if u is None else u)
    for k in range(taps - 1):
        s = taps - 1 - k
        out = out + cw_ref[k:k + 1, cols] * buf[HALO - s:HALO - s + tm, :]
    return out


def _normed_input(x, gain_ref, head):
    xn = _rms(x) * gain_ref[...]
    if head:
        row = lax.broadcasted_iota(jnp.int32, (x.shape[0], 1), 0)
        xn = jnp.where(row >= PAD, xn, 0.0)
    return xn.astype(BF16)


def _in_proj_kernel(*refs, tiles_per_seq, tm, tn, head):
    x_ref, g_ref, wz_ref = refs[:3]
    wx_refs = refs[3:3 + XBC_BLOCKS]
    wdt_ref, cw_ref, cb_ref, dtb_ref, init_ref, zs_ref, xbc_ref, dt_ref = refs[3 + XBC_BLOCKS:11 + XBC_BLOCKS]
    tail_ref = refs[11 + XBC_BLOCKS] if head else None
    ext_ref, carry_ref = refs[-2:]
    first = (pl.program_id(0) % tiles_per_seq) == 0
    xn = _normed_input(x_ref[...], g_ref, head)
    valid = lax.broadcasted_iota(jnp.int32, (tm, 1), 0) >= PAD
    def z_block(n0):
        zs_ref[:, n0:n0 + tn] = _silu(_dot(xn, wz_ref[:, n0:n0 + tn])).astype(BF16)

    def stage(n0):
        w_ref, w0 = wx_refs[n0 // XBC_BLOCK], n0 % XBC_BLOCK
        _conv_stage(_dot(xn, w_ref[:, w0:w0 + tn]), first, init_ref, ext_ref.at[(n0 // tn) % 2], carry_ref,
                    tail_ref, n0)

    def finish(n0):
        y = _silu(_conv_finish(ext_ref.at[(n0 // tn) % 2], cw_ref, cb_ref, n0, SSM_CONV))
        if head and n0 >= D_INNER:
            y = jnp.where(valid, y, 0.0)
        xbc_ref[:, n0:n0 + tn] = y.astype(BF16)

    z_cols, x_cols = list(range(0, D_INNER, tn)), list(range(0, D_XBC, tn))
    stage(x_cols[0])
    for i, n0 in enumerate(x_cols):
        if i + 1 < len(x_cols):
            stage(x_cols[i + 1])
        finish(n0)
        if z_cols:
            z_block(z_cols.pop(0))
    for n0 in z_cols:
        z_block(n0)
    dt_raw = _dot(xn, wdt_ref[...]) + dtb_ref[...]
    dt = jnp.maximum(dt_raw, 0.0) + jnp.log1p(jnp.exp(-jnp.abs(dt_raw)))
    dt_ref[...] = jnp.where(valid, dt, 0.0) if head else dt


def _in_proj(h, gain, w_in, layer, wdt, conv_w, conv_b, dt_bias, conv_init, seq_rows, head):
    rows = h.shape[0]
    tm, tn = (CHUNK if head else IN_PROJ_TM), IN_PROJ_TN
    assert seq_rows % tm == 0 and D_INNER % tn == 0 and XBC_BLOCK % tn == 0
    row_spec = lambda n: pl.BlockSpec((tm, n), lambda i: (i, 0))
    w_block = lambda width, idx: pl.BlockSpec((None, D_MODEL, width), lambda i: (layer, 0, idx),
                                              pipeline_mode=pl.Buffered(1))
    out_specs = [row_spec(D_INNER), row_spec(D_XBC), row_spec(LANES)]
    out_shape = [jax.ShapeDtypeStruct((rows, D_INNER), BF16), jax.ShapeDtypeStruct((rows, D_XBC), BF16),
                 jax.ShapeDtypeStruct((rows, LANES), F32)]
    if head:
        out_specs.append(_resident((HALO, D_XBC)))
        out_shape.append(jax.ShapeDtypeStruct((HALO, D_XBC), F32))
    return pl.pallas_call(
        functools.partial(_in_proj_kernel, tiles_per_seq=seq_rows // tm, tm=tm, tn=tn, head=head),
        grid=(rows // tm,),
        in_specs=[row_spec(D_MODEL), _resident((1, D_MODEL)), w_block(D_INNER, 0)]
                 + [w_block(XBC_BLOCK, D_INNER // XBC_BLOCK + k) for k in range(XBC_BLOCKS)]
                 + [_resident(wdt.shape), _resident(conv_w.shape), _resident(conv_b.shape),
                    _resident(dt_bias.shape), _resident(conv_init.shape)],
        out_specs=out_specs,
        out_shape=out_shape,
        scratch_shapes=[pltpu.VMEM((2, HALO + tm, tn), F32), pltpu.VMEM((HALO, D_XBC), F32)],
        compiler_params=_params(1),
        name="mamba_in_proj",
    )(h, gain, w_in, *([w_in] * XBC_BLOCKS), wdt, conv_w, conv_b, dt_bias, conv_init)


def _split3(x):
    hi = x.astype(BF16).astype(F32)
    r = x - hi
    mid = r.astype(BF16).astype(F32)
    return hi, mid, r - mid


def _pack3(x, lane, fill):
    hi, mid, lo = _split3(x)
    packed = jnp.where(lane < REP, hi, jnp.where(lane < 2 * REP, mid, jnp.where(lane < 3 * REP, lo, fill)))
    return packed.astype(BF16)


def _ssd_kernel(*refs, head, cps):
    (zs_ref, xbc_ref, dt_ref, alog_ref, dskip_ref, gn_ref, tril3_ref, e3_ref, selc_ref, init_ref, o_ref) = refs[:11]
    final_ref = refs[11] if head else None
    state_ref, yg_ref, cb_ref, yoff_ref, seg_ref = refs[-5:]

    @pl.when(pl.program_id(1) == 0)
    def _():
        state_ref[...] = init_ref[...]

    for sub in range(cps):
        rows = pl.ds(sub * CHUNK, CHUNK)
        _ssd_chunk(zs_ref.at[rows], xbc_ref.at[rows], dt_ref.at[rows], alog_ref, dskip_ref, gn_ref, tril3_ref,
                   e3_ref, selc_ref, o_ref.at[rows], state_ref, yg_ref.at[rows], cb_ref, yoff_ref, seg_ref)
    if head:
        final_ref[...] = state_ref[...]


def _ssd_chunk(zs_ref, xbc_ref, dt_ref, alog_ref, dskip_ref, gn_ref, tril3_ref, e3_ref, selc_ref, o_ref,
               state_ref, yg_ref, cb_ref, yoff_ref, seg_ref):
    lane = lax.broadcasted_iota(jnp.int32, (CHUNK, LANES), 1)
    li = lax.broadcasted_iota(jnp.int32, (CHUNK, CHUNK), 0)
    cbias = jnp.where(li >= lane, 0.0, NEG_INF)
    cbias = jnp.concatenate([cbias] * HEADS_PER_GROUP, axis=1)

    dt = dt_ref[...]
    a = dt * (-jnp.exp(alog_ref[...]))
    a3 = jnp.concatenate([t.astype(BF16) for t in _split3(a)], axis=0)
    acs = _dot(tril3_ref[...], a3)
    b = jnp.maximum(jnp.log(dt), NEG_INF) - acs
    expa = jnp.exp(acs)
    wst = jnp.exp(acs[CHUNK - 1:CHUNK, :] + b)
    spread = _dot(jnp.concatenate([_pack3(expa, lane, 0.0), _pack3(wst, lane, 0.0)], axis=0), e3_ref[...])
    e_exp = spread[:CHUNK]
    w_exp = spread[CHUNK:]
    u3 = _pack3(acs * LOG2E, lane, 1.0)
    bt_terms = _split3((b * LOG2E).T)
    sub = lax.broadcasted_iota(jnp.int32, (HEADS_PER_GROUP, HEADS_PER_GROUP * CHUNK), 0)
    blk = lax.broadcasted_iota(jnp.int32, (HEADS_PER_GROUP, HEADS_PER_GROUP * CHUNK), 1) // CHUNK
    on_diag = sub == blk
    lane_lo = lane < SSM_HEAD_DIM
    ssq = jnp.zeros((CHUNK, 1), F32)
    gw = HEADS_PER_GROUP * SSM_HEAD_DIM

    def stage(g):
        gc = slice(g * gw, (g + 1) * gw)
        bg_b = xbc_ref[:, D_INNER + g * D_STATE:D_INNER + (g + 1) * D_STATE]
        cg_b = xbc_ref[:, D_INNER + D_BC + g * D_STATE:D_INNER + D_BC + (g + 1) * D_STATE]
        cb_ref[g % 2] = _dot_nt(cg_b, bg_b)
        st = state_ref[:, gc]
        yoff_ref[g % 2] = _dot(cg_b, st.astype(BF16))
        xw_b = (xbc_ref[:, gc].astype(F32) * w_exp[:, gc]).astype(BF16)
        state_ref[:, gc] = st * e_exp[CHUNK - 1:CHUNK, gc] + _dot_tn(bg_b, xw_b)
        dyn = [jnp.where(on_diag, jnp.concatenate([t[g * HEADS_PER_GROUP:(g + 1) * HEADS_PER_GROUP, :]]
                                                  * HEADS_PER_GROUP, axis=1), 0.0) for t in bt_terms]
        dyn.append(jnp.zeros_like(dyn[0]))
        v3 = jnp.concatenate([selc_ref[g], jnp.concatenate(dyn, axis=0).astype(BF16)], axis=0)
        seg_ref[g % 2] = _dot(u3, v3)

    stage(0)
    for g in range(SSM_GROUPS):
        if g + 1 < SSM_GROUPS:
            stage(g + 1)
        gc = slice(g * gw, (g + 1) * gw)
        cb_mat, y_off, seg = cb_ref[g % 2], yoff_ref[g % 2], seg_ref[g % 2]
        m = (jnp.exp2(seg + cbias) * jnp.concatenate([cb_mat] * HEADS_PER_GROUP, axis=1)).astype(BF16)
        y_parts = []
        for pr in range(HEADS_PER_GROUP // 2):
            c0 = g * gw + pr * LANES
            xp = xbc_ref[:, c0:c0 + LANES]
            zero = jnp.zeros_like(xp)
            rhs = jnp.concatenate([jnp.where(lane_lo, xp, zero), jnp.where(lane_lo, zero, xp)], axis=0)
            y_parts.append(_dot(m[:, pr * 2 * CHUNK:(pr + 1) * 2 * CHUNK], rhs))
        y = jnp.concatenate(y_parts, axis=1) + y_off * e_exp[:, gc] + dskip_ref[:, gc] * xbc_ref[:, gc].astype(F32)
        y = y * zs_ref[:, gc].astype(F32)
        ssq = ssq + jnp.sum(y * y, axis=-1, keepdims=True)
        yg_ref[:, gc] = y
    inv = lax.rsqrt(ssq * (1.0 / D_INNER) + RMS_EPS)
    o_ref[...] = (yg_ref[...] * inv * gn_ref[...]).astype(BF16)


def _ssd_constants():
    tril3 = np.tile(np.tril(np.ones((CHUNK, CHUNK), np.float32)), (1, 3))
    e3 = np.zeros((LANES, D_INNER), np.float32)
    selc = np.zeros((SSM_GROUPS, 3 * REP, HEADS_PER_GROUP * CHUNK), np.float32)
    for t in range(3):
        for h in range(SSM_HEADS):
            e3[REP * t + h, h * SSM_HEAD_DIM:(h + 1) * SSM_HEAD_DIM] = 1.0
            g, k = divmod(h, HEADS_PER_GROUP)
            selc[g, REP * t + h, k * CHUNK:(k + 1) * CHUNK] = 1.0
    return jnp.asarray(tril3, BF16), jnp.asarray(e3, BF16), jnp.asarray(selc, BF16)


def _ssd(zs, xbc, dt, a_log, d_skip, gate_norm, state_init, bsz, nc, head):
    rows = zs.shape[0]
    tril3, e3, selc = _ssd_constants()
    cps = _chunks_per_step(nc)
    steps = nc // cps
    row_spec = lambda n: pl.BlockSpec((cps * CHUNK, n), lambda b, c: (b * steps + c, 0))
    out_specs = [row_spec(D_INNER)]
    out_shape = [jax.ShapeDtypeStruct((rows, D_INNER), BF16)]
    if head:
        out_specs.append(_resident((D_STATE, D_INNER)))
        out_shape.append(jax.ShapeDtypeStruct((D_STATE, D_INNER), F32))
    return pl.pallas_call(
        functools.partial(_ssd_kernel, head=head, cps=cps),
        grid=(bsz, steps),
        in_specs=[row_spec(D_INNER), row_spec(D_XBC), row_spec(LANES),
                  _resident(a_log.shape), _resident(d_skip.shape), _resident(gate_norm.shape),
                  _resident(tril3.shape), _resident(e3.shape), _resident(selc.shape),
                  _resident(state_init.shape)],
        out_specs=out_specs,
        out_shape=out_shape,
        scratch_shapes=[pltpu.VMEM((D_STATE, D_INNER), F32),
                        pltpu.VMEM((cps * CHUNK, D_INNER), F32),
                        pltpu.VMEM((2, CHUNK, CHUNK), F32),
                        pltpu.VMEM((2, CHUNK, HEADS_PER_GROUP * SSM_HEAD_DIM), F32),
                        pltpu.VMEM((2, CHUNK, HEADS_PER_GROUP * CHUNK), F32)],
        compiler_params=_params(2),
        name="mamba_ssd",
    )(zs, xbc, dt, a_log, d_skip, gate_norm, tril3, e3, selc, state_init)


def _ffn_kernel(*refs, tiles_per_seq, tm, tn, head, mix_transposed):
    (res_ref, mix_ref, wmix_ref, gmix_ref, gpre_ref, wg_ref, wv_ref, cw_ref, cb_ref, wd_ref, gpost_ref,
     init_ref, o_ref) = refs[:13]
    tail_ref = refs[13] if head else None
    ext_ref, carry_ref, act_ref = refs[-3:]
    first = (pl.program_id(0) % tiles_per_seq) == 0
    part = tm // 2 if (tm // 2) % LANES == 0 else tm
    mixes = [_dot_tn(mix_ref[:, r0:r0 + part], wmix_ref[...]) if mix_transposed
             else _dot(mix_ref[r0:r0 + part, :], wmix_ref[...]) for r0 in range(0, tm, part)]
    x = jnp.concatenate([res_ref[r0:r0 + part, :] + _rms(m) * gmix_ref[...]
                         for r0, m in zip(range(0, tm, part), mixes)], axis=0)
    xn = _normed_input(x, gpre_ref, head)

    def up_conv(w_ref, col):
        u = _dot(xn, w_ref[:, col % D_FF:col % D_FF + tn])
        _conv_stage(u, first, init_ref, ext_ref, carry_ref, tail_ref, col)
        return _conv_finish(ext_ref, cw_ref, cb_ref, col, FFN_CONV, u)

    for j in range(D_FF // tn):
        gate = up_conv(wg_ref, j * tn)
        val = up_conv(wv_ref, D_FF + j * tn)
        act_ref[:, j * tn:(j + 1) * tn] = (_silu(gate) * val).astype(BF16)
    ys = [_dot(act_ref[r0:r0 + part, :], wd_ref[...]) for r0 in range(0, tm, part)]
    for r0, y in zip(range(0, tm, part), ys):
        o_ref[r0:r0 + part, :] = x[r0:r0 + part, :] + _rms(y) * gpost_ref[...]


def _ffn(res, mix, wmix, gmix, gpre, w_up, w_down, layer, conv_w, conv_b, gpost, conv_init, seq_rows, head,
         mix_transposed):
    rows = res.shape[0]
    k = wmix.shape[0]
    tm, tn = _row_tile(head), FFN_TN
    assert seq_rows % tm == 0 and D_FF % tn == 0
    row_spec = pl.BlockSpec((tm, D_MODEL), lambda i: (i, 0))
    mix_spec = pl.BlockSpec((k, tm), lambda i: (0, i)) if mix_transposed else pl.BlockSpec((tm, k), lambda i: (i, 0))
    up_half = lambda half: pl.BlockSpec((None, D_MODEL, D_FF), lambda i: (layer, 0, half), pipeline_mode=pl.Buffered(1))
    down_spec = pl.BlockSpec((None, D_FF, D_MODEL), lambda i: (layer, 0, 0), pipeline_mode=pl.Buffered(1))
    out_specs = [row_spec]
    out_shape = [jax.ShapeDtypeStruct((rows, D_MODEL), F32)]
    if head:
        out_specs.append(_resident((HALO, 2 * D_FF)))
        out_shape.append(jax.ShapeDtypeStruct((HALO, 2 * D_FF), F32))
    return pl.pallas_call(
        functools.partial(_ffn_kernel, tiles_per_seq=seq_rows // tm, tm=tm, tn=tn, head=head,
                          mix_transposed=mix_transposed),
        grid=(rows // tm,),
        in_specs=[row_spec, mix_spec, _resident(wmix.shape), _resident(gmix.shape), _resident(gpre.shape),
                  up_half(0), up_half(1), _resident(conv_w.shape), _resident(conv_b.shape), down_spec,
                  _resident(gpost.shape), _resident(conv_init.shape)],
        out_specs=out_specs,
        out_shape=out_shape,
        scratch_shapes=[pltpu.VMEM((HALO + tm, tn), F32),
                        pltpu.VMEM((HALO, 2 * D_FF), F32),
                        pltpu.VMEM((tm, D_FF), BF16)],
        compiler_params=pltpu.CompilerParams(
            dimension_semantics=("arbitrary",), vmem_limit_bytes=VMEM_LIMIT,
            allow_input_fusion=[i in (5, 6, 9) for i in range(12)]),
        name="mix_proj_conv_ffn",
    )(res, mix, wmix, gmix, gpre, w_up, w_up, conv_w, conv_b, w_down, gpost, conv_init)


def _qkv_proj_kernel(x_ref, gq_ref, gkv_ref, wq_ref, wk_ref, wv_ref, qt_ref, k_ref, vt_ref):
    xr = _rms(x_ref[...])
    xq = (xr * gq_ref[...]).astype(BF16)
    xkv = (xr * gkv_ref[...]).astype(BF16)
    scale = 1.0 / math.sqrt(ATTN_HEAD_DIM)
    qt_ref[...] = (_dot(xq, wq_ref[...]) * scale).T.astype(BF16)
    k_ref[...] = _dot(xkv, wk_ref[...]).astype(BF16)
    vt_ref[...] = _dot(xkv, wv_ref[...]).T.astype(BF16)


def _qkv_proj(h, gq, gkv, wq, wkv, head):
    rows = h.shape[0]
    tm = CHUNK if head else IN_PROJ_TM
    return pl.pallas_call(
        _qkv_proj_kernel,
        grid=(rows // tm,),
        in_specs=[pl.BlockSpec((tm, D_MODEL), lambda i: (i, 0)), _resident(gq.shape), _resident(gkv.shape),
                  _resident(wq.shape), pl.BlockSpec((D_MODEL, D_KV), lambda i: (0, 0), pipeline_mode=pl.Buffered(1)),
                  pl.BlockSpec((D_MODEL, D_KV), lambda i: (0, 1), pipeline_mode=pl.Buffered(1))],
        out_specs=[pl.BlockSpec((D_ATTN, tm), lambda i: (0, i)), pl.BlockSpec((tm, D_KV), lambda i: (i, 0)),
                   pl.BlockSpec((D_KV, tm), lambda i: (0, i))],
        out_shape=[jax.ShapeDtypeStruct((D_ATTN, rows), BF16), jax.ShapeDtypeStruct((rows, D_KV), BF16),
                   jax.ShapeDtypeStruct((D_KV, rows), BF16)],
        compiler_params=_params(1),
        name="qkv_proj",
    )(h, gq, gkv, wq, wkv, wkv)


def _attn_kernel(sink_ref, qt_ref, kc_ref, kp_ref, km_ref, vc_ref, vp_ref, vm_ref, ot_ref, s_ref, *, head, cps):
    def views(sub):
        rows = pl.ds(sub * CHUNK, CHUNK)
        if sub == 0:
            return rows, kp_ref, vp_ref, pl.program_id(1) >= 1
        before = pl.ds((sub - 1) * CHUNK, CHUNK)
        return rows, kc_ref.at[before], vc_ref.at[:, before], True

    def scores(sub):
        rows, k_prev, _, _ = views(sub)
        _attn_scores(qt_ref.at[:, rows], kc_ref.at[rows], k_prev, km_ref, s_ref.at[sub % 2])

    def finish(sub):
        rows, _, v_prev, has_prev = views(sub)
        _attn_finish(sink_ref, s_ref.at[sub % 2], vc_ref.at[:, rows], v_prev, vm_ref, ot_ref.at[:, rows], has_prev, head)

    scores(0)
    for sub in range(cps):
        if sub + 1 < cps:
            scores(sub + 1)
        finish(sub)


def _attn_scores(qt_ref, kc_ref, kp_ref, km_ref, s_ref):
    k_all = jnp.concatenate([km_ref[PAD:CHUNK, :], kp_ref[...], kc_ref[...]], axis=0)
    qw = Q_PER_KV * CHUNK
    zeros = jnp.zeros((ATTN_HEAD_DIM, qw), BF16)

    def q_heads(kh):
        return jnp.concatenate([qt_ref[hq * ATTN_HEAD_DIM:(hq + 1) * ATTN_HEAD_DIM, :]
                                for hq in range(kh * Q_PER_KV, (kh + 1) * Q_PER_KV)], axis=1)

    for pair in range(N_KV_HEADS // 2):
        qbd = jnp.concatenate([jnp.concatenate([q_heads(2 * pair), zeros], axis=1),
                               jnp.concatenate([zeros, q_heads(2 * pair + 1)], axis=1)], axis=0)
        s_ref[:, pair * 2 * qw:(pair + 1) * 2 * qw] = _dot(k_all[:, pair * LANES:(pair + 1) * LANES], qbd)


def _attn_finish(sink_ref, s_ref, vc_ref, vp_ref, vm_ref, ot_ref, has_prev, head):
    nkeys = N_META + 2 * CHUNK
    ki = lax.broadcasted_iota(jnp.int32, (nkeys, CHUNK), 0)
    qi = lax.broadcasted_iota(jnp.int32, (nkeys, CHUNK), 1)
    kprev = ki - N_META
    kcur = ki - (N_META + CHUNK)
    in_cur = jnp.logical_and(kcur >= 0, kcur <= qi)
    if head:
        visible = jnp.logical_and(in_cur, kcur >= PAD)
    else:
        in_prev = jnp.logical_and(jnp.logical_and(kprev >= 0, kprev < CHUNK), jnp.logical_and(kprev > qi, has_prev))
        visible = jnp.logical_or(jnp.logical_or(ki < N_META, in_prev), in_cur)
    bias = jnp.concatenate([jnp.where(visible, 0.0, NEG_INF)] * N_Q_HEADS, axis=1)
    qw = Q_PER_KV * CHUNK
    s = s_ref[...] + bias
    sink = jnp.concatenate([jnp.full((1, CHUNK), sink_ref[hq], F32) for hq in range(N_Q_HEADS)], axis=1)
    m = jnp.maximum(jnp.max(s, axis=0, keepdims=True), sink)
    p = jnp.exp(s - m)
    inv_den = 1.0 / (jnp.sum(p, axis=0, keepdims=True) + jnp.exp(sink - m))
    pb = p.astype(BF16)
    pad_rows = jnp.zeros((PAD, qw), BF16)
    for kh in range(N_KV_HEADS):
        rows = slice(kh * ATTN_HEAD_DIM, (kh + 1) * ATTN_HEAD_DIM)
        vt = jnp.concatenate([vm_ref[rows, :], vp_ref[rows, :], vc_ref[rows, :]], axis=1)
        p_all = jnp.concatenate([pad_rows, pb[:, kh * qw:(kh + 1) * qw]], axis=0)
        o = _dot(vt, p_all) * inv_den[:, kh * qw:(kh + 1) * qw]
        for g in range(Q_PER_KV):
            hq = kh * Q_PER_KV + g
            ot_ref[hq * ATTN_HEAD_DIM:(hq + 1) * ATTN_HEAD_DIM, :] = o[:, g * CHUNK:(g + 1) * CHUNK].astype(BF16)


def _attn(qt, k, vt, k_head, vt_head, sinks, bsz, nc, head):
    rows = qt.shape[1]
    cps = _chunks_per_step(nc)
    steps = nc // cps
    cur = lambda b, c: b * steps + c
    prev = lambda b, c: b * nc + jnp.maximum(c * cps - 1, 0)
    return pl.pallas_call(
        functools.partial(_attn_kernel, head=head, cps=cps),
        grid=(bsz, steps),
        in_specs=[pl.BlockSpec(memory_space=pltpu.SMEM),
                  pl.BlockSpec((D_ATTN, cps * CHUNK), lambda b, c: (0, cur(b, c))),
                  pl.BlockSpec((cps * CHUNK, D_KV), lambda b, c: (cur(b, c), 0)),
                  pl.BlockSpec((CHUNK, D_KV), lambda b, c: (prev(b, c), 0)), _resident(k_head.shape),
                  pl.BlockSpec((D_KV, cps * CHUNK), lambda b, c: (0, cur(b, c))),
                  pl.BlockSpec((D_KV, CHUNK), lambda b, c: (0, prev(b, c))), _resident(vt_head.shape)],
        out_specs=pl.BlockSpec((D_ATTN, cps * CHUNK), lambda b, c: (0, cur(b, c))),
        out_shape=jax.ShapeDtypeStruct((D_ATTN, rows), BF16),
        scratch_shapes=[pltpu.VMEM((2, N_META + 2 * CHUNK, N_Q_HEADS * CHUNK), F32)],
        compiler_params=_params(2),
        name="swa_sink_attention",
    )(sinks, qt, k, k, k_head, vt, vt, vt_head)


def kernel(x, meta_tokens, a_norm_pre, a_w_in, a_conv_w, a_conv_b, a_dt_bias, a_a_log, a_d_skip, a_gate_norm, a_w_out, a_norm_post, kv_norm, w_kv, b_norm_pre, b_w_q, b_sinks, b_w_o, b_norm_post, f_norm_pre, f_w_up, f_conv_w, f_conv_b, f_w_down, f_norm_post):
    bsz, seq, _ = x.shape
    assert seq % BODY_TM == 0
    depth = f_norm_pre.shape[0]
    n_a = a_norm_pre.shape[0]

    def row(v):
        return v.astype(F32).reshape(1, -1)

    def rep3(v):
        return jnp.pad(jnp.concatenate([v] * 3, axis=1), ((0, 0), (0, LANES - 3 * REP)))

    passes = {True: (1, CHUNK, 1), False: (bsz, seq, seq // CHUNK)}
    hs = {True: jnp.concatenate([jnp.zeros((PAD, D_MODEL), F32), meta_tokens.astype(F32)], axis=0),
          False: x.astype(F32).reshape(bsz * seq, D_MODEL)}

    w_in_b, w_up_b, w_down_b = a_w_in.astype(BF16), f_w_up.astype(BF16), f_w_down.astype(BF16)
    kv = {}
    for i in range(depth):
        mixes = {}
        if i < n_a:
            j = i
            wdt = rep3(a_w_in[j][:, D_INNER + D_XBC:]).astype(BF16)
            w_mix, g_mix, mix_transposed = a_w_out[j].astype(BF16), row(a_norm_post[j]), False
            conv_init = jnp.zeros((HALO, D_XBC), F32)
            state_init = jnp.zeros((D_STATE, D_INNER), F32)
            for head in (True, False):
                nseq, seq_rows, nc = passes[head]
                outs = _in_proj(hs[head], row(a_norm_pre[j]), w_in_b, j, wdt, a_conv_w[j].astype(F32),
                                row(a_conv_b[j]), rep3(row(a_dt_bias[j])), conv_init, seq_rows, head)
                ys = _ssd(outs[0], outs[1], outs[2], rep3(row(a_a_log[j])),
                          row(jnp.repeat(a_d_skip[j], SSM_HEAD_DIM)), row(a_gate_norm[j]), state_init, nseq, nc, head)
                mixes[head] = ys[0]
                if head:
                    conv_init, state_init = outs[3], ys[1]
        else:
            j = i - n_a
            wq, wkv = b_w_q[j].astype(BF16), w_kv.astype(BF16)
            w_mix, g_mix, mix_transposed = b_w_o[j].astype(BF16), row(b_norm_post[j]), True
            for head in (True, False):
                nseq, seq_rows, nc = passes[head]
                qt, k_new, vt_new = _qkv_proj(hs[head], row(b_norm_pre[j]), row(kv_norm), wq, wkv, head)
                if j == 0:
                    kv[head] = (k_new, vt_new)
                mixes[head] = _attn(qt, kv[head][0], kv[head][1], kv[True][0], kv[True][1],
                                    b_sinks[j].astype(F32), nseq, nc, head)
        conv_init = jnp.zeros((HALO, 2 * D_FF), F32)
        for head in (True, False):
            nseq, seq_rows, nc = passes[head]
            outs = _ffn(hs[head], mixes[head], w_mix, g_mix, row(f_norm_pre[i]), w_up_b, w_down_b, i,
                        f_conv_w[i].astype(F32), row(f_conv_b[i]), row(f_norm_post[i]), conv_init, seq_rows, head,
                        mix_transposed)
            hs[head] = outs[0]
            if head:
                conv_init = outs[1]
    return hs[False].reshape(bsz, seq, D_MODEL)
```
